```python
import jax, jax.numpy as jnp
from jax import lax
import numpy as np

D_MODEL = 1024
BATCH = 8
SEQ = 2048
DEPTH = 2
DEC_BATCH = 32
DEC_SEQ = 4
PAST_LEN = 8192
PAGE_SIZE = 128

HEAD_DIM = 64
EPS = 1e-6
ROPE_THETA = 10000.0
NEG_INF = -1e30
BIG = 1e9
FOX_HEADS = 8
FOX_WIDTH = FOX_HEADS * HEAD_DIM
FOX_QBLOCK = 128
FOX_GATE_BIAS = 4.0
POOL_WINDOWS = (2, 4, 8, 16)
POOL_GROUPS = len(POOL_WINDOWS)
POOL_WIDTH = D_MODEL - FOX_WIDTH
POOL_GROUP_CH = POOL_WIDTH // POOL_GROUPS
POOL_BUF = max(POOL_WINDOWS) - 1
AB_IN = 3 * FOX_WIDTH + FOX_HEADS + POOL_WIDTH
NSA_HEADS = D_MODEL // HEAD_DIM
NSA_KV_HEADS = 4
NSA_KV_WIDTH = NSA_KV_HEADS * HEAD_DIM
NSA_BRANCHES = 3
NSA_IN = NSA_HEADS * HEAD_DIM + 6 * NSA_KV_WIDTH + NSA_BRANCHES * NSA_HEADS
CMP_BLOCK = 32
CMP_STRIDE = 16
CMP_HIDDEN = 256
SEL_BLOCK = 64
SEL_TOPK = 16
WINDOW = 512
NSA_QBLOCK = 32
D_FF = ((8 * D_MODEL + 3 * 256 - 1) // (3 * 256)) * 256
N_AB = (DEPTH + 1) // 2
N_NSA = DEPTH // 2
STATE_NAMES = ('fox_k', 'fox_v', 'fox_logf', 'pool', 'nsa_kc', 'nsa_vc', 'nsa_ks', 'nsa_vs', 'nsa_kw', 'nsa_vw')

kernel_name = 'fox_pool_nsa_adaln_decoder_step'


def rms_norm(x, gain):
    xf = x.astype(jnp.float32)
    xf = xf * lax.rsqrt(jnp.mean(xf * xf, axis=-1, keepdims=True) + EPS)
    return (xf * gain.astype(jnp.float32)).astype(x.dtype)


def rope(x, pos):
    half = HEAD_DIM // 2
    inv_freq = ROPE_THETA ** (-jnp.arange(half, dtype=jnp.float32) / half)
    ang = pos.astype(jnp.float32)[:, None] * inv_freq[None, :]
    cos = jnp.cos(ang)[None, :, None, :]
    sin = jnp.sin(ang)[None, :, None, :]
    xf = x.astype(jnp.float32)
    x1, x2 = xf[..., :half], xf[..., half:]
    return jnp.concatenate([x1 * cos - x2 * sin, x2 * cos + x1 * sin], axis=-1).astype(x.dtype)


def masked_softmax(logits, mask):
    p = jax.nn.softmax(jnp.where(mask, logits, NEG_INF), axis=-1)
    return jnp.where(jnp.any(mask, axis=-1, keepdims=True), p, 0.0)


def query_blocks(n_q, qblock):
    qb = qblock if n_q % qblock == 0 else n_q
    return qb, n_q // qb


def merge_blocks(out):
    out = jnp.moveaxis(out, 0, 1)
    return out.reshape((out.shape[0], out.shape[1] * out.shape[2]) + out.shape[3:])


def gather_pages(pool, page_table):
    g = pool[page_table]
    return g.reshape((page_table.shape[0], page_table.shape[1] * pool.shape[1]) + pool.shape[2:])


def swiglu(h, w_up, w_down):
    gate, up = jnp.split(h @ w_up, 2, axis=-1)
    return (jax.nn.silu(gate) * up) @ w_down


def fox_attention(q, k, v, logf, pos0):
    B, T = q.shape[:2]
    L = k.shape[1]
    cum = jnp.cumsum(logf.astype(jnp.float32), axis=1).transpose(0, 2, 1)
    cum_q = cum[:, :, pos0:pos0 + T]
    k_pos = jnp.arange(L)
    qb, nb = query_blocks(T, FOX_QBLOCK)
    scale = HEAD_DIM ** -0.5

    def block(i):
        s0 = i * qb
        qi = lax.dynamic_slice_in_dim(q, s0, qb, axis=1)
        ci = lax.dynamic_slice_in_dim(cum_q, s0, qb, axis=2)
        q_pos = pos0 + s0 + jnp.arange(qb)
        logits = jnp.einsum('bqhd,bkhd->bhqk', qi, k, preferred_element_type=jnp.float32) * scale
        logits = logits + ci[..., :, None] - cum[:, :, None, :]
        mask = (k_pos[None, :] <= q_pos[:, None])[None, None]
        p = masked_softmax(logits, mask)
        return jnp.einsum('bhqk,bkhd->bqhd', p.astype(v.dtype), v)

    return merge_blocks(lax.map(block, jnp.arange(nb)))


def pool_mixer(u_ext, pos0, w_map, scale):
    B, n_ext, C = u_ext.shape
    T = n_ext - POOL_BUF
    cs = jnp.cumsum(u_ext.astype(jnp.float32), axis=1)
    cs = jnp.concatenate([jnp.zeros((B, 1, C), jnp.float32), cs], axis=1)
    end = cs[:, POOL_BUF + 1:]
    u_new = u_ext[:, POOL_BUF:].astype(jnp.float32)
    q_pos = pos0 + jnp.arange(T)
    groups = []
    for g, w in enumerate(POOL_WINDOWS):
        lo, hi = g * POOL_GROUP_CH, (g + 1) * POOL_GROUP_CH
        start = cs[:, POOL_BUF + 1 - w:POOL_BUF + 1 - w + T, lo:hi]
        count = jnp.minimum(w, q_pos + 1).astype(jnp.float32)[None, :, None]
        groups.append((end[..., lo:hi] - start) / count - u_new[..., lo:hi])
    d = jnp.stack(groups, axis=2)
    y = jnp.einsum('btgc,gce->btge', d, w_map.astype(jnp.float32)).reshape(B, T, C)
    return (y * scale.astype(jnp.float32)).astype(u_ext.dtype)


def compress(x, pos_emb, w1, w2):
    B, L, G, hd = x.shape
    n_chunk = L // CMP_STRIDE
    ch = x[:, :n_chunk * CMP_STRIDE].reshape(B, n_chunk, CMP_STRIDE, G, hd)
    ch = ch.transpose(0, 1, 3, 2, 4).reshape(B, n_chunk, G, CMP_STRIDE * hd)
    half = CMP_STRIDE * hd
    hidden = (ch[:, :-1] @ w1[:half] + ch[:, 1:] @ w1[half:]) + pos_emb.reshape(-1) @ w1
    return jax.nn.gelu(hidden) @ w2


def nsa_attention(q, gates, kc, vc, ks, vs, kw_ext, vw_ext, pos0, cmp_k, cmp_v):
    B, T, H, hd = q.shape
    L = kc.shape[1]
    G = NSA_KV_HEADS
    R = H // G
    scale = hd ** -0.5
    kcmp = compress(kc, *cmp_k)
    vcmp = compress(vc, *cmp_v)
    n_cmp = kcmp.shape[1]
    cmp_start = jnp.arange(n_cmp) * CMP_STRIDE
    cmp_end = cmp_start + CMP_BLOCK - 1
    n_sel = -(-L // SEL_BLOCK)
    pad = n_sel * SEL_BLOCK - L
    kb = jnp.pad(ks, ((0, 0), (0, pad), (0, 0), (0, 0))).reshape(B, n_sel, SEL_BLOCK, G, hd).transpose(0, 3, 1, 2, 4)
    vb = jnp.pad(vs, ((0, 0), (0, pad), (0, 0), (0, 0))).reshape(B, n_sel, SEL_BLOCK, G, hd).transpose(0, 3, 1, 2, 4)
    sel_start = jnp.arange(n_sel) * SEL_BLOCK
    blk_id = jnp.arange(n_sel)
    overlap = ((cmp_start[:, None] < sel_start[None, :] + SEL_BLOCK) & (cmp_end[:, None] >= sel_start[None, :])).astype(jnp.float32)
    top = min(SEL_TOPK, n_sel)
    w_pad = kw_ext.shape[1] - T
    qb, nb = query_blocks(T, NSA_QBLOCK)
    b_idx = jnp.arange(B)[:, None, None, None]
    g_idx = jnp.arange(G)[None, None, :, None]

    def block(i):
        s0 = i * qb
        qi = lax.dynamic_slice_in_dim(q, s0, qb, axis=1).reshape(B, qb, G, R, hd)
        q_pos = pos0 + s0 + jnp.arange(qb)
        lc = jnp.einsum('bqgrd,bcgd->bqgrc', qi, kcmp, preferred_element_type=jnp.float32) * scale
        pc = masked_softmax(lc, (cmp_end[None, :] <= q_pos[:, None])[None, :, None, None, :])
        o_cmp = jnp.einsum('bqgrc,bcgd->bqgrd', pc.astype(vcmp.dtype), vcmp)
        imp = jnp.einsum('bqgrc,cj->bqgj', pc, overlap)
        forced = (blk_id[None, :] == 0) | (blk_id[None, :] == (q_pos // SEL_BLOCK)[:, None])
        valid = sel_start[None, :] <= q_pos[:, None]
        score = jnp.where(valid[None, :, None, :], jnp.where(forced[None, :, None, :], BIG, imp), -BIG)
        _, idx = lax.top_k(score, top)
        k_sel = kb[b_idx, g_idx, idx].reshape(B, qb, G, top * SEL_BLOCK, hd)
        v_sel = vb[b_idx, g_idx, idx].reshape(B, qb, G, top * SEL_BLOCK, hd)
        tok_pos = (idx[..., None] * SEL_BLOCK + jnp.arange(SEL_BLOCK)).reshape(B, qb, G, 1, top * SEL_BLOCK)
        ls = jnp.einsum('bqgrd,bqgnd->bqgrn', qi, k_sel, preferred_element_type=jnp.float32) * scale
        ps = masked_softmax(ls, tok_pos <= q_pos[None, :, None, None, None])
        o_sel = jnp.einsum('bqgrn,bqgnd->bqgrd', ps.astype(v_sel.dtype), v_sel)
        kw = lax.dynamic_slice_in_dim(kw_ext, s0, w_pad + qb, axis=1)
        vw = lax.dynamic_slice_in_dim(vw_ext, s0, w_pad + qb, axis=1)
        k_pos = pos0 - w_pad + s0 + jnp.arange(w_pad + qb)
        mw = (k_pos[None, :] <= q_pos[:, None]) & (k_pos[None, :] > q_pos[:, None] - WINDOW) & (k_pos[None, :] >= 0)
        lw = jnp.einsum('bqgrd,bkgd->bqgrk', qi, kw, preferred_element_type=jnp.float32) * scale
        pw = masked_softmax(lw, mw[None, :, None, None, :])
        o_win = jnp.einsum('bqgrk,bkgd->bqgrd', pw.astype(vw.dtype), vw)
        gi = lax.dynamic_slice_in_dim(gates, s0, qb, axis=1).reshape(B, qb, G, R, NSA_BRANCHES)
        o = gi[..., 0:1] * o_cmp + gi[..., 1:2] * o_sel + gi[..., 2:3] * o_win
        return o.reshape(B, qb, H, hd).astype(q.dtype)

    return merge_blocks(lax.map(block, jnp.arange(nb)))


def mix_ab(h, pos0, k_past, v_past, lf_past, pool_prefix, w_in, b_fgate, q_norm, k_norm, pool_map, pool_scale, w_out):
    B, T, _ = h.shape
    q, k, v, f, u = jnp.split(h @ w_in, [FOX_WIDTH, 2 * FOX_WIDTH, 3 * FOX_WIDTH, 3 * FOX_WIDTH + FOX_HEADS], axis=-1)
    q = rms_norm(q.reshape(B, T, FOX_HEADS, HEAD_DIM), q_norm)
    k = rms_norm(k.reshape(B, T, FOX_HEADS, HEAD_DIM), k_norm)
    v = v.reshape(B, T, FOX_HEADS, HEAD_DIM)
    logf = jax.nn.log_sigmoid((f + b_fgate).astype(jnp.float32)).astype(h.dtype)
    if k_past is None:
        k_all, v_all, lf_all = k, v, logf
    else:
        k_all = jnp.concatenate([k_past, k], axis=1)
        v_all = jnp.concatenate([v_past, v], axis=1)
        lf_all = jnp.concatenate([lf_past, logf], axis=1)
    o_fox = fox_attention(q, k_all, v_all, lf_all, pos0).reshape(B, T, FOX_WIDTH)
    u_ext = jnp.concatenate([pool_prefix, u], axis=1)
    o_pool = pool_mixer(u_ext, pos0, pool_map, pool_scale)
    y = jnp.concatenate([o_fox, o_pool], axis=-1) @ w_out
    return y, k, v, logf, u_ext[:, -POOL_BUF:]


def mix_nsa(h, pos0, past_rows, kw_prefix, vw_prefix, buf_len, w_in, b_gate, q_norm, k_norm, cmp_k, cmp_v, w_out):
    B, T, _ = h.shape
    G, H, hd = NSA_KV_HEADS, NSA_HEADS, HEAD_DIM
    splits = [H * hd + i * NSA_KV_WIDTH for i in range(0, 7)]
    q, kc, vc, ks, vs, kw, vw, gl = jnp.split(h @ w_in, splits, axis=-1)
    pos = pos0 + jnp.arange(T)

    def kv(a):
        return a.reshape(B, T, G, hd)

    q = rope(rms_norm(q.reshape(B, T, H, hd), q_norm), pos)
    kc = rope(rms_norm(kv(kc), k_norm[0]), pos)
    ks = rope(rms_norm(kv(ks), k_norm[1]), pos)
    kw = rope(rms_norm(kv(kw), k_norm[2]), pos)
    vc, vs, vw = kv(vc), kv(vs), kv(vw)
    gates = jax.nn.sigmoid((gl + b_gate).astype(jnp.float32)).reshape(B, T, H, NSA_BRANCHES)
    if past_rows is None:
        kc_all, vc_all, ks_all, vs_all = kc, vc, ks, vs
    else:
        kc_p, vc_p, ks_p, vs_p = past_rows
        kc_all = jnp.concatenate([kc_p, kc], axis=1)
        vc_all = jnp.concatenate([vc_p, vc], axis=1)
        ks_all = jnp.concatenate([ks_p, ks], axis=1)
        vs_all = jnp.concatenate([vs_p, vs], axis=1)
    kw_ext = jnp.concatenate([kw_prefix, kw], axis=1)
    vw_ext = jnp.concatenate([vw_prefix, vw], axis=1)
    o = nsa_attention(q, gates, kc_all, vc_all, ks_all, vs_all, kw_ext, vw_ext, pos0, cmp_k, cmp_v)
    y = o.reshape(B, T, H * hd) @ w_out
    return y, kc, vc, ks, vs, kw_ext[:, -buf_len:], vw_ext[:, -buf_len:]


def run_trunk(x, c, pos0, P, past, page_table):
    B, T, _ = x.shape
    st = {n: [] for n in STATE_NAMES}
    c_act = jax.nn.silu(c)
    for layer in range(DEPTH):
        mod = c_act @ P['w_mod'][layer] + P['b_mod'][layer]
        sh1, sc1, g1, sh2, sc2, g2 = [m[:, None, :] for m in jnp.split(mod, 6, axis=-1)]
        h = rms_norm(x, P['norm_mix'][layer]) * (1.0 + sc1) + sh1
        if layer % 2 == 0:
            e = layer // 2
            if past is None:
                k_past = v_past = lf_past = None
                pool_prefix = jnp.zeros((B, POOL_BUF, POOL_WIDTH), x.dtype)
            else:
                k_past = gather_pages(past['cache_fox_k'][e], page_table)
                v_past = gather_pages(past['cache_fox_v'][e], page_table)
                lf_past = gather_pages(past['cache_fox_logf'][e], page_table)
                pool_prefix = past['state_pool'][e]
            y, k_new, v_new, lf_new, pool_new = mix_ab(
                h, pos0, k_past, v_past, lf_past, pool_prefix, P['ab_w_in'][e], P['ab_b_fgate'][e],
                P['ab_q_norm'][e], P['ab_k_norm'][e], P['ab_pool_map'][e], P['ab_pool_scale'][e], P['ab_w_out'][e])
            for n, a in zip(STATE_NAMES[:4], (k_new, v_new, lf_new, pool_new)):
                st[n].append(a)
        else:
            o = layer // 2
            if past is None:
                past_rows = None
                kw_prefix = jnp.zeros((B, WINDOW, NSA_KV_HEADS, HEAD_DIM), x.dtype)
                vw_prefix = kw_prefix
                buf_len = min(WINDOW, T)
            else:
                past_rows = tuple(gather_pages(past[n][o], page_table) for n in ('cache_nsa_kc', 'cache_nsa_vc', 'cache_nsa_ks', 'cache_nsa_vs'))
                kw_prefix = past['state_nsa_kw'][o]
                vw_prefix = past['state_nsa_vw'][o]
                buf_len = kw_prefix.shape[1]
            y, *new_rows = mix_nsa(
                h, pos0, past_rows, kw_prefix, vw_prefix, buf_len, P['nsa_w_in'][o], P['nsa_b_gate'][o],
                P['nsa_q_norm'][o], P['nsa_k_norm'][o],
                (P['nsa_cmp_pos_k'][o], P['nsa_cmp_w1_k'][o], P['nsa_cmp_w2_k'][o]),
                (P['nsa_cmp_pos_v'][o], P['nsa_cmp_w1_v'][o], P['nsa_cmp_w2_v'][o]), P['nsa_w_out'][o])
            for n, a in zip(STATE_NAMES[4:], new_rows):
                st[n].append(a)
        x = x + g1 * y
        h = rms_norm(x, P['norm_ffn'][layer]) * (1.0 + sc2) + sh2
        x = x + g2 * swiglu(h, P['w_up'][layer], P['w_down'][layer])
    return x, [jnp.stack(st[n]) for n in STATE_NAMES]


def setup_inputs(seed: int = 0) -> dict:
    key = jax.random.key(seed)
    keys = iter(jax.random.split(key, 64))

    def nrm(shape, scale=1.0):
        return jax.random.normal(next(keys), shape, jnp.float32) * scale

    D = D_MODEL
    n_pages = PAST_LEN // PAGE_SIZE
    n_pool = (5 * DEC_BATCH * n_pages + 3) // 4
    wbuf = min(WINDOW, PAST_LEN)
    x_prompt = nrm((BATCH, SEQ, D))
    x_sample = nrm((DEC_BATCH, DEC_SEQ, D))
    cache_fox_k = nrm((N_AB, n_pool, PAGE_SIZE, FOX_HEADS, HEAD_DIM))
    cache_fox_v = nrm((N_AB, n_pool, PAGE_SIZE, FOX_HEADS, HEAD_DIM))
    cache_fox_logf = jax.nn.log_sigmoid(FOX_GATE_BIAS + nrm((N_AB, n_pool, PAGE_SIZE, FOX_HEADS), 1.0))
    state_pool = nrm((N_AB, DEC_BATCH, POOL_BUF, POOL_WIDTH))
    cache_nsa_kc = nrm((N_NSA, n_pool, PAGE_SIZE, NSA_KV_HEADS, HEAD_DIM))
    cache_nsa_vc = nrm((N_NSA, n_pool, PAGE_SIZE, NSA_KV_HEADS, HEAD_DIM))
    cache_nsa_ks = nrm((N_NSA, n_pool, PAGE_SIZE, NSA_KV_HEADS, HEAD_DIM))
    cache_nsa_vs = nrm((N_NSA, n_pool, PAGE_SIZE, NSA_KV_HEADS, HEAD_DIM))
    state_nsa_kw = nrm((N_NSA, DEC_BATCH, wbuf, NSA_KV_HEADS, HEAD_DIM))
    state_nsa_vw = nrm((N_NSA, DEC_BATCH, wbuf, NSA_KV_HEADS, HEAD_DIM))
    page_table = jax.random.permutation(next(keys), n_pool)[:DEC_BATCH * n_pages].reshape(DEC_BATCH, n_pages).astype(jnp.int32)
    c_prompt = nrm((BATCH, D))
    c_sample = nrm((DEC_BATCH, D))
    return {
        'x_prompt': x_prompt, 'x_sample': x_sample,
        'cache_fox_k': cache_fox_k, 'cache_fox_v': cache_fox_v, 'cache_fox_logf': cache_fox_logf,
        'state_pool': state_pool,
        'cache_nsa_kc': cache_nsa_kc, 'cache_nsa_vc': cache_nsa_vc, 'cache_nsa_ks': cache_nsa_ks, 'cache_nsa_vs': cache_nsa_vs,
        'state_nsa_kw': state_nsa_kw, 'state_nsa_vw': state_nsa_vw,
        'page_table': page_table, 'c_prompt': c_prompt, 'c_sample': c_sample,
        'w_mod': nrm((DEPTH, D, 6 * D), 0.5 * D ** -0.5), 'b_mod': nrm((DEPTH, 6 * D), 0.02),
        'norm_mix': 1.0 + nrm((DEPTH, D), 0.02), 'norm_ffn': 1.0 + nrm((DEPTH, D), 0.02),
        'w_up': nrm((DEPTH, D, 2 * D_FF), D ** -0.5), 'w_down': nrm((DEPTH, D_FF, D), D_FF ** -0.5),
        'ab_w_in': nrm((N_AB, D, AB_IN), D ** -0.5), 'ab_b_fgate': FOX_GATE_BIAS + nrm((N_AB, FOX_HEADS), 0.5),
        'ab_q_norm': 1.0 + nrm((N_AB, HEAD_DIM), 0.02), 'ab_k_norm': 1.0 + nrm((N_AB, HEAD_DIM), 0.02),
        'ab_pool_map': nrm((N_AB, POOL_GROUPS, POOL_GROUP_CH, POOL_GROUP_CH), POOL_GROUP_CH ** -0.5),
        'ab_pool_scale': 1.0 + nrm((N_AB, POOL_WIDTH), 0.1),
        'ab_w_out': nrm((N_AB, FOX_WIDTH + POOL_WIDTH, D), (FOX_WIDTH + POOL_WIDTH) ** -0.5),
        'nsa_w_in': nrm((N_NSA, D, NSA_IN), D ** -0.5), 'nsa_b_gate': nrm((N_NSA, NSA_BRANCHES * NSA_HEADS), 0.02),
        'nsa_q_norm': 1.0 + nrm((N_NSA, HEAD_DIM), 0.02), 'nsa_k_norm': 1.0 + nrm((N_NSA, NSA_BRANCHES, HEAD_DIM), 0.02),
        'nsa_cmp_pos_k': nrm((N_NSA, CMP_BLOCK, HEAD_DIM), 0.5),
        'nsa_cmp_w1_k': nrm((N_NSA, CMP_BLOCK * HEAD_DIM, CMP_HIDDEN), (CMP_BLOCK * HEAD_DIM) ** -0.5),
        'nsa_cmp_w2_k': nrm((N_NSA, CMP_HIDDEN, HEAD_DIM), CMP_HIDDEN ** -0.5),
        'nsa_cmp_pos_v': nrm((N_NSA, CMP_BLOCK, HEAD_DIM), 0.5),
        'nsa_cmp_w1_v': nrm((N_NSA, CMP_BLOCK * HEAD_DIM, CMP_HIDDEN), (CMP_BLOCK * HEAD_DIM) ** -0.5),
        'nsa_cmp_w2_v': nrm((N_NSA, CMP_HIDDEN, HEAD_DIM), CMP_HIDDEN ** -0.5),
        'nsa_w_out': nrm((N_NSA, NSA_HEADS * HEAD_DIM, D), (NSA_HEADS * HEAD_DIM) ** -0.5),
    }


def reference(x_prompt, x_sample, cache_fox_k, cache_fox_v, cache_fox_logf, state_pool,
              cache_nsa_kc, cache_nsa_vc, cache_nsa_ks, cache_nsa_vs, state_nsa_kw, state_nsa_vw,
              page_table, c_prompt, c_sample,
              w_mod, b_mod, norm_mix, norm_ffn, w_up, w_down,
              ab_w_in, ab_b_fgate, ab_q_norm, ab_k_norm, ab_pool_map, ab_pool_scale, ab_w_out,
              nsa_w_in, nsa_b_gate, nsa_q_norm, nsa_k_norm,
              nsa_cmp_pos_k, nsa_cmp_w1_k, nsa_cmp_w2_k, nsa_cmp_pos_v, nsa_cmp_w1_v, nsa_cmp_w2_v, nsa_w_out):
    P = {
        'w_mod': w_mod, 'b_mod': b_mod, 'norm_mix': norm_mix, 'norm_ffn': norm_ffn, 'w_up': w_up, 'w_down': w_down,
        'ab_w_in': ab_w_in, 'ab_b_fgate': ab_b_fgate, 'ab_q_norm': ab_q_norm, 'ab_k_norm': ab_k_norm,
        'ab_pool_map': ab_pool_map, 'ab_pool_scale': ab_pool_scale, 'ab_w_out': ab_w_out,
        'nsa_w_in': nsa_w_in, 'nsa_b_gate': nsa_b_gate, 'nsa_q_norm': nsa_q_norm, 'nsa_k_norm': nsa_k_norm,
        'nsa_cmp_pos_k': nsa_cmp_pos_k, 'nsa_cmp_w1_k': nsa_cmp_w1_k, 'nsa_cmp_w2_k': nsa_cmp_w2_k,
        'nsa_cmp_pos_v': nsa_cmp_pos_v, 'nsa_cmp_w1_v': nsa_cmp_w1_v, 'nsa_cmp_w2_v': nsa_cmp_w2_v,
        'nsa_w_out': nsa_w_out,
    }
    past = {
        'cache_fox_k': cache_fox_k, 'cache_fox_v': cache_fox_v, 'cache_fox_logf': cache_fox_logf,
        'state_pool': state_pool, 'cache_nsa_kc': cache_nsa_kc, 'cache_nsa_vc': cache_nsa_vc,
        'cache_nsa_ks': cache_nsa_ks, 'cache_nsa_vs': cache_nsa_vs,
        'state_nsa_kw': state_nsa_kw, 'state_nsa_vw': state_nsa_vw,
    }
    past_len = page_table.shape[1] * PAGE_SIZE
    y_prompt, sp = run_trunk(x_prompt, c_prompt, 0, P, None, None)
    y_sample, ss = run_trunk(x_sample, c_sample, past_len, P, past, page_table)
    fk_p, fv_p, flf_p, pool_p, kc_p, vc_p, ks_p, vs_p, kw_p, vw_p = sp
    fk_s, fv_s, flf_s, pool_s, kc_s, vc_s, ks_s, vs_s, kw_s, vw_s = ss
    return (y_prompt, y_sample,
            fk_p, fv_p, flf_p, pool_p, kc_p, vc_p, ks_p, vs_p, kw_p, vw_p,
            fk_s, fv_s, flf_s, pool_s, kc_s, vc_s, ks_s, vs_s, kw_s, vw_s)
```

```python
import functools

import jax
import jax.numpy as jnp
from jax import lax
from jax.experimental import pallas as pl
from jax.experimental.pallas import tpu as pltpu

F32 = jnp.float32
BF16 = jnp.bfloat16

D_MODEL = 1024
HEAD_DIM = 64
EPS = 1e-6
ROPE_THETA = 10000.0
NEG_INF = -1e30
BIG = 1e9
PAGE = 128
FOX_HEADS = 8
FOX_WIDTH = FOX_HEADS * HEAD_DIM
POOL_WINDOWS = (2, 4, 8, 16)
POOL_WIDTH = 512
POOL_GROUP_CH = 128
POOL_HALO = 16
NSA_HEADS = 16
NSA_GROUPS = 4
NSA_REP = NSA_HEADS // NSA_GROUPS
NSA_KV_WIDTH = NSA_GROUPS * HEAD_DIM
NSA_BRANCHES = 3
CMP_BLOCK = 32
CMP_STRIDE = 16
SEL_BLOCK = 64
SEL_TOPK = 16
WINDOW = 512
D_FF = 2816
LANE = 128
VMEM_LIMIT = 48 * 1024 * 1024


def _params(*sem):
    return pltpu.CompilerParams(dimension_semantics=sem, vmem_limit_bytes=VMEM_LIMIT)


def _nt(a, b):
    return lax.dot_general(a, b, (((1,), (1,)), ((), ())), preferred_element_type=F32)


def _mm(a, b):
    return jnp.dot(a, b, preferred_element_type=F32)


def _mod_body(c_ref, w_ref, b_ref, o_ref):
    c = c_ref[...]
    a = (c * jax.nn.sigmoid(c)).astype(BF16)
    o_ref[0] = _mm(a, w_ref[0].astype(BF16)) + b_ref[0]


def _modulation(c_all, w_mod, b_mod):
    n_layers, d, n = w_mod.shape
    rows = c_all.shape[0]
    tn = 1536
    return pl.pallas_call(
        _mod_body,
        grid=(n_layers, n // tn),
        in_specs=[
            pl.BlockSpec((rows, d), lambda l, j: (0, 0)),
            pl.BlockSpec((1, d, tn), lambda l, j: (l, 0, j)),
            pl.BlockSpec((1, 1, tn), lambda l, j: (l, 0, j)),
        ],
        out_specs=pl.BlockSpec((1, rows, tn), lambda l, j: (l, 0, j)),
        out_shape=jax.ShapeDtypeStruct((n_layers, rows, n), F32),
        compiler_params=_params("arbitrary", "arbitrary"),
        name="modulation",
    )(c_all, w_mod, b_mod.reshape(n_layers, 1, n))


def _modulated_norm(x, gain, sc, sh):
    xn = x * lax.rsqrt(jnp.mean(x * x, axis=-1, keepdims=True) + EPS) * gain
    return xn * (1.0 + sc) + sh


def _proj_body(*refs, has_side, has_t):
    x_ref, g_ref, sc_ref, sh_ref, w_ref = refs[:5]
    k = 5
    side_ref = t_ref = None
    if has_side:
        side_ref = refs[k]; k += 1
    if has_t:
        t_ref = refs[k]; k += 1
    y_ref = refs[k]; k += 1
    yside_ref = yt_ref = None
    if has_side:
        yside_ref = refs[k]; k += 1
    if has_t:
        yt_ref = refs[k]; k += 1
    h_scr = refs[k]

    @pl.when(pl.program_id(2) == 0)
    def _():
        hb = _modulated_norm(x_ref[0], g_ref[...], sc_ref[0], sh_ref[0]).astype(BF16)
        h_scr[...] = hb
        if has_side:
            yside_ref[0] = _mm(hb, side_ref[...].astype(BF16))
        if has_t:
            yt_ref[0] = _nt(t_ref[...].astype(BF16), hb)

    y_ref[0] = _mm(h_scr[...], w_ref[...].astype(BF16))


def _project(x, gain, sc, sh, w, n_cols, tn, tm, w_side=None, w_t=None):
    bx, tx, d = x.shape
    r = sc.shape[1]
    mod_spec = (pl.BlockSpec((1, tm, d), lambda b, i, j: (b, i, 0)) if r == tx
                else pl.BlockSpec((1, 1, d), lambda b, i, j: (b, 0, 0)))
    in_specs = [
        pl.BlockSpec((1, tm, d), lambda b, i, j: (b, i, 0)),
        pl.BlockSpec((1, d), lambda b, i, j: (0, 0)),
        mod_spec, mod_spec,
        pl.BlockSpec((d, tn), lambda b, i, j: (0, j)),
    ]
    args = [x, gain.reshape(1, d), sc, sh, w]
    out_specs = [pl.BlockSpec((1, tm, tn), lambda b, i, j: (b, i, j))]
    out_shape = [jax.ShapeDtypeStruct((bx, tx, n_cols), F32)]
    if w_side is not None:
        ns = w_side.shape[1]
        in_specs.append(pl.BlockSpec((d, ns), lambda b, i, j: (0, 0)))
        args.append(w_side)
        out_specs.append(pl.BlockSpec((1, tm, ns), lambda b, i, j: (b, i, 0)))
        out_shape.append(jax.ShapeDtypeStruct((bx, tx, ns), F32))
    if w_t is not None:
        nt = w_t.shape[0]
        in_specs.append(pl.BlockSpec((nt, d), lambda b, i, j: (0, 0)))
        args.append(w_t)
        out_specs.append(pl.BlockSpec((1, nt, tm), lambda b, i, j: (b, 0, i)))
        out_shape.append(jax.ShapeDtypeStruct((bx, nt, tx), F32))
    return pl.pallas_call(
        functools.partial(_proj_body, has_side=w_side is not None, has_t=w_t is not None),
        grid=(bx, tx // tm, n_cols // tn),
        in_specs=in_specs, out_specs=out_specs, out_shape=out_shape,
        scratch_shapes=[pltpu.VMEM((tm, d), BF16)],
        compiler_params=_params("arbitrary", "arbitrary", "arbitrary"),
        name="project",
    )(*args)


def _headnorm_body(*refs, tiles, rope, has_gate, n_out):
    y_ref, gain_ref = refs[:2]
    k = 2
    cos_ref = sin_ref = gl_ref = bg_ref = None
    if rope:
        cos_ref, sin_ref = refs[k], refs[k + 1]; k += 2
    if has_gate:
        gl_ref, bg_ref = refs[k], refs[k + 1]; k += 2
    outs = refs[k:k + n_out]
    gate_out = refs[k + n_out] if has_gate else None
    lane = lax.broadcasted_iota(jnp.int32, (1, LANE), 1)
    low_head = lane < HEAD_DIM
    first_half = (lane % HEAD_DIM) < (HEAD_DIM // 2)
    for src, oi, dst in tiles:
        y = y_ref[0, :, src * LANE:(src + 1) * LANE]
        y2 = y * y
        s_lo = jnp.sum(jnp.where(low_head, y2, 0.0), axis=-1, keepdims=True)
        s_hi = jnp.sum(jnp.where(low_head, 0.0, y2), axis=-1, keepdims=True)
        ms = jnp.where(low_head, s_lo, s_hi) * (1.0 / HEAD_DIM)
        yn = y * lax.rsqrt(ms + EPS) * gain_ref[:, src * LANE:(src + 1) * LANE]
        if rope:
            partner = jnp.where(first_half, pltpu.roll(yn, LANE - HEAD_DIM // 2, 1), pltpu.roll(yn, HEAD_DIM // 2, 1))
            yn = yn * cos_ref[...] + partner * sin_ref[...]
        outs[oi][0, :, dst * LANE:(dst + 1) * LANE] = yn
    if has_gate:
        gate_out[0] = jax.nn.sigmoid(gl_ref[0] + bg_ref[...])


def _headnorm(y, width, gains, tiles, out_widths, tm, rope_tabs=None, gate_logits=None, gate_bias=None):
    bx, tx, _ = y.shape
    rope = rope_tabs is not None
    has_gate = gate_logits is not None
    in_specs = [pl.BlockSpec((1, tm, width), lambda b, i: (b, i, 0)),
                pl.BlockSpec((1, width), lambda b, i: (0, 0))]
    args = [y, gains]
    if rope:
        in_specs += [pl.BlockSpec((tm, LANE), lambda b, i: (i, 0))] * 2
        args += list(rope_tabs)
    if has_gate:
        in_specs += [pl.BlockSpec((1, tm, LANE), lambda b, i: (b, i, 0)), pl.BlockSpec((1, LANE), lambda b, i: (0, 0))]
        args += [gate_logits, gate_bias]
    out_specs = [pl.BlockSpec((1, tm, w), lambda b, i: (b, i, 0)) for w in out_widths]
    out_shape = [jax.ShapeDtypeStruct((bx, tx, w), F32) for w in out_widths]
    if has_gate:
        out_specs.append(pl.BlockSpec((1, tm, LANE), lambda b, i: (b, i, 0)))
        out_shape.append(jax.ShapeDtypeStruct((bx, tx, LANE), F32))
    return pl.pallas_call(
        functools.partial(_headnorm_body, tiles=tuple(tiles), rope=rope, has_gate=has_gate, n_out=len(out_widths)),
        grid=(bx, tx // tm),
        in_specs=in_specs, out_specs=out_specs, out_shape=out_shape,
        compiler_params=_params("arbitrary", "arbitrary"),
        name="headnorm",
    )(*args)


def _rope_tables(pos):
    half = HEAD_DIM // 2
    inv_freq = ROPE_THETA ** (-jnp.arange(half, dtype=F32) / half)
    ang = pos.astype(F32)[:, None] * inv_freq[None, :]
    cos, sin = jnp.cos(ang), jnp.sin(ang)
    reps = LANE // HEAD_DIM
    return (jnp.tile(jnp.concatenate([cos, cos], axis=1), (1, reps)),
            jnp.tile(jnp.concatenate([-sin, sin], axis=1), (1, reps)))


def _lane_cumsum(x):
    n = x.shape[-1]
    lane = lax.broadcasted_iota(jnp.int32, x.shape, x.ndim - 1)
    s = 1
    while s < n:
        x = x + jnp.where(lane >= s, pltpu.roll(x, s, x.ndim - 1), 0.0)
        s *= 2
    return x


def _logf_body(f_ref, b_ref, lf_ref, cum_ref):
    lf = jax.nn.log_sigmoid(f_ref[0] + b_ref[...])
    lf_ref[0] = lf
    cum_ref[0] = _lane_cumsum(lf)


def _logf_cumsum(f_t, b_fgate):
    bx, h, t = f_t.shape
    spec = pl.BlockSpec((1, h, t), lambda b: (b, 0, 0))
    return pl.pallas_call(
        _logf_body, grid=(bx,),
        in_specs=[spec, pl.BlockSpec((h, 1), lambda b: (0, 0))],
        out_specs=[spec, spec],
        out_shape=[jax.ShapeDtypeStruct(f_t.shape, F32)] * 2,
        compiler_params=_params("arbitrary"),
        name="logf_cumsum",
    )(f_t, b_fgate.reshape(h, 1))


def _flash_body(*refs, nkv, rep, tq, nk, mode, has_cum, has_sel, gate_branch, scale):
    q_ref, k_ref, v_ref = refs[:3]
    i = 3
    cum_ref = sel_ref = gate_ref = None
    if has_cum:
        cum_ref = refs[i]; i += 1
    if has_sel:
        sel_ref = refs[i]; i += 1
    if gate_branch is not None:
        gate_ref = refs[i]; i += 1
    o_ref, m_scr, l_scr, acc_scr = refs[i:i + 4]
    qi, ki = pl.program_id(1), pl.program_id(2)
    kt = ki if mode == "causal" else qi - (nk - 1) + ki
    active = (ki <= qi) if mode == "causal" else (kt >= 0)

    @pl.when(ki == 0)
    def _():
        m_scr[...] = jnp.full(m_scr.shape, NEG_INF, F32)
        l_scr[...] = jnp.zeros(l_scr.shape, F32)
        acc_scr[...] = jnp.zeros(acc_scr.shape, F32)

    @pl.when(active)
    def _():
        qpos = qi * tq + lax.broadcasted_iota(jnp.int32, (tq, tq), 0)
        kpos = kt * tq + lax.broadcasted_iota(jnp.int32, (tq, tq), 1)
        vis = kpos <= qpos
        if mode == "window":
            vis = vis & (kpos > qpos - WINDOW)
        if has_sel:
            nsel = sel_ref.shape[-1]
            blk = lax.broadcasted_iota(jnp.int32, (nsel, tq), 0)
            kblk = (kt * tq + lax.broadcasted_iota(jnp.int32, (nsel, tq), 1)) // SEL_BLOCK
            expand = (blk == kblk).astype(BF16)
        for g in range(nkv):
            if rep == 1:
                q = q_ref[0, :, g * HEAD_DIM:(g + 1) * HEAD_DIM]
            else:
                q = jnp.concatenate([q_ref[0, :, (g * rep + r) * HEAD_DIM:(g * rep + r + 1) * HEAD_DIM]
                                     for r in range(rep)], axis=0)
            q = (q * scale).astype(BF16)
            k = k_ref[0, :, g * HEAD_DIM:(g + 1) * HEAD_DIM].astype(BF16)
            v = v_ref[0, :, g * HEAD_DIM:(g + 1) * HEAD_DIM].astype(BF16)
            s = _nt(q, k)
            if has_cum:
                s = s - cum_ref[0, g:g + 1, :]
            vis_g = vis
            if has_sel:
                vis_g = vis & (_mm(sel_ref[0, g].astype(BF16), expand) > 0.5)
            if rep > 1:
                vis_g = jnp.concatenate([vis_g] * rep, axis=0)
            s = jnp.where(vis_g, s, NEG_INF)
            m_prev = m_scr[g]
            m_new = jnp.maximum(m_prev, jnp.max(s, axis=-1, keepdims=True))
            alpha = jnp.exp(m_prev - m_new)
            p = jnp.where(vis_g, jnp.exp(s - m_new), 0.0)
            l_scr[g] = alpha * l_scr[g] + jnp.sum(p, axis=-1, keepdims=True)
            acc_scr[g] = alpha * acc_scr[g] + _mm(p.astype(BF16), v)
            m_scr[g] = m_new

    @pl.when(ki == nk - 1)
    def _():
        for g in range(nkv):
            o = acc_scr[g] / l_scr[g]
            for r in range(rep):
                h = g * rep + r
                oh = o[r * tq:(r + 1) * tq]
                if gate_branch is not None:
                    c = h * NSA_BRANCHES + gate_branch
                    oh = oh * gate_ref[0, :, c:c + 1]
                o_ref[0, :, h * HEAD_DIM:(h + 1) * HEAD_DIM] = oh


def _flash(q_arr, qcb, k_arr, kcb, v_arr, vcb, *, nkv, rep, tq, mode, cum=None, sel=None, gates=None,
           gate_branch=None):
    bx, t, _ = q_arr.shape
    wq, wk = nkv * rep * HEAD_DIM, nkv * HEAD_DIM
    nq = t // tq
    nk = nq if mode == "causal" else WINDOW // tq + 1
    if mode == "causal":
        kmap = lambda qi, ki: jnp.minimum(ki, qi)
    else:
        kmap = lambda qi, ki: jnp.maximum(qi - (nk - 1) + ki, 0)
    in_specs = [
        pl.BlockSpec((1, tq, wq), lambda b, qi, ki: (b, qi, qcb)),
        pl.BlockSpec((1, tq, wk), lambda b, qi, ki: (b, kmap(qi, ki), kcb)),
        pl.BlockSpec((1, tq, wk), lambda b, qi, ki: (b, kmap(qi, ki), vcb)),
    ]
    args = [q_arr, k_arr, v_arr]
    if cum is not None:
        in_specs.append(pl.BlockSpec((1, nkv, tq), lambda b, qi, ki: (b, 0, kmap(qi, ki))))
        args.append(cum)
    if sel is not None:
        in_specs.append(pl.BlockSpec((1, nkv, tq, sel.shape[-1]), lambda b, qi, ki: (b, 0, qi, 0)))
        args.append(sel)
    if gates is not None:
        in_specs.append(pl.BlockSpec((1, tq, LANE), lambda b, qi, ki: (b, qi, 0)))
        args.append(gates)
    return pl.pallas_call(
        functools.partial(_flash_body, nkv=nkv, rep=rep, tq=tq, nk=nk, mode=mode, has_cum=cum is not None,
                          has_sel=sel is not None, gate_branch=gate_branch, scale=HEAD_DIM ** -0.5),
        grid=(bx, nq, nk),
        in_specs=in_specs,
        out_specs=pl.BlockSpec((1, tq, wq), lambda b, qi, ki: (b, qi, 0)),
        out_shape=jax.ShapeDtypeStruct((bx, t, wq), F32),
        scratch_shapes=[pltpu.VMEM((nkv, rep * tq, 1), F32), pltpu.VMEM((nkv, rep * tq, 1), F32),
                        pltpu.VMEM((nkv, rep * tq, HEAD_DIM), F32)],
        compiler_params=_params("arbitrary", "arbitrary", "arbitrary"),
        name="flash_" + mode,
    )(*args)


def _pool_body(u_ref, pre_ref, map_ref, scale_ref, o_ref, ext_scr, *, tm, pos0):
    j = pl.program_id(1)

    @pl.when(j == 0)
    def _():
        ext_scr[0:POOL_HALO] = pre_ref[0]

    @pl.when(j > 0)
    def _():
        ext_scr[0:POOL_HALO] = ext_scr[tm:tm + POOL_HALO]

    ext_scr[POOL_HALO:POOL_HALO + tm] = u_ref[0]
    qpos = pos0 + j * tm + lax.broadcasted_iota(jnp.int32, (tm, 1), 0)
    for g, w in enumerate(POOL_WINDOWS):
        lo, hi = g * POOL_GROUP_CH, (g + 1) * POOL_GROUP_CH
        u_new = ext_scr[POOL_HALO:POOL_HALO + tm, lo:hi]
        tot = u_new
        for d in range(1, w):
            tot = tot + ext_scr[POOL_HALO - d:POOL_HALO - d + tm, lo:hi]
        count = jnp.minimum(w, qpos + 1).astype(F32)
        diff = tot / count - u_new
        y = _mm(diff.astype(BF16), map_ref[g].astype(BF16))
        o_ref[0, :, lo:hi] = y * scale_ref[:, lo:hi]


def _pool_mixer(u_arr, ucb, prefix, w_map, scale, pos0, tm):
    bx, t, _ = u_arr.shape
    c = POOL_WIDTH
    return pl.pallas_call(
        functools.partial(_pool_body, tm=tm, pos0=pos0),
        grid=(bx, t // tm),
        in_specs=[pl.BlockSpec((1, tm, c), lambda b, j: (b, j, ucb)),
                  pl.BlockSpec((1, POOL_HALO, c), lambda b, j: (b, 0, 0)),
                  pl.BlockSpec((len(POOL_WINDOWS), POOL_GROUP_CH, POOL_GROUP_CH), lambda b, j: (0, 0, 0)),
                  pl.BlockSpec((1, c), lambda b, j: (0, 0))],
        out_specs=pl.BlockSpec((1, tm, c), lambda b, j: (b, j, 0)),
        out_shape=jax.ShapeDtypeStruct((bx, t, c), F32),
        scratch_shapes=[pltpu.VMEM((POOL_HALO + tm, c), F32)],
        compiler_params=_params("arbitrary", "arbitrary"),
        name="pool_mixer",
    )(u_arr, prefix, w_map, scale.reshape(1, c))


def _outproj_body(*refs, group_sizes):
    n_a = sum(group_sizes)
    a_refs = refs[:n_a]
    w_refs = refs[n_a:n_a + len(group_sizes)]
    res_ref, gate_ref, o_ref = refs[n_a + len(group_sizes):]
    y = None
    k = 0
    for gi, n in enumerate(group_sizes):
        a = a_refs[k][0]
        for r in a_refs[k + 1:k + n]:
            a = a + r[0]
        k += n
        part = _mm(a.astype(BF16), w_refs[gi][...].astype(BF16))
        y = part if y is None else y + part
    o_ref[0] = res_ref[0] + gate_ref[0] * y


def _out_project(groups, w, res, gate, tm, tn):
    bx, tx, d = res.shape
    kg = groups[0][0].shape[-1]
    r = gate.shape[1]
    a_spec = pl.BlockSpec((1, tm, kg), lambda b, i, j: (b, i, 0))
    in_specs, args = [], []
    for grp in groups:
        for a in grp:
            in_specs.append(a_spec); args.append(a)
    for gi in range(len(groups)):
        in_specs.append(pl.BlockSpec((kg, tn), lambda b, i, j, gi=gi: (gi, j)))
        args.append(w)
    in_specs.append(pl.BlockSpec((1, tm, tn), lambda b, i, j: (b, i, j)))
    in_specs.append(pl.BlockSpec((1, tm, tn), lambda b, i, j: (b, i, j)) if r == tx
                    else pl.BlockSpec((1, 1, tn), lambda b, i, j: (b, 0, j)))
    args += [res, gate]
    return pl.pallas_call(
        functools.partial(_outproj_body, group_sizes=tuple(len(g) for g in groups)),
        grid=(bx, tx // tm, d // tn),
        in_specs=in_specs,
        out_specs=pl.BlockSpec((1, tm, tn), lambda b, i, j: (b, i, j)),
        out_shape=jax.ShapeDtypeStruct((bx, tx, d), F32),
        compiler_params=_params("arbitrary", "arbitrary", "arbitrary"),
        name="out_project",
    )(*args)


def _ffn_body(x_ref, g_ref, sc_ref, sh_ref, gate_ref, wg_ref, wu_ref, wd_ref, o_ref, h_scr, acc_scr, *, nf):
    j = pl.program_id(2)

    @pl.when(j == 0)
    def _():
        h_scr[...] = _modulated_norm(x_ref[0], g_ref[...], sc_ref[0], sh_ref[0]).astype(BF16)
        acc_scr[...] = jnp.zeros(acc_scr.shape, F32)

    h = h_scr[...]
    gt = _mm(h, wg_ref[...].astype(BF16))
    up = _mm(h, wu_ref[...].astype(BF16))
    act = (gt * jax.nn.sigmoid(gt)) * up
    acc_scr[...] += _mm(act.astype(BF16), wd_ref[...].astype(BF16))

    @pl.when(j == nf - 1)
    def _():
        o_ref[0] = x_ref[0] + gate_ref[0] * acc_scr[...]


def _ffn(x, gain, sc, sh, gate, w_up, w_down, tm, tf):
    bx, tx, d = x.shape
    nf = D_FF // tf
    r = sc.shape[1]
    mod_spec = (pl.BlockSpec((1, tm, d), lambda b, i, j: (b, i, 0)) if r == tx
                else pl.BlockSpec((1, 1, d), lambda b, i, j: (b, 0, 0)))
    x_spec = pl.BlockSpec((1, tm, d), lambda b, i, j: (b, i, 0))
    return pl.pallas_call(
        functools.partial(_ffn_body, nf=nf),
        grid=(bx, tx // tm, nf),
        in_specs=[x_spec, pl.BlockSpec((1, d), lambda b, i, j: (0, 0)), mod_spec, mod_spec, mod_spec,
                  pl.BlockSpec((d, tf), lambda b, i, j: (0, j)),
                  pl.BlockSpec((d, tf), lambda b, i, j: (0, j + nf)),
                  pl.BlockSpec((tf, d), lambda b, i, j: (j, 0))],
        out_specs=x_spec,
        out_shape=jax.ShapeDtypeStruct(x.shape, F32),
        scratch_shapes=[pltpu.VMEM((tm, d), BF16), pltpu.VMEM((tm, d), F32)],
        compiler_params=_params("arbitrary", "arbitrary", "arbitrary"),
        name="swiglu",
    )(x, gain.reshape(1, d), sc, sh, gate, w_up, w_up, w_down)


GROUPS_PER_TILE = LANE // HEAD_DIM


def _chunk_rows_body(a_ref, o_ref, *, n_chunks):
    for s in range(CMP_STRIDE):
        rows = a_ref[0, pl.ds(s, n_chunks, stride=CMP_STRIDE), :]
        for g in range(GROUPS_PER_TILE):
            o_ref[0, g, :, s * HEAD_DIM:(s + 1) * HEAD_DIM] = rows[:, g * HEAD_DIM:(g + 1) * HEAD_DIM]


def _chunk_rows(a, acb, tm):
    bx, t, _ = a.shape
    n_chunks = tm // CMP_STRIDE
    tiles = NSA_KV_WIDTH // LANE
    return pl.pallas_call(
        functools.partial(_chunk_rows_body, n_chunks=n_chunks),
        grid=(bx, t // tm, tiles),
        in_specs=[pl.BlockSpec((1, tm, LANE), lambda b, i, c: (b, i, acb * tiles + c))],
        out_specs=pl.BlockSpec((1, GROUPS_PER_TILE, n_chunks, CMP_STRIDE * HEAD_DIM), lambda b, i, c: (b, c, i, 0)),
        out_shape=jax.ShapeDtypeStruct((bx, NSA_GROUPS, t // CMP_STRIDE, CMP_STRIDE * HEAD_DIM), F32),
        compiler_params=_params("arbitrary", "arbitrary", "arbitrary"),
        name="chunk_rows",
    )(a)


def _chunk_pages_body(pt_ref, a_ref, o_ref, tok_scr):
    del pt_ref
    for c in range(NSA_GROUPS // GROUPS_PER_TILE):
        pair = a_ref[0, c * GROUPS_PER_TILE:(c + 1) * GROUPS_PER_TILE].reshape(LANE, PAGE)
        tok_scr[...] = pair.T
        for s in range(CMP_STRIDE):
            rows = tok_scr[pl.ds(s, PAGE // CMP_STRIDE, stride=CMP_STRIDE), :]
            for g in range(GROUPS_PER_TILE):
                o_ref[0, c * GROUPS_PER_TILE + g, :, s * HEAD_DIM:(s + 1) * HEAD_DIM] = (
                    rows[:, g * HEAD_DIM:(g + 1) * HEAD_DIM])


def _chunk_pages(cache_t, page_table):
    bx, n_pages = page_table.shape
    cpp = PAGE // CMP_STRIDE
    return pl.pallas_call(
        _chunk_pages_body,
        grid_spec=pltpu.PrefetchScalarGridSpec(
            num_scalar_prefetch=1, grid=(bx, n_pages),
            in_specs=[pl.BlockSpec((1, NSA_GROUPS, HEAD_DIM, PAGE), lambda b, p, pt: (pt[b * n_pages + p], 0, 0, 0))],
            out_specs=pl.BlockSpec((1, NSA_GROUPS, cpp, CMP_STRIDE * HEAD_DIM), lambda b, p, pt: (b, 0, p, 0)),
            scratch_shapes=[pltpu.VMEM((PAGE, LANE), F32)]),
        out_shape=jax.ShapeDtypeStruct((bx, NSA_GROUPS, n_pages * cpp, CMP_STRIDE * HEAD_DIM), F32),
        compiler_params=_params("arbitrary", "arbitrary"),
        name="chunk_pages",
    )(page_table.reshape(-1), cache_t)


def _compress_body(a_ref, pos_ref, w1_ref, w2_ref, o_ref, *, n_chunks):
    half = CMP_STRIDE * HEAD_DIM
    a = a_ref[0, 0].astype(BF16)
    w1 = w1_ref[...].astype(BF16)
    first = _mm(a, w1[:half])
    second = _mm(a, w1[half:])
    bias = _mm(pos_ref[...].astype(BF16), w1)
    hidden = (first + pltpu.roll(second, n_chunks - 1, 0)) + bias
    o_ref[0, 0] = _mm(jax.nn.gelu(hidden).astype(BF16), w2_ref[...].astype(BF16))


def _compress(chunks, pos_emb, w1, w2):
    bx, g, n_chunks, half = chunks.shape
    hidden = w1.shape[1]
    return pl.pallas_call(
        functools.partial(_compress_body, n_chunks=n_chunks),
        grid=(bx, g),
        in_specs=[pl.BlockSpec((1, 1, n_chunks, half), lambda b, i: (b, i, 0, 0)),
                  pl.BlockSpec((1, 2 * half), lambda b, i: (0, 0)),
                  pl.BlockSpec((2 * half, hidden), lambda b, i: (0, 0)),
                  pl.BlockSpec((hidden, HEAD_DIM), lambda b, i: (0, 0))],
        out_specs=pl.BlockSpec((1, 1, n_chunks, HEAD_DIM), lambda b, i: (b, i, 0, 0)),
        out_shape=jax.ShapeDtypeStruct((bx, g, n_chunks, HEAD_DIM), F32),
        compiler_params=_params("arbitrary", "arbitrary"),
        name="compress",
    )(chunks, pos_emb.reshape(1, 2 * half), w1, w2)


def _cmp_select_body(q_ref, kc_ref, vc_ref, gate_ref, o_ref, sel_ref, *, tq, n_cmp, n_sel, nselp, pos0, scale):
    g, qi = pl.program_id(1), pl.program_id(2)
    ncp = kc_ref.shape[2]
    qpos = pos0 + qi * tq + lax.broadcasted_iota(jnp.int32, (tq, 1), 0)
    c_idx = lax.broadcasted_iota(jnp.int32, (1, ncp), 1)
    c_valid = (c_idx * CMP_STRIDE + (CMP_BLOCK - 1) <= qpos) & (c_idx < n_cmp)
    kc = kc_ref[0, 0].astype(BF16)
    vc = vc_ref[0, 0].astype(BF16)
    pc_sum = jnp.zeros((tq, ncp), F32)
    for r in range(NSA_REP):
        q = (q_ref[0, :, r * HEAD_DIM:(r + 1) * HEAD_DIM] * scale).astype(BF16)
        s = jnp.where(c_valid, _nt(q, kc), NEG_INF)
        m = jnp.max(s, axis=-1, keepdims=True)
        p = jnp.where(c_valid, jnp.exp(s - m), 0.0)
        l = jnp.sum(p, axis=-1, keepdims=True)
        pc = jnp.where(l > 0.0, p / jnp.where(l > 0.0, l, 1.0), 0.0)
        pc_sum = pc_sum + pc
        col = (g * NSA_REP + r) * NSA_BRANCHES
        lane = lax.broadcasted_iota(jnp.int32, (1, LANE), 1)
        gate = jnp.sum(jnp.where(lane == col, gate_ref[0], 0.0), axis=-1, keepdims=True)
        o_ref[0, :, r * HEAD_DIM:(r + 1) * HEAD_DIM] = _mm(pc.astype(BF16), vc) * gate
    cj = lax.broadcasted_iota(jnp.int32, (ncp, nselp), 0) * CMP_STRIDE
    sj = lax.broadcasted_iota(jnp.int32, (ncp, nselp), 1) * SEL_BLOCK
    overlap = ((cj < sj + SEL_BLOCK) & (cj + (CMP_BLOCK - 1) >= sj)).astype(F32)
    imp = jnp.dot(pc_sum, overlap, preferred_element_type=F32, precision=lax.Precision.HIGHEST)
    j_idx = lax.broadcasted_iota(jnp.int32, (1, nselp), 1)
    forced = (j_idx == 0) | (j_idx == qpos // SEL_BLOCK)
    valid = j_idx * SEL_BLOCK <= qpos
    score = jnp.where(valid, jnp.where(forced, BIG, imp), -BIG)
    rank = jnp.zeros((tq, nselp), jnp.int32)
    for i in range(n_sel):
        si = jnp.sum(jnp.where(j_idx == i, score, 0.0), axis=-1, keepdims=True)
        ahead = (si > score) | ((si == score) & (i < j_idx))
        rank = rank + ahead.astype(jnp.int32)
    sel_ref[0, 0] = ((rank < SEL_TOPK) & valid).astype(F32)


def _cmp_select(qn, kcmp, vcmp, gates, n_cmp, n_sel, nselp, pos0, tq):
    bx, t, _ = qn.shape
    ncp = kcmp.shape[2]
    gw = NSA_REP * HEAD_DIM
    return pl.pallas_call(
        functools.partial(_cmp_select_body, tq=tq, n_cmp=n_cmp, n_sel=n_sel, nselp=nselp, pos0=pos0,
                          scale=HEAD_DIM ** -0.5),
        grid=(bx, NSA_GROUPS, t // tq),
        in_specs=[pl.BlockSpec((1, tq, gw), lambda b, g, i: (b, i, g)),
                  pl.BlockSpec((1, 1, ncp, HEAD_DIM), lambda b, g, i: (b, g, 0, 0)),
                  pl.BlockSpec((1, 1, ncp, HEAD_DIM), lambda b, g, i: (b, g, 0, 0)),
                  pl.BlockSpec((1, tq, LANE), lambda b, g, i: (b, i, 0))],
        out_specs=[pl.BlockSpec((1, tq, gw), lambda b, g, i: (b, i, g)),
                   pl.BlockSpec((1, 1, tq, nselp), lambda b, g, i: (b, g, i, 0))],
        out_shape=[jax.ShapeDtypeStruct((bx, t, NSA_HEADS * HEAD_DIM), F32),
                   jax.ShapeDtypeStruct((bx, NSA_GROUPS, t, nselp), F32)],
        compiler_params=_params("arbitrary", "arbitrary", "arbitrary"),
        name="cmp_select",
    )(qn, kcmp, vcmp, gates)


def _softmax_step(s, vis, m_scr, l_scr, acc_scr, pv_fn):
    m_prev = m_scr[...]
    m_new = jnp.maximum(m_prev, jnp.max(s, axis=-1, keepdims=True))
    alpha = jnp.exp(m_prev - m_new)
    p = jnp.exp(s - m_new)
    if vis is not None:
        p = jnp.where(vis, p, 0.0)
    l_scr[...] = alpha * l_scr[...] + jnp.sum(p, axis=-1, keepdims=True)
    acc_scr[...] = alpha * acc_scr[...] + pv_fn(p)
    m_scr[...] = m_new


def _new_rows_step(qbd, k_new, v_new, row_t, bias_cols, m_scr, l_scr, acc_scr, tnew):
    cols = []
    for j in range(tnew):
        sj = jnp.sum(qbd * k_new[j:j + 1, :], axis=-1, keepdims=True)
        if bias_cols is not None:
            sj = sj - bias_cols[j]
        cols.append(jnp.where(row_t >= j, sj, NEG_INF))
    m_prev = m_scr[...]
    m_new = m_prev
    for sj in cols:
        m_new = jnp.maximum(m_new, sj)
    alpha = jnp.exp(m_prev - m_new)
    l = alpha * l_scr[...]
    acc = alpha * acc_scr[...]
    for j, sj in enumerate(cols):
        pj = jnp.where(row_t >= j, jnp.exp(sj - m_new), 0.0)
        l = l + pj
        acc = acc + pj * v_new[j:j + 1, :]
    return acc / l


def _init_softmax(m_scr, l_scr, acc_scr):
    m_scr[...] = jnp.full(m_scr.shape, NEG_INF, F32)
    l_scr[...] = jnp.zeros(l_scr.shape, F32)
    acc_scr[...] = jnp.zeros(acc_scr.shape, F32)


def _fox_sample_body(pt_ref, q_ref, kn_ref, vn_ref, fn_ref, bf_ref, kt_ref, vt_ref, lf_ref, o_ref, lfo_ref,
                     qbd_scr, m_scr, l_scr, acc_scr, carry_scr, *, n_pages, tnew, scale):
    del pt_ref
    p = pl.program_id(1)
    nh, w = FOX_HEADS, FOX_WIDTH
    rows = tnew * nh
    head_of_lane = lax.broadcasted_iota(jnp.int32, (nh, w), 1) // HEAD_DIM
    hmask = (head_of_lane == lax.broadcasted_iota(jnp.int32, (nh, w), 0)).astype(F32)

    @pl.when(p == 0)
    def _():
        for t in range(tnew):
            qbd_scr[t * nh:(t + 1) * nh, :] = q_ref[0, t:t + 1, :] * scale * hmask
        _init_softmax(m_scr, l_scr, acc_scr)
        carry_scr[...] = jnp.zeros(carry_scr.shape, F32)

    cum_k = carry_scr[...] + _lane_cumsum(lf_ref[0])
    carry_scr[...] = cum_k[:, PAGE - 1:PAGE]
    bias = jnp.concatenate([cum_k] * tnew, axis=0)
    kt = kt_ref[0].reshape(w, PAGE).astype(BF16)
    vt = vt_ref[0].reshape(w, PAGE).astype(BF16)
    s = _mm(qbd_scr[...].astype(BF16), kt) - bias
    _softmax_step(s, None, m_scr, l_scr, acc_scr, lambda pr: _nt(pr.astype(BF16), vt))

    @pl.when(p == n_pages - 1)
    def _():
        lf_new = jax.nn.log_sigmoid(fn_ref[0] + bf_ref[...])
        lfo_ref[0] = lf_new
        run = carry_scr[...]
        bias_cols = []
        for j in range(tnew):
            run = run + lf_new[:, j:j + 1]
            bias_cols.append(jnp.concatenate([run] * tnew, axis=0))
        row_t = lax.broadcasted_iota(jnp.int32, (rows, 1), 0) // nh
        o = _new_rows_step(qbd_scr[...], kn_ref[0], vn_ref[0], row_t, bias_cols, m_scr, l_scr, acc_scr, tnew)
        o = o * jnp.concatenate([hmask] * tnew, axis=0)
        for t in range(tnew):
            o_ref[0, t:t + 1, :] = jnp.sum(o[t * nh:(t + 1) * nh], axis=0, keepdims=True)


def _fox_sample(q, k_new, v_new, f_new, b_fgate, cache_kt, cache_vt, cache_lft, page_table):
    bx, tnew, w = q.shape
    n_pages = page_table.shape[1]
    nh = FOX_HEADS
    rows = tnew * nh
    new_spec = pl.BlockSpec((1, tnew, w), lambda b, p, pt: (b, 0, 0))
    page = lambda b, p, pt: (pt[b * n_pages + p], 0, 0, 0)
    return pl.pallas_call(
        functools.partial(_fox_sample_body, n_pages=n_pages, tnew=tnew, scale=HEAD_DIM ** -0.5),
        grid_spec=pltpu.PrefetchScalarGridSpec(
            num_scalar_prefetch=1, grid=(bx, n_pages),
            in_specs=[new_spec, new_spec, new_spec,
                      pl.BlockSpec((1, nh, tnew), lambda b, p, pt: (b, 0, 0)),
                      pl.BlockSpec((nh, 1), lambda b, p, pt: (0, 0)),
                      pl.BlockSpec((1, nh, HEAD_DIM, PAGE), page),
                      pl.BlockSpec((1, nh, HEAD_DIM, PAGE), page),
                      pl.BlockSpec((1, nh, PAGE), lambda b, p, pt: (pt[b * n_pages + p], 0, 0))],
            out_specs=[new_spec, pl.BlockSpec((1, nh, tnew), lambda b, p, pt: (b, 0, 0))],
            scratch_shapes=[pltpu.VMEM((rows, w), F32), pltpu.VMEM((rows, 1), F32), pltpu.VMEM((rows, 1), F32),
                            pltpu.VMEM((rows, w), F32), pltpu.VMEM((nh, 1), F32)]),
        out_shape=[jax.ShapeDtypeStruct((bx, tnew, w), F32), jax.ShapeDtypeStruct((bx, nh, tnew), F32)],
        compiler_params=_params("arbitrary", "arbitrary"),
        name="fox_sample",
    )(page_table.reshape(-1), q, k_new, v_new, f_new, b_fgate.reshape(nh, 1), cache_kt, cache_vt, cache_lft)


def _fill_group_queries(qbd_scr, q_ref, tnew, scale):
    qbd_scr[...] = jnp.zeros(qbd_scr.shape, F32)
    for h in range(NSA_HEADS):
        g = h // NSA_REP
        qbd_scr[h * tnew:(h + 1) * tnew, g * HEAD_DIM:(g + 1) * HEAD_DIM] = (
            q_ref[0, :, h * HEAD_DIM:(h + 1) * HEAD_DIM] * scale)


def _write_group_heads(o_ref, o, gate_ref, branch, tnew):
    for h in range(NSA_HEADS):
        g = h // NSA_REP
        c = h * NSA_BRANCHES + branch
        o_ref[0, :, h * HEAD_DIM:(h + 1) * HEAD_DIM] = (
            o[h * tnew:(h + 1) * tnew, g * HEAD_DIM:(g + 1) * HEAD_DIM] * gate_ref[0, :, c:c + 1])


def _sel_sample_body(pt_ref, q_ref, kn_ref, vn_ref, selrows_ref, gate_ref, kt_ref, vt_ref, o_ref,
                     qbd_scr, m_scr, l_scr, acc_scr, *, n_pages, tnew, scale):
    del pt_ref
    p = pl.program_id(1)
    rows = NSA_HEADS * tnew
    nselp = selrows_ref.shape[-1]

    @pl.when(p == 0)
    def _():
        _fill_group_queries(qbd_scr, q_ref, tnew, scale)
        _init_softmax(m_scr, l_scr, acc_scr)

    blk_of_lane = p * (PAGE // SEL_BLOCK) + lax.broadcasted_iota(jnp.int32, (nselp, PAGE), 1) // SEL_BLOCK
    expand = (lax.broadcasted_iota(jnp.int32, (nselp, PAGE), 0) == blk_of_lane).astype(BF16)
    vis = _mm(selrows_ref[0].astype(BF16), expand) > 0.5
    kt = kt_ref[0].reshape(NSA_KV_WIDTH, PAGE).astype(BF16)
    vt = vt_ref[0].reshape(NSA_KV_WIDTH, PAGE).astype(BF16)
    s = jnp.where(vis, _mm(qbd_scr[...].astype(BF16), kt), NEG_INF)
    _softmax_step(s, vis, m_scr, l_scr, acc_scr, lambda pr: _nt(pr.astype(BF16), vt))

    @pl.when(p == n_pages - 1)
    def _():
        row_t = lax.broadcasted_iota(jnp.int32, (rows, 1), 0) % tnew
        o = _new_rows_step(qbd_scr[...], kn_ref[0], vn_ref[0], row_t, None, m_scr, l_scr, acc_scr, tnew)
        _write_group_heads(o_ref, o, gate_ref, 1, tnew)


def _sel_sample(q, k_new, v_new, selrows, gates, cache_kt, cache_vt, page_table):
    bx, tnew, wq = q.shape
    n_pages = page_table.shape[1]
    rows = NSA_HEADS * tnew
    wk = NSA_KV_WIDTH
    fixed = lambda b, p, pt: (b, 0, 0)
    page = lambda b, p, pt: (pt[b * n_pages + p], 0, 0, 0)
    return pl.pallas_call(
        functools.partial(_sel_sample_body, n_pages=n_pages, tnew=tnew, scale=HEAD_DIM ** -0.5),
        grid_spec=pltpu.PrefetchScalarGridSpec(
            num_scalar_prefetch=1, grid=(bx, n_pages),
            in_specs=[pl.BlockSpec((1, tnew, wq), fixed), pl.BlockSpec((1, tnew, wk), fixed),
                      pl.BlockSpec((1, tnew, wk), fixed), pl.BlockSpec((1, rows, selrows.shape[-1]), fixed),
                      pl.BlockSpec((1, tnew, LANE), fixed),
                      pl.BlockSpec((1, NSA_GROUPS, HEAD_DIM, PAGE), page),
                      pl.BlockSpec((1, NSA_GROUPS, HEAD_DIM, PAGE), page)],
            out_specs=pl.BlockSpec((1, tnew, wq), fixed),
            scratch_shapes=[pltpu.VMEM((rows, wk), F32), pltpu.VMEM((rows, 1), F32), pltpu.VMEM((rows, 1), F32),
                            pltpu.VMEM((rows, wk), F32)]),
        out_shape=jax.ShapeDtypeStruct((bx, tnew, wq), F32),
        compiler_params=_params("arbitrary", "arbitrary"),
        name="sel_sample",
    )(page_table.reshape(-1), q, k_new, v_new, selrows, gates, cache_kt, cache_vt)


def _win_sample_body(q_ref, kn_ref, vn_ref, gate_ref, kt_ref, vt_ref, o_ref, qbd_scr, m_scr, l_scr, acc_scr,
                     *, tnew, wbuf, scale):
    rows = NSA_HEADS * tnew
    _fill_group_queries(qbd_scr, q_ref, tnew, scale)
    _init_softmax(m_scr, l_scr, acc_scr)
    row_t = lax.broadcasted_iota(jnp.int32, (rows, 1), 0) % tnew
    vis = lax.broadcasted_iota(jnp.int32, (rows, wbuf), 1) > row_t + (wbuf - WINDOW)
    kt = kt_ref[0].reshape(NSA_KV_WIDTH, wbuf).astype(BF16)
    vt = vt_ref[0].reshape(NSA_KV_WIDTH, wbuf).astype(BF16)
    s = jnp.where(vis, _mm(qbd_scr[...].astype(BF16), kt), NEG_INF)
    _softmax_step(s, vis, m_scr, l_scr, acc_scr, lambda pr: _nt(pr.astype(BF16), vt))
    o = _new_rows_step(qbd_scr[...], kn_ref[0], vn_ref[0], row_t, None, m_scr, l_scr, acc_scr, tnew)
    _write_group_heads(o_ref, o, gate_ref, 2, tnew)


def _win_sample(q, k_new, v_new, gates, buf_kt, buf_vt):
    bx, tnew, wq = q.shape
    wbuf = buf_kt.shape[-1]
    rows = NSA_HEADS * tnew
    wk = NSA_KV_WIDTH
    fixed = lambda b: (b, 0, 0)
    buf = pl.BlockSpec((1, NSA_GROUPS, HEAD_DIM, wbuf), lambda b: (b, 0, 0, 0))
    return pl.pallas_call(
        functools.partial(_win_sample_body, tnew=tnew, wbuf=wbuf, scale=HEAD_DIM ** -0.5),
        grid=(bx,),
        in_specs=[pl.BlockSpec((1, tnew, wq), fixed), pl.BlockSpec((1, tnew, wk), fixed),
                  pl.BlockSpec((1, tnew, wk), fixed), pl.BlockSpec((1, tnew, LANE), fixed), buf, buf],
        out_specs=pl.BlockSpec((1, tnew, wq), fixed),
        out_shape=jax.ShapeDtypeStruct((bx, tnew, wq), F32),
        scratch_shapes=[pltpu.VMEM((rows, wk), F32), pltpu.VMEM((rows, 1), F32), pltpu.VMEM((rows, 1), F32),
                        pltpu.VMEM((rows, wk), F32)],
        compiler_params=_params("arbitrary"),
        name="win_sample",
    )(q, k_new, v_new, gates, buf_kt, buf_vt)


def _tile_heads(v, n):
    return jnp.tile(v.astype(F32), n)


def _mix_ab(x, mods, pos0, past, p, tiles):
    bx, tx, _ = x.shape
    w_in = p["ab_w_in"]
    split_f = 3 * FOX_WIDTH
    w_main = jnp.concatenate([w_in[:, :split_f], w_in[:, split_f + FOX_HEADS:]], axis=1)
    w_f_t = w_in[:, split_f:split_f + FOX_HEADS].T
    y, f_t = _project(x, p["norm_mix"], mods["sc1"], mods["sh1"], w_main, w_main.shape[1], 512, tiles["tm"],
                      w_t=w_f_t)
    gains = jnp.concatenate([_tile_heads(p["ab_q_norm"], FOX_HEADS), _tile_heads(p["ab_k_norm"], FOX_HEADS)])
    n_tiles = 2 * FOX_WIDTH // LANE
    (qk,) = _headnorm(y, 2 * FOX_WIDTH, gains.reshape(1, -1), [(c, 0, c) for c in range(n_tiles)],
                      [2 * FOX_WIDTH], tiles["tm_norm"])
    return y, f_t, qk


def _nsa_project(x, mods, pos_rows, p, tiles):
    w_in = p["nsa_w_in"]
    n_main = NSA_HEADS * HEAD_DIM + 6 * NSA_KV_WIDTH
    n_gate = NSA_BRANCHES * NSA_HEADS
    w_gate = jnp.pad(w_in[:, n_main:], ((0, 0), (0, LANE - n_gate)))
    y, gl = _project(x, p["norm_mix"], mods["sc1"], mods["sh1"], w_in, n_main, 512, tiles["tm"], w_side=w_gate)
    kn3 = p["nsa_k_norm"]
    qw = NSA_HEADS * HEAD_DIM
    gains = jnp.concatenate([
        _tile_heads(p["nsa_q_norm"], NSA_HEADS),
        _tile_heads(kn3[0], NSA_GROUPS), jnp.ones((NSA_KV_WIDTH,), F32),
        _tile_heads(kn3[1], NSA_GROUPS), jnp.ones((NSA_KV_WIDTH,), F32),
        _tile_heads(kn3[2], NSA_GROUPS)])
    width = gains.shape[0]
    qt = qw // LANE
    kt = NSA_KV_WIDTH // LANE
    tile_map = [(c, 0, c) for c in range(qt)]
    for i in range(NSA_BRANCHES):
        tile_map += [(qt + 2 * i * kt + c, 1, i * kt + c) for c in range(kt)]
    b_gate = jnp.pad(p["nsa_b_gate"], (0, LANE - n_gate)).reshape(1, LANE)
    qn, kn, gates = _headnorm(y, width, gains.reshape(1, -1), tile_map, [qw, NSA_BRANCHES * NSA_KV_WIDTH],
                              tiles["tm_norm"], rope_tabs=_rope_tables(pos_rows), gate_logits=gl, gate_bias=b_gate)
    return y, qn, kn, gates


def _layer_params(params, layer):
    e = layer // 2
    p = {"norm_mix": params["norm_mix"][layer], "norm_ffn": params["norm_ffn"][layer],
         "w_up": params["w_up"][layer], "w_down": params["w_down"][layer]}
    prefix = "ab_" if layer % 2 == 0 else "nsa_"
    for k, v in params.items():
        if k.startswith(prefix):
            p[k] = v[e]
    return p


def _heads(a, n):
    return a.reshape(a.shape[0], a.shape[1], n, HEAD_DIM)


def _prompt_trunk(x, mod, params):
    bx, t, d = x.shape
    tiles = {"tm": 512, "tm_norm": 256}
    states = {}
    for layer in range(mod.shape[0]):
        p = _layer_params(params, layer)
        sh1, sc1, g1, sh2, sc2, g2 = [m[:, None, :] for m in jnp.split(mod[layer], 6, axis=-1)]
        mods = {"sc1": sc1, "sh1": sh1}
        if layer % 2 == 0:
            y, f_t, qk = _mix_ab(x, mods, 0, None, p, tiles)
            lf_t, cum = _logf_cumsum(f_t, p["ab_b_fgate"])
            o_fox = _flash(qk, 0, qk, 1, y, 2, nkv=FOX_HEADS, rep=1, tq=512, mode="causal", cum=cum)
            o_pool = _pool_mixer(y, 3, jnp.zeros((bx, POOL_HALO, POOL_WIDTH), F32), p["ab_pool_map"],
                                 p["ab_pool_scale"], 0, 512)
            x = _out_project([[o_fox], [o_pool]], p["ab_w_out"], x, g1, 512, 512)
            states["fox_k"] = _heads(qk[:, :, FOX_WIDTH:], FOX_HEADS)
            states["fox_v"] = _heads(y[:, :, 2 * FOX_WIDTH:3 * FOX_WIDTH], FOX_HEADS)
            states["fox_logf"] = lf_t.transpose(0, 2, 1)
            states["pool"] = y[:, t - (POOL_HALO - 1):, 3 * FOX_WIDTH:]
        else:
            y, qn, kn, gates = _nsa_project(x, mods, jnp.arange(t), p, tiles)
            qw, kw_ = NSA_HEADS * HEAD_DIM, NSA_KV_WIDTH
            n_chunk = t // CMP_STRIDE
            kcmp = _compress(_chunk_rows(kn, 0, t), p["nsa_cmp_pos_k"], p["nsa_cmp_w1_k"], p["nsa_cmp_w2_k"])
            vcmp = _compress(_chunk_rows(y, (qw + kw_) // kw_, t), p["nsa_cmp_pos_v"], p["nsa_cmp_w1_v"],
                             p["nsa_cmp_w2_v"])
            n_sel = -(-t // SEL_BLOCK)
            o_cmp, sel = _cmp_select(qn, kcmp, vcmp, gates, n_chunk - 1, n_sel, LANE, 0, 256)
            o_sel = _flash(qn, 0, kn, 1, y, (qw + 3 * kw_) // kw_, nkv=NSA_GROUPS, rep=NSA_REP, tq=256,
                           mode="causal", sel=sel, gates=gates, gate_branch=1)
            o_win = _flash(qn, 0, kn, 2, y, (qw + 5 * kw_) // kw_, nkv=NSA_GROUPS, rep=NSA_REP, tq=256,
                           mode="window", gates=gates, gate_branch=2)
            x = _out_project([[o_cmp, o_sel, o_win]], p["nsa_w_out"], x, g1, 512, 512)
            buf = min(WINDOW, t)
            states["nsa_kc"] = _heads(kn[:, :, :kw_], NSA_GROUPS)
            states["nsa_vc"] = _heads(y[:, :, qw + kw_:qw + 2 * kw_], NSA_GROUPS)
            states["nsa_ks"] = _heads(kn[:, :, kw_:2 * kw_], NSA_GROUPS)
            states["nsa_vs"] = _heads(y[:, :, qw + 3 * kw_:qw + 4 * kw_], NSA_GROUPS)
            states["nsa_kw"] = _heads(kn[:, t - buf:, 2 * kw_:], NSA_GROUPS)
            states["nsa_vw"] = _heads(y[:, t - buf:, qw + 5 * kw_:qw + 6 * kw_], NSA_GROUPS)
        x = _ffn(x, p["norm_ffn"], sc2, sh2, g2, p["w_up"], p["w_down"], 512, 256)
    return x, states


def _sample_trunk(x, mod, params, past, page_table):
    bx, tnew, d = x.shape
    rows = bx * tnew
    n_pages = page_table.shape[1]
    pos0 = n_pages * PAGE
    assert tnew < CMP_STRIDE and pos0 % CMP_STRIDE == 0 and pos0 >= WINDOW
    tiles = {"tm": rows, "tm_norm": rows}
    xf = x.reshape(1, rows, d)
    per_batch = lambda a: a.reshape(bx, tnew, a.shape[-1])
    states = {}
    for layer in range(mod.shape[0]):
        p = _layer_params(params, layer)
        e = layer // 2
        sh1, sc1, g1, sh2, sc2, g2 = [jnp.repeat(m, tnew, axis=0)[None] for m in jnp.split(mod[layer], 6, axis=-1)]
        mods = {"sc1": sc1, "sh1": sh1}
        if layer % 2 == 0:
            y, f_t, qk = _mix_ab(xf, mods, pos0, None, p, tiles)
            q_s, k_s = per_batch(qk[0, :, :FOX_WIDTH]), per_batch(qk[0, :, FOX_WIDTH:])
            v_s = per_batch(y[0, :, 2 * FOX_WIDTH:3 * FOX_WIDTH])
            u_s = per_batch(y[0, :, 3 * FOX_WIDTH:])
            f_new = f_t[0].reshape(FOX_HEADS, bx, tnew).transpose(1, 0, 2)
            cache_kt = past["cache_fox_k"][e].transpose(0, 2, 3, 1)
            cache_vt = past["cache_fox_v"][e].transpose(0, 2, 3, 1)
            cache_lft = past["cache_fox_logf"][e].transpose(0, 2, 1)
            o_fox, lf_new = _fox_sample(q_s, k_s, v_s, f_new, p["ab_b_fgate"], cache_kt, cache_vt, cache_lft,
                                        page_table)
            pool_prev = past["state_pool"][e]
            prefix = jnp.pad(pool_prev, ((0, 0), (1, 0), (0, 0)))
            o_pool = _pool_mixer(u_s, 0, prefix, p["ab_pool_map"], p["ab_pool_scale"], pos0, tnew)
            xf = _out_project([[o_fox.reshape(1, rows, -1)], [o_pool.reshape(1, rows, -1)]], p["ab_w_out"], xf, g1,
                              rows, 512)
            states["fox_k"] = _heads(k_s, FOX_HEADS)
            states["fox_v"] = _heads(v_s, FOX_HEADS)
            states["fox_logf"] = lf_new.transpose(0, 2, 1)
            states["pool"] = jnp.concatenate([pool_prev, u_s], axis=1)[:, -(POOL_HALO - 1):]
        else:
            pos_rows = pos0 + jnp.arange(rows) % tnew
            y, qn, kn, gates = _nsa_project(xf, mods, pos_rows, p, tiles)
            qw, kw_ = NSA_HEADS * HEAD_DIM, NSA_KV_WIDTH
            q_s, gates_s = per_batch(qn[0]), per_batch(gates[0])
            kn_s, y_s = per_batch(kn[0]), per_batch(y[0])
            kc_s, ks_s, kwn_s = kn_s[..., :kw_], kn_s[..., kw_:2 * kw_], kn_s[..., 2 * kw_:]
            vc_s, vs_s, vwn_s = (y_s[..., qw + kw_:qw + 2 * kw_], y_s[..., qw + 3 * kw_:qw + 4 * kw_],
                                 y_s[..., qw + 5 * kw_:qw + 6 * kw_])
            d_major = lambda a: a.transpose(0, 2, 3, 1)
            kcmp = _compress(_chunk_pages(d_major(past["cache_nsa_kc"][e]), page_table), p["nsa_cmp_pos_k"],
                             p["nsa_cmp_w1_k"], p["nsa_cmp_w2_k"])
            vcmp = _compress(_chunk_pages(d_major(past["cache_nsa_vc"][e]), page_table), p["nsa_cmp_pos_v"],
                             p["nsa_cmp_w1_v"], p["nsa_cmp_w2_v"])
            total = pos0 + tnew
            n_cmp = total // CMP_STRIDE - 1
            n_sel = -(-total // SEL_BLOCK)
            nselp = -(-n_sel // (2 * LANE)) * (2 * LANE)
            o_cmp, sel = _cmp_select(q_s, kcmp, vcmp, gates_s, n_cmp, n_sel, nselp, pos0, tnew)
            selrows = jnp.repeat(sel, NSA_REP, axis=1).reshape(bx, NSA_HEADS * tnew, nselp)
            o_sel = _sel_sample(q_s, ks_s, vs_s, selrows, gates_s, d_major(past["cache_nsa_ks"][e]),
                                d_major(past["cache_nsa_vs"][e]), page_table)
            kw_prev, vw_prev = past["state_nsa_kw"][e], past["state_nsa_vw"][e]
            o_win = _win_sample(q_s, kwn_s, vwn_s, gates_s, d_major(kw_prev), d_major(vw_prev))
            flat = lambda a: a.reshape(1, rows, -1)
            xf = _out_project([[flat(o_cmp), flat(o_sel), flat(o_win)]], p["nsa_w_out"], xf, g1, rows, 512)
            buf = kw_prev.shape[1]
            states["nsa_kc"] = _heads(kc_s, NSA_GROUPS)
            states["nsa_vc"] = _heads(vc_s, NSA_GROUPS)
            states["nsa_ks"] = _heads(ks_s, NSA_GROUPS)
            states["nsa_vs"] = _heads(vs_s, NSA_GROUPS)
            states["nsa_kw"] = jnp.concatenate([kw_prev, _heads(kwn_s, NSA_GROUPS)], axis=1)[:, -buf:]
            states["nsa_vw"] = jnp.concatenate([vw_prev, _heads(vwn_s, NSA_GROUPS)], axis=1)[:, -buf:]
        xf = _ffn(xf, p["norm_ffn"], sc2, sh2, g2, p["w_up"], p["w_down"], rows, 256)
    return xf.reshape(bx, tnew, d), states


_STATE_NAMES = ("fox_k", "fox_v", "fox_logf", "pool", "nsa_kc", "nsa_vc", "nsa_ks", "nsa_vs", "nsa_kw", "nsa_vw")


def kernel(x_prompt, x_sample, cache_fox_k, cache_fox_v, cache_fox_logf, state_pool, cache_nsa_kc, cache_nsa_vc,
           cache_nsa_ks, cache_nsa_vs, state_nsa_kw, state_nsa_vw, page_table, c_prompt, c_sample, w_mod, b_mod,
           norm_mix, norm_ffn, w_up, w_down, ab_w_in, ab_b_fgate, ab_q_norm, ab_k_norm, ab_pool_map, ab_pool_scale,
           ab_w_out, nsa_w_in, nsa_b_gate, nsa_q_norm, nsa_k_norm, nsa_cmp_pos_k, nsa_cmp_w1_k, nsa_cmp_w2_k,
           nsa_cmp_pos_v, nsa_cmp_w1_v, nsa_cmp_w2_v, nsa_w_out):
    assert w_mod.shape[0] == 2, "one forgetting/pooling layer followed by one sparse-attention layer"
    params = {
        "norm_mix": norm_mix, "norm_ffn": norm_ffn, "w_up": w_up, "w_down": w_down,
        "ab_w_in": ab_w_in, "ab_b_fgate": ab_b_fgate, "ab_q_norm": ab_q_norm, "ab_k_norm": ab_k_norm,
        "ab_pool_map": ab_pool_map, "ab_pool_scale": ab_pool_scale, "ab_w_out": ab_w_out,
        "nsa_w_in": nsa_w_in, "nsa_b_gate": nsa_b_gate, "nsa_q_norm": nsa_q_norm, "nsa_k_norm": nsa_k_norm,
        "nsa_cmp_pos_k": nsa_cmp_pos_k, "nsa_cmp_w1_k": nsa_cmp_w1_k, "nsa_cmp_w2_k": nsa_cmp_w2_k,
        "nsa_cmp_pos_v": nsa_cmp_pos_v, "nsa_cmp_w1_v": nsa_cmp_w1_v, "nsa_cmp_w2_v": nsa_cmp_w2_v,
        "nsa_w_out": nsa_w_out,
    }
    past = {
        "cache_fox_k": cache_fox_k, "cache_fox_v": cache_fox_v, "cache_fox_logf": cache_fox_logf,
        "state_pool": state_pool, "cache_nsa_kc": cache_nsa_kc, "cache_nsa_vc": cache_nsa_vc,
        "cache_nsa_ks": cache_nsa_ks, "cache_nsa_vs": cache_nsa_vs,
        "state_nsa_kw": state_nsa_kw, "state_nsa_vw": state_nsa_vw,
    }
    n_prompt = c_prompt.shape[0]
    mod = _modulation(jnp.concatenate([c_prompt, c_sample], axis=0), w_mod, b_mod)
    y_prompt, sp = _prompt_trunk(x_prompt, mod[:, :n_prompt], params)
    y_sample, ss = _sample_trunk(x_sample, mod[:, n_prompt:], params, past, page_table)
    return (y_prompt, y_sample, *[sp[n][None] for n in _STATE_NAMES], *[ss[n][None] for n in _STATE_NAMES])
```

```python
import functools

import jax
import jax.numpy as jnp
from jax import lax
from jax.experimental import pallas as pl
from jax.experimental.pallas import tpu as pltpu

F32 = jnp.float32
BF16 = jnp.bfloat16

D_MODEL = 1024
HEAD_DIM = 64
EPS = 1e-6
ROPE_THETA = 10000.0
NEG_INF = -1e30
MASKED = -2e30
BIG = 1e9
PAGE = 128
FOX_HEADS = 8
FOX_WIDTH = FOX_HEADS * HEAD_DIM
POOL_WINDOWS = (2, 4, 8, 16)
POOL_WIDTH = 512
POOL_GROUP_CH = 128
POOL_HALO = 16
NSA_HEADS = 16
NSA_GROUPS = 4
NSA_REP = NSA_HEADS // NSA_GROUPS
NSA_KV_WIDTH = NSA_GROUPS * HEAD_DIM
NSA_BRANCHES = 3
CMP_BLOCK = 32
CMP_STRIDE = 16
SEL_BLOCK = 64
SEL_TOPK = 16
WINDOW = 512
D_FF = 2816
LANE = 128
SUBLANE = 8
GROUPS_PER_TILE = LANE // HEAD_DIM
PAGES_PER_STEP = 8
VMEM_LIMIT = 48 * 1024 * 1024


def _params(*sem):
    return pltpu.CompilerParams(dimension_semantics=sem, vmem_limit_bytes=VMEM_LIMIT)


def _nt(a, b):
    return lax.dot_general(a, b, (((1,), (1,)), ((), ())), preferred_element_type=F32)


def _mm(a, b):
    return jnp.dot(a, b, preferred_element_type=F32)


def _mod_body(c_ref, w_ref, b_ref, o_ref):
    c = c_ref[...]
    a = (c * jax.nn.sigmoid(c)).astype(BF16)
    o_ref[0] = _mm(a, w_ref[0].astype(BF16)) + b_ref[0]


def _modulation(c_all, w_mod, b_mod):
    n_layers, d, n = w_mod.shape
    rows = c_all.shape[0]
    tn = 1536
    return pl.pallas_call(
        _mod_body,
        grid=(n_layers, n // tn),
        in_specs=[
            pl.BlockSpec((rows, d), lambda l, j: (0, 0)),
            pl.BlockSpec((1, d, tn), lambda l, j: (l, 0, j)),
            pl.BlockSpec((1, 1, tn), lambda l, j: (l, 0, j)),
        ],
        out_specs=pl.BlockSpec((1, rows, tn), lambda l, j: (l, 0, j)),
        out_shape=jax.ShapeDtypeStruct((n_layers, rows, n), F32),
        compiler_params=_params("arbitrary", "arbitrary"),
        name="modulation",
    )(c_all, w_mod, b_mod.reshape(n_layers, 1, n))


def _modulated_norm(x, gain, sc, sh):
    xn = x * lax.rsqrt(jnp.mean(x * x, axis=-1, keepdims=True) + EPS) * gain
    return xn * (1.0 + sc) + sh


def _proj_body(*refs, has_side, has_t):
    x_ref, g_ref, sc_ref, sh_ref, w_ref = refs[:5]
    k = 5
    side_ref = t_ref = None
    if has_side:
        side_ref = refs[k]; k += 1
    if has_t:
        t_ref = refs[k]; k += 1
    y_ref = refs[k]; k += 1
    yside_ref = yt_ref = None
    if has_side:
        yside_ref = refs[k]; k += 1
    if has_t:
        yt_ref = refs[k]; k += 1
    h_scr = refs[k]

    @pl.when(pl.program_id(2) == 0)
    def _():
        hb = _modulated_norm(x_ref[0], g_ref[...], sc_ref[0], sh_ref[0]).astype(BF16)
        h_scr[...] = hb
        if has_side:
            yside_ref[0] = _mm(hb, side_ref[...].astype(BF16))
        if has_t:
            yt_ref[0] = _nt(t_ref[...].astype(BF16), hb)

    y_ref[0] = _mm(h_scr[...], w_ref[...].astype(BF16))


def _project(x, gain, sc, sh, w, n_cols, tn, tm, w_side=None, w_t=None):
    bx, tx, d = x.shape
    r = sc.shape[1]
    mod_spec = (pl.BlockSpec((1, tm, d), lambda b, i, j: (b, i, 0)) if r == tx
                else pl.BlockSpec((1, 1, d), lambda b, i, j: (b, 0, 0)))
    in_specs = [
        pl.BlockSpec((1, tm, d), lambda b, i, j: (b, i, 0)),
        pl.BlockSpec((1, d), lambda b, i, j: (0, 0)),
        mod_spec, mod_spec,
        pl.BlockSpec((d, tn), lambda b, i, j: (0, j)),
    ]
    args = [x, gain.reshape(1, d), sc, sh, w]
    out_specs = [pl.BlockSpec((1, tm, tn), lambda b, i, j: (b, i, j))]
    out_shape = [jax.ShapeDtypeStruct((bx, tx, n_cols), F32)]
    if w_side is not None:
        ns = w_side.shape[1]
        in_specs.append(pl.BlockSpec((d, ns), lambda b, i, j: (0, 0)))
        args.append(w_side)
        out_specs.append(pl.BlockSpec((1, tm, ns), lambda b, i, j: (b, i, 0)))
        out_shape.append(jax.ShapeDtypeStruct((bx, tx, ns), F32))
    if w_t is not None:
        nt = w_t.shape[0]
        in_specs.append(pl.BlockSpec((nt, d), lambda b, i, j: (0, 0)))
        args.append(w_t)
        out_specs.append(pl.BlockSpec((1, nt, tm), lambda b, i, j: (b, 0, i)))
        out_shape.append(jax.ShapeDtypeStruct((bx, nt, tx), F32))
    return pl.pallas_call(
        functools.partial(_proj_body, has_side=w_side is not None, has_t=w_t is not None),
        grid=(bx, tx // tm, n_cols // tn),
        in_specs=in_specs, out_specs=out_specs, out_shape=out_shape,
        scratch_shapes=[pltpu.VMEM((tm, d), BF16)],
        compiler_params=_params("arbitrary", "arbitrary", "arbitrary"),
        name="project",
    )(*args)


def _headnorm_body(*refs, tiles, rope, has_gate, n_out):
    y_ref, gain_ref = refs[:2]
    k = 2
    cos_ref = sin_ref = gl_ref = bg_ref = None
    if rope:
        cos_ref, sin_ref = refs[k], refs[k + 1]; k += 2
    if has_gate:
        gl_ref, bg_ref = refs[k], refs[k + 1]; k += 2
    outs = refs[k:k + n_out]
    gate_out = refs[k + n_out] if has_gate else None
    lane = lax.broadcasted_iota(jnp.int32, (1, LANE), 1)
    low_head = lane < HEAD_DIM
    first_half = (lane % HEAD_DIM) < (HEAD_DIM // 2)
    for src, oi, dst in tiles:
        y = y_ref[0, :, src * LANE:(src + 1) * LANE]
        y2 = y * y
        s_lo = jnp.sum(jnp.where(low_head, y2, 0.0), axis=-1, keepdims=True)
        s_hi = jnp.sum(jnp.where(low_head, 0.0, y2), axis=-1, keepdims=True)
        ms = jnp.where(low_head, s_lo, s_hi) * (1.0 / HEAD_DIM)
        yn = y * lax.rsqrt(ms + EPS) * gain_ref[:, src * LANE:(src + 1) * LANE]
        if rope:
            partner = jnp.where(first_half, pltpu.roll(yn, LANE - HEAD_DIM // 2, 1), pltpu.roll(yn, HEAD_DIM // 2, 1))
            yn = yn * cos_ref[...] + partner * sin_ref[...]
        outs[oi][0, :, dst * LANE:(dst + 1) * LANE] = yn
    if has_gate:
        gate_out[0] = jax.nn.sigmoid(gl_ref[0] + bg_ref[...])


def _headnorm(y, width, gains, tiles, out_widths, tm, rope_tabs=None, gate_logits=None, gate_bias=None):
    bx, tx, _ = y.shape
    rope = rope_tabs is not None
    has_gate = gate_logits is not None
    in_specs = [pl.BlockSpec((1, tm, width), lambda b, i: (b, i, 0)),
                pl.BlockSpec((1, width), lambda b, i: (0, 0))]
    args = [y, gains]
    if rope:
        in_specs += [pl.BlockSpec((tm, LANE), lambda b, i: (i, 0))] * 2
        args += list(rope_tabs)
    if has_gate:
        in_specs += [pl.BlockSpec((1, tm, LANE), lambda b, i: (b, i, 0)), pl.BlockSpec((1, LANE), lambda b, i: (0, 0))]
        args += [gate_logits, gate_bias]
    out_specs = [pl.BlockSpec((1, tm, w), lambda b, i: (b, i, 0)) for w in out_widths]
    out_shape = [jax.ShapeDtypeStruct((bx, tx, w), F32) for w in out_widths]
    if has_gate:
        out_specs.append(pl.BlockSpec((1, tm, LANE), lambda b, i: (b, i, 0)))
        out_shape.append(jax.ShapeDtypeStruct((bx, tx, LANE), F32))
    return pl.pallas_call(
        functools.partial(_headnorm_body, tiles=tuple(tiles), rope=rope, has_gate=has_gate, n_out=len(out_widths)),
        grid=(bx, tx // tm),
        in_specs=in_specs, out_specs=out_specs, out_shape=out_shape,
        compiler_params=_params("arbitrary", "arbitrary"),
        name="headnorm",
    )(*args)


def _rope_tables(pos):
    half = HEAD_DIM // 2
    inv_freq = ROPE_THETA ** (-jnp.arange(half, dtype=F32) / half)
    ang = pos.astype(F32)[:, None] * inv_freq[None, :]
    cos, sin = jnp.cos(ang), jnp.sin(ang)
    reps = LANE // HEAD_DIM
    return (jnp.tile(jnp.concatenate([cos, cos], axis=1), (1, reps)),
            jnp.tile(jnp.concatenate([-sin, sin], axis=1), (1, reps)))


def _lane_cumsum(x):
    n = x.shape[-1]
    lane = lax.broadcasted_iota(jnp.int32, x.shape, x.ndim - 1)
    s = 1
    while s < n:
        x = x + jnp.where(lane >= s, pltpu.roll(x, s, x.ndim - 1), 0.0)
        s *= 2
    return x


def _logf_body(f_ref, b_ref, lf_ref, cum_ref):
    lf = jax.nn.log_sigmoid(f_ref[0] + b_ref[...])
    lf_ref[0] = lf
    cum_ref[0] = _lane_cumsum(lf)


def _logf_cumsum(f_t, b_fgate):
    bx, h, t = f_t.shape
    spec = pl.BlockSpec((1, h, t), lambda b: (b, 0, 0))
    return pl.pallas_call(
        _logf_body, grid=(bx,),
        in_specs=[spec, pl.BlockSpec((h, 1), lambda b: (0, 0))],
        out_specs=[spec, spec],
        out_shape=[jax.ShapeDtypeStruct(f_t.shape, F32)] * 2,
        compiler_params=_params("arbitrary"),
        name="logf_cumsum",
    )(f_t, b_fgate.reshape(h, 1))


def _half_mask(shape, half):
    lane = lax.broadcasted_iota(jnp.int32, shape, len(shape) - 1)
    return (lane % LANE) // HEAD_DIM == half


def _fox_flash_body(q_ref, k_ref, v_ref, cum_ref, o_ref, qm_scr, m_scr, l_scr, acc_scr, *, tq, nk, scale):
    qi, ki = pl.program_id(1), pl.program_id(2)
    n_pairs = FOX_HEADS // 2
    reps = tq // LANE

    @pl.when(ki == 0)
    def _():
        for h in range(FOX_HEADS):
            c = h // 2
            qpair = q_ref[0, :, c * LANE:(c + 1) * LANE] * scale
            qm_scr[h] = jnp.where(_half_mask(qpair.shape, h % 2), qpair, 0.0).astype(BF16)
        m_scr[...] = jnp.full(m_scr.shape, NEG_INF, F32)
        l_scr[...] = jnp.zeros(l_scr.shape, F32)
        acc_scr[...] = jnp.zeros(acc_scr.shape, F32)

    @pl.when(ki <= qi)
    def _():
        qpos = qi * tq + lax.broadcasted_iota(jnp.int32, (tq, tq), 0)
        kpos = ki * tq + lax.broadcasted_iota(jnp.int32, (tq, tq), 1)
        vis = kpos <= qpos
        for c in range(n_pairs):
            k_pair = k_ref[0, :, c * LANE:(c + 1) * LANE].astype(BF16)
            v_pair = v_ref[0, :, c * LANE:(c + 1) * LANE]
            pv = []
            alphas = []
            for half in range(2):
                h = 2 * c + half
                s = _nt(qm_scr[h], k_pair) - cum_ref[0, h:h + 1, :]
                s = jnp.where(vis, s, MASKED)
                m_prev = m_scr[h]
                m_new = jnp.maximum(m_prev, jnp.max(s, axis=1, keepdims=True))
                alpha = jnp.exp(m_prev - m_new)
                p = jnp.exp(s - jnp.concatenate([m_new] * reps, axis=1))
                l_scr[h] = alpha * l_scr[h] + jnp.sum(p, axis=1, keepdims=True)
                m_scr[h] = m_new
                v_half = jnp.where(_half_mask(v_pair.shape, half), v_pair, 0.0).astype(BF16)
                pv.append(_mm(p.astype(BF16), v_half))
                alphas.append(alpha)
            alpha_pair = jnp.where(_half_mask(alphas[0].shape, 0), alphas[0], alphas[1])
            acc_scr[c] = alpha_pair * acc_scr[c] + (pv[0] + pv[1])

    @pl.when(ki == nk - 1)
    def _():
        for c in range(n_pairs):
            l_pair = jnp.where(_half_mask((tq, LANE), 0), l_scr[2 * c], l_scr[2 * c + 1])
            o_ref[0, :, c * LANE:(c + 1) * LANE] = acc_scr[c] / l_pair


def _fox_flash(qk, v_arr, vcb, cum, tq):
    bx, t, _ = qk.shape
    w = FOX_WIDTH
    nq = t // tq
    kmap = lambda qi, ki: jnp.minimum(ki, qi)
    return pl.pallas_call(
        functools.partial(_fox_flash_body, tq=tq, nk=nq, scale=HEAD_DIM ** -0.5),
        grid=(bx, nq, nq),
        in_specs=[pl.BlockSpec((1, tq, w), lambda b, qi, ki: (b, qi, 0)),
                  pl.BlockSpec((1, tq, w), lambda b, qi, ki: (b, kmap(qi, ki), 1)),
                  pl.BlockSpec((1, tq, w), lambda b, qi, ki: (b, kmap(qi, ki), vcb)),
                  pl.BlockSpec((1, FOX_HEADS, tq), lambda b, qi, ki: (b, 0, kmap(qi, ki)))],
        out_specs=pl.BlockSpec((1, tq, w), lambda b, qi, ki: (b, qi, 0)),
        out_shape=jax.ShapeDtypeStruct((bx, t, w), F32),
        scratch_shapes=[pltpu.VMEM((FOX_HEADS, tq, LANE), BF16), pltpu.VMEM((FOX_HEADS, tq, LANE), F32),
                        pltpu.VMEM((FOX_HEADS, tq, LANE), F32), pltpu.VMEM((FOX_HEADS // 2, tq, LANE), F32)],
        compiler_params=_params("arbitrary", "arbitrary", "arbitrary"),
        name="fox_flash",
    )(qk, qk, v_arr, cum)


def _nsa_flash_body(*refs, tq, nk, mode, has_sel, gate_branch, scale):
    q_ref, k_ref, v_ref = refs[:3]
    i = 3
    sel_ref = None
    if has_sel:
        sel_ref = refs[i]; i += 1
    gate_ref, o_ref, qt_scr, m_scr, l_scr, acc_scr = refs[i:i + 6]
    qi, ki = pl.program_id(1), pl.program_id(2)
    kt = ki if mode == "causal" else qi - (nk - 1) + ki
    active = (ki <= qi) if mode == "causal" else (kt >= 0)

    @pl.when(ki == 0)
    def _():
        zeros = jnp.zeros((HEAD_DIM, tq), F32)
        for pair in range(NSA_HEADS // 2):
            q_pair_t = (q_ref[0, :, pair * LANE:(pair + 1) * LANE] * scale).T
            for half in range(2):
                g, r = divmod(2 * pair + half, NSA_REP)
                q_t = q_pair_t[half * HEAD_DIM:(half + 1) * HEAD_DIM]
                parts = [q_t, zeros] if g % GROUPS_PER_TILE == 0 else [zeros, q_t]
                qt_scr[g, :, r * tq:(r + 1) * tq] = jnp.concatenate(parts, axis=0).astype(BF16)
        m_scr[...] = jnp.full(m_scr.shape, NEG_INF, F32)
        l_scr[...] = jnp.zeros(l_scr.shape, F32)
        acc_scr[...] = jnp.zeros(acc_scr.shape, F32)

    @pl.when(active)
    def _():
        kpos = kt * tq + lax.broadcasted_iota(jnp.int32, (tq, tq), 0)
        qpos = qi * tq + lax.broadcasted_iota(jnp.int32, (tq, tq), 1)
        vis = kpos <= qpos
        if mode == "window":
            vis = vis & (kpos > qpos - WINDOW)
        if has_sel:
            nsel = sel_ref.shape[2]
            kblk = (kt * tq + lax.broadcasted_iota(jnp.int32, (tq, nsel), 0)) // SEL_BLOCK
            expand = (kblk == lax.broadcasted_iota(jnp.int32, (tq, nsel), 1)).astype(BF16)
        for c in range(NSA_GROUPS // GROUPS_PER_TILE):
            k_pair = k_ref[0, :, c * LANE:(c + 1) * LANE].astype(BF16)
            v_pair_t = v_ref[0, :, c * LANE:(c + 1) * LANE].T
            for gg in range(GROUPS_PER_TILE):
                g = c * GROUPS_PER_TILE + gg
                v_t = v_pair_t[gg * HEAD_DIM:(gg + 1) * HEAD_DIM].astype(BF16)
                vis_g = vis
                if has_sel:
                    vis_g = vis & (_mm(expand, sel_ref[0, g].astype(BF16)) > 0.5)
                vis_w = jnp.concatenate([vis_g] * NSA_REP, axis=1)
                s = jnp.where(vis_w, _mm(k_pair, qt_scr[g]), MASKED)
                m_prev = m_scr[g]
                m_new = jnp.maximum(m_prev, jnp.max(s, axis=0, keepdims=True))
                alpha = jnp.exp(m_prev - m_new)
                p = jnp.exp(s - m_new[0:1])
                l_scr[g] = alpha * l_scr[g] + jnp.sum(p, axis=0, keepdims=True)
                acc_scr[g] = alpha[0:1] * acc_scr[g] + _mm(v_t, p.astype(BF16))
                m_scr[g] = m_new

    @pl.when(ki == nk - 1)
    def _():
        low = _half_mask((tq, LANE), 0)
        for pair in range(NSA_HEADS // 2):
            parts, gate_cols = [], []
            for half in range(2):
                h = 2 * pair + half
                g, r = divmod(h, NSA_REP)
                lanes = slice(r * tq, (r + 1) * tq)
                parts.append(acc_scr[g, :, lanes] / l_scr[g, 0:1, lanes])
                c = h * NSA_BRANCHES + gate_branch
                gate_cols.append(gate_ref[0, :, c:c + 1])
            o_pair = jnp.concatenate(parts, axis=0).T
            o_ref[0, :, pair * LANE:(pair + 1) * LANE] = o_pair * jnp.where(low, gate_cols[0], gate_cols[1])


def _nsa_flash(qn, kn, kcb, y, vcb, gates, gate_branch, tq, mode, sel_t=None):
    bx, t, wq = qn.shape
    wk = NSA_KV_WIDTH
    nq = t // tq
    nk = nq if mode == "causal" else WINDOW // tq + 1
    if mode == "causal":
        kmap = lambda qi, ki: jnp.minimum(ki, qi)
    else:
        kmap = lambda qi, ki: jnp.maximum(qi - (nk - 1) + ki, 0)
    in_specs = [pl.BlockSpec((1, tq, wq), lambda b, qi, ki: (b, qi, 0)),
                pl.BlockSpec((1, tq, wk), lambda b, qi, ki: (b, kmap(qi, ki), kcb)),
                pl.BlockSpec((1, tq, wk), lambda b, qi, ki: (b, kmap(qi, ki), vcb))]
    args = [qn, kn, y]
    if sel_t is not None:
        in_specs.append(pl.BlockSpec((1, NSA_GROUPS, sel_t.shape[2], tq), lambda b, qi, ki: (b, 0, 0, qi)))
        args.append(sel_t)
    in_specs.append(pl.BlockSpec((1, tq, LANE), lambda b, qi, ki: (b, qi, 0)))
    args.append(gates)
    return pl.pallas_call(
        functools.partial(_nsa_flash_body, tq=tq, nk=nk, mode=mode, has_sel=sel_t is not None,
                          gate_branch=gate_branch, scale=HEAD_DIM ** -0.5),
        grid=(bx, nq, nk),
        in_specs=in_specs,
        out_specs=pl.BlockSpec((1, tq, wq), lambda b, qi, ki: (b, qi, 0)),
        out_shape=jax.ShapeDtypeStruct((bx, t, wq), F32),
        scratch_shapes=[pltpu.VMEM((NSA_GROUPS, LANE, NSA_REP * tq), BF16),
                        pltpu.VMEM((NSA_GROUPS, SUBLANE, NSA_REP * tq), F32),
                        pltpu.VMEM((NSA_GROUPS, SUBLANE, NSA_REP * tq), F32),
                        pltpu.VMEM((NSA_GROUPS, HEAD_DIM, NSA_REP * tq), F32)],
        compiler_params=_params("arbitrary", "arbitrary", "arbitrary"),
        name="nsa_flash_" + mode,
    )(*args)


def _pool_body(u_ref, pre_ref, map_ref, scale_ref, o_ref, ext_scr, *, tm, pos0):
    j = pl.program_id(1)

    @pl.when(j == 0)
    def _():
        ext_scr[0:POOL_HALO] = pre_ref[0]

    @pl.when(j > 0)
    def _():
        ext_scr[0:POOL_HALO] = ext_scr[tm:tm + POOL_HALO]

    ext_scr[POOL_HALO:POOL_HALO + tm] = u_ref[0]
    qpos = pos0 + j * tm + lax.broadcasted_iota(jnp.int32, (tm, 1), 0)
    for g, w in enumerate(POOL_WINDOWS):
        lo, hi = g * POOL_GROUP_CH, (g + 1) * POOL_GROUP_CH
        u_new = ext_scr[POOL_HALO:POOL_HALO + tm, lo:hi]
        tot = u_new
        for d in range(1, w):
            tot = tot + ext_scr[POOL_HALO - d:POOL_HALO - d + tm, lo:hi]
        count = jnp.minimum(w, qpos + 1).astype(F32)
        diff = tot / count - u_new
        y = _mm(diff.astype(BF16), map_ref[g].astype(BF16))
        o_ref[0, :, lo:hi] = y * scale_ref[:, lo:hi]


def _pool_mixer(u_arr, ucb, prefix, w_map, scale, pos0, tm):
    bx, t, _ = u_arr.shape
    c = POOL_WIDTH
    return pl.pallas_call(
        functools.partial(_pool_body, tm=tm, pos0=pos0),
        grid=(bx, t // tm),
        in_specs=[pl.BlockSpec((1, tm, c), lambda b, j: (b, j, ucb)),
                  pl.BlockSpec((1, POOL_HALO, c), lambda b, j: (b, 0, 0)),
                  pl.BlockSpec((len(POOL_WINDOWS), POOL_GROUP_CH, POOL_GROUP_CH), lambda b, j: (0, 0, 0)),
                  pl.BlockSpec((1, c), lambda b, j: (0, 0))],
        out_specs=pl.BlockSpec((1, tm, c), lambda b, j: (b, j, 0)),
        out_shape=jax.ShapeDtypeStruct((bx, t, c), F32),
        scratch_shapes=[pltpu.VMEM((POOL_HALO + tm, c), F32)],
        compiler_params=_params("arbitrary", "arbitrary"),
        name="pool_mixer",
    )(u_arr, prefix, w_map, scale.reshape(1, c))


def _outproj_body(*refs, group_sizes):
    n_a = sum(group_sizes)
    a_refs = refs[:n_a]
    w_refs = refs[n_a:n_a + len(group_sizes)]
    res_ref, gate_ref, o_ref = refs[n_a + len(group_sizes):]
    y = None
    k = 0
    for gi, n in enumerate(group_sizes):
        a = a_refs[k][0]
        for r in a_refs[k + 1:k + n]:
            a = a + r[0]
        k += n
        part = _mm(a.astype(BF16), w_refs[gi][...].astype(BF16))
        y = part if y is None else y + part
    o_ref[0] = res_ref[0] + gate_ref[0] * y


def _out_project(groups, w, res, gate, tm, tn):
    bx, tx, d = res.shape
    kg = groups[0][0].shape[-1]
    r = gate.shape[1]
    a_spec = pl.BlockSpec((1, tm, kg), lambda b, i, j: (b, i, 0))
    in_specs, args = [], []
    for grp in groups:
        for a in grp:
            in_specs.append(a_spec); args.append(a)
    for gi in range(len(groups)):
        in_specs.append(pl.BlockSpec((kg, tn), lambda b, i, j, gi=gi: (gi, j)))
        args.append(w)
    in_specs.append(pl.BlockSpec((1, tm, tn), lambda b, i, j: (b, i, j)))
    in_specs.append(pl.BlockSpec((1, tm, tn), lambda b, i, j: (b, i, j)) if r == tx
                    else pl.BlockSpec((1, 1, tn), lambda b, i, j: (b, 0, j)))
    args += [res, gate]
    return pl.pallas_call(
        functools.partial(_outproj_body, group_sizes=tuple(len(g) for g in groups)),
        grid=(bx, tx // tm, d // tn),
        in_specs=in_specs,
        out_specs=pl.BlockSpec((1, tm, tn), lambda b, i, j: (b, i, j)),
        out_shape=jax.ShapeDtypeStruct((bx, tx, d), F32),
        compiler_params=_params("arbitrary", "arbitrary", "arbitrary"),
        name="out_project",
    )(*args)


def _ffn_body(x_ref, g_ref, sc_ref, sh_ref, gate_ref, wg_ref, wu_ref, wd_ref, o_ref, h_scr, acc_scr, *, nf):
    j = pl.program_id(2)

    @pl.when(j == 0)
    def _():
        h_scr[...] = _modulated_norm(x_ref[0], g_ref[...], sc_ref[0], sh_ref[0]).astype(BF16)
        acc_scr[...] = jnp.zeros(acc_scr.shape, F32)

    h = h_scr[...]
    gt = _mm(h, wg_ref[...].astype(BF16))
    up = _mm(h, wu_ref[...].astype(BF16))
    act = (gt * jax.nn.sigmoid(gt)) * up
    acc_scr[...] += _mm(act.astype(BF16), wd_ref[...].astype(BF16))

    @pl.when(j == nf - 1)
    def _():
        o_ref[0] = x_ref[0] + gate_ref[0] * acc_scr[...]


def _ffn(x, gain, sc, sh, gate, w_up, w_down, tm, tf):
    bx, tx, d = x.shape
    nf = D_FF // tf
    r = sc.shape[1]
    mod_spec = (pl.BlockSpec((1, tm, d), lambda b, i, j: (b, i, 0)) if r == tx
                else pl.BlockSpec((1, 1, d), lambda b, i, j: (b, 0, 0)))
    x_spec = pl.BlockSpec((1, tm, d), lambda b, i, j: (b, i, 0))
    return pl.pallas_call(
        functools.partial(_ffn_body, nf=nf),
        grid=(bx, tx // tm, nf),
        in_specs=[x_spec, pl.BlockSpec((1, d), lambda b, i, j: (0, 0)), mod_spec, mod_spec, mod_spec,
                  pl.BlockSpec((d, tf), lambda b, i, j: (0, j)),
                  pl.BlockSpec((d, tf), lambda b, i, j: (0, j + nf)),
                  pl.BlockSpec((tf, d), lambda b, i, j: (j, 0))],
        out_specs=x_spec,
        out_shape=jax.ShapeDtypeStruct(x.shape, F32),
        scratch_shapes=[pltpu.VMEM((tm, d), BF16), pltpu.VMEM((tm, d), F32)],
        compiler_params=_params("arbitrary", "arbitrary", "arbitrary"),
        name="swiglu",
    )(x, gain.reshape(1, d), sc, sh, gate, w_up, w_up, w_down)


def _chunk_rows_body(a_ref, o_ref, *, n_chunks):
    for s in range(CMP_STRIDE):
        rows = a_ref[0, pl.ds(s, n_chunks, stride=CMP_STRIDE), :]
        for g in range(GROUPS_PER_TILE):
            o_ref[0, g, :, s * HEAD_DIM:(s + 1) * HEAD_DIM] = rows[:, g * HEAD_DIM:(g + 1) * HEAD_DIM]


def _chunk_rows(a, acb, tm):
    bx, t, _ = a.shape
    n_chunks = tm // CMP_STRIDE
    tiles = NSA_KV_WIDTH // LANE
    return pl.pallas_call(
        functools.partial(_chunk_rows_body, n_chunks=n_chunks),
        grid=(bx, t // tm, tiles),
        in_specs=[pl.BlockSpec((1, tm, LANE), lambda b, i, c: (b, i, acb * tiles + c))],
        out_specs=pl.BlockSpec((1, GROUPS_PER_TILE, n_chunks, CMP_STRIDE * HEAD_DIM), lambda b, i, c: (b, c, i, 0)),
        out_shape=jax.ShapeDtypeStruct((bx, NSA_GROUPS, t // CMP_STRIDE, CMP_STRIDE * HEAD_DIM), F32),
        compiler_params=_params("arbitrary", "arbitrary", "arbitrary"),
        name="chunk_rows",
    )(a)


def _compress_mlp(a, pos_ref, w1_ref, w2_ref):
    half = CMP_STRIDE * HEAD_DIM
    w1 = w1_ref[...].astype(BF16)
    first = _mm(a, w1[:half])
    second = _mm(a, w1[half:])
    bias = _mm(pos_ref[...].astype(BF16), w1)
    hidden = (first + pltpu.roll(second, a.shape[0] - 1, 0)) + bias
    return _mm(jax.nn.gelu(hidden).astype(BF16), w2_ref[...].astype(BF16))


def _compress_body(a_ref, pos_ref, w1_ref, w2_ref, o_ref):
    o_ref[0, 0] = _compress_mlp(a_ref[0, 0].astype(BF16), pos_ref, w1_ref, w2_ref)


def _compress(chunks, pos_emb, w1, w2):
    bx, g, n_chunks, half = chunks.shape
    hidden = w1.shape[1]
    return pl.pallas_call(
        _compress_body,
        grid=(bx, g),
        in_specs=[pl.BlockSpec((1, 1, n_chunks, half), lambda b, i: (b, i, 0, 0)),
                  pl.BlockSpec((1, 2 * half), lambda b, i: (0, 0)),
                  pl.BlockSpec((2 * half, hidden), lambda b, i: (0, 0)),
                  pl.BlockSpec((hidden, HEAD_DIM), lambda b, i: (0, 0))],
        out_specs=pl.BlockSpec((1, 1, n_chunks, HEAD_DIM), lambda b, i: (b, i, 0, 0)),
        out_shape=jax.ShapeDtypeStruct((bx, g, n_chunks, HEAD_DIM), F32),
        compiler_params=_params("arbitrary", "arbitrary"),
        name="compress",
    )(chunks, pos_emb.reshape(1, 2 * half), w1, w2)


def _compress_pages_body(pt_ref, *rest, n_steps, pps):
    del pt_ref
    page_refs = rest[:pps]
    pos_ref, w1_ref, w2_ref, o_ref, tok_scr, chunk_scr = rest[pps:]
    p = pl.program_id(1)
    cpp = PAGE // CMP_STRIDE
    pairs = NSA_GROUPS // GROUPS_PER_TILE
    for i in range(pps):
        row0 = pl.multiple_of((p * pps + i) * cpp, cpp)
        for c in range(pairs):
            tok = tok_scr.at[i * pairs + c]
            tok[...] = page_refs[i][0, c * GROUPS_PER_TILE:(c + 1) * GROUPS_PER_TILE].reshape(LANE, PAGE).T
            for s in range(CMP_STRIDE):
                rows = tok[pl.ds(s, cpp, stride=CMP_STRIDE), :]
                for g in range(GROUPS_PER_TILE):
                    chunk_scr[c * GROUPS_PER_TILE + g, pl.ds(row0, cpp), s * HEAD_DIM:(s + 1) * HEAD_DIM] = (
                        rows[:, g * HEAD_DIM:(g + 1) * HEAD_DIM])

    @pl.when(p == n_steps - 1)
    def _():
        for g in range(NSA_GROUPS):
            o_ref[0, g] = _compress_mlp(chunk_scr[g].astype(BF16), pos_ref, w1_ref, w2_ref)


def _compress_pages(cache_t, page_table, pos_emb, w1, w2):
    bx, n_pages = page_table.shape
    pps = _pages_per_step(n_pages)
    cpp = PAGE // CMP_STRIDE
    n_chunks = n_pages * cpp
    half = CMP_STRIDE * HEAD_DIM
    hidden = w1.shape[1]
    const = lambda b, p, pt: (0, 0)
    return pl.pallas_call(
        functools.partial(_compress_pages_body, n_steps=n_pages // pps, pps=pps),
        grid_spec=pltpu.PrefetchScalarGridSpec(
            num_scalar_prefetch=1, grid=(bx, n_pages // pps),
            in_specs=[pl.BlockSpec((1, NSA_GROUPS, HEAD_DIM, PAGE), _page_map(n_pages, pps, i, 4)) for i in range(pps)]
            + [pl.BlockSpec((1, 2 * half), const), pl.BlockSpec((2 * half, hidden), const),
               pl.BlockSpec((hidden, HEAD_DIM), const)],
            out_specs=pl.BlockSpec((1, NSA_GROUPS, n_chunks, HEAD_DIM), lambda b, p, pt: (b, 0, 0, 0)),
            scratch_shapes=[pltpu.VMEM((pps * NSA_GROUPS // GROUPS_PER_TILE, PAGE, LANE), F32),
                            pltpu.VMEM((NSA_GROUPS, n_chunks, half), F32)]),
        out_shape=jax.ShapeDtypeStruct((bx, NSA_GROUPS, n_chunks, HEAD_DIM), F32),
        compiler_params=_params("arbitrary", "arbitrary"),
        name="compress_pages",
    )(page_table.reshape(-1), *([cache_t] * pps), pos_emb.reshape(1, 2 * half), w1, w2)


def _cmp_select_body(q_ref, kc_ref, vc_ref, gate_ref, o_ref, sel_ref, *, tq, n_cmp, n_sel, nselp, pos0, scale):
    g, qi = pl.program_id(1), pl.program_id(2)
    ncp = kc_ref.shape[2]
    qpos = pos0 + qi * tq + lax.broadcasted_iota(jnp.int32, (tq, 1), 0)
    c_idx = lax.broadcasted_iota(jnp.int32, (1, ncp), 1)
    c_valid = (c_idx * CMP_STRIDE + (CMP_BLOCK - 1) <= qpos) & (c_idx < n_cmp)
    kc = kc_ref[0, 0].astype(BF16)
    vc = vc_ref[0, 0].astype(BF16)
    pc_sum = jnp.zeros((tq, ncp), F32)
    for r in range(NSA_REP):
        q = (q_ref[0, :, r * HEAD_DIM:(r + 1) * HEAD_DIM] * scale).astype(BF16)
        s = jnp.where(c_valid, _nt(q, kc), NEG_INF)
        m = jnp.max(s, axis=-1, keepdims=True)
        p = jnp.where(c_valid, jnp.exp(s - m), 0.0)
        l = jnp.sum(p, axis=-1, keepdims=True)
        pc = jnp.where(l > 0.0, p / jnp.where(l > 0.0, l, 1.0), 0.0)
        pc_sum = pc_sum + pc
        col = (g * NSA_REP + r) * NSA_BRANCHES
        lane = lax.broadcasted_iota(jnp.int32, (1, LANE), 1)
        gate = jnp.sum(jnp.where(lane == col, gate_ref[0], 0.0), axis=-1, keepdims=True)
        o_ref[0, :, r * HEAD_DIM:(r + 1) * HEAD_DIM] = _mm(pc.astype(BF16), vc) * gate
    cj = lax.broadcasted_iota(jnp.int32, (ncp, nselp), 0) * CMP_STRIDE
    sj = lax.broadcasted_iota(jnp.int32, (ncp, nselp), 1) * SEL_BLOCK
    overlap = ((cj < sj + SEL_BLOCK) & (cj + (CMP_BLOCK - 1) >= sj)).astype(F32)
    imp = jnp.dot(pc_sum, overlap, preferred_element_type=F32, precision=lax.Precision.HIGHEST)
    j_idx = lax.broadcasted_iota(jnp.int32, (1, nselp), 1)
    forced = (j_idx == 0) | (j_idx == qpos // SEL_BLOCK)
    valid = j_idx * SEL_BLOCK <= qpos
    score = jnp.where(valid, jnp.where(forced, BIG, imp), -BIG)
    rank = jnp.zeros((tq, nselp), jnp.int32)
    for i in range(n_sel):
        si = jnp.sum(jnp.where(j_idx == i, score, 0.0), axis=-1, keepdims=True)
        ahead = (si > score) | ((si == score) & (i < j_idx))
        rank = rank + ahead.astype(jnp.int32)
    sel_ref[0, 0] = ((rank < SEL_TOPK) & valid).astype(F32)


def _cmp_select(qn, kcmp, vcmp, gates, n_cmp, n_sel, nselp, pos0, tq):
    bx, t, _ = qn.shape
    ncp = kcmp.shape[2]
    gw = NSA_REP * HEAD_DIM
    return pl.pallas_call(
        functools.partial(_cmp_select_body, tq=tq, n_cmp=n_cmp, n_sel=n_sel, nselp=nselp, pos0=pos0,
                          scale=HEAD_DIM ** -0.5),
        grid=(bx, NSA_GROUPS, t // tq),
        in_specs=[pl.BlockSpec((1, tq, gw), lambda b, g, i: (b, i, g)),
                  pl.BlockSpec((1, 1, ncp, HEAD_DIM), lambda b, g, i: (b, g, 0, 0)),
                  pl.BlockSpec((1, 1, ncp, HEAD_DIM), lambda b, g, i: (b, g, 0, 0)),
                  pl.BlockSpec((1, tq, LANE), lambda b, g, i: (b, i, 0))],
        out_specs=[pl.BlockSpec((1, tq, gw), lambda b, g, i: (b, i, g)),
                   pl.BlockSpec((1, 1, tq, nselp), lambda b, g, i: (b, g, i, 0))],
        out_shape=[jax.ShapeDtypeStruct((bx, t, NSA_HEADS * HEAD_DIM), F32),
                   jax.ShapeDtypeStruct((bx, NSA_GROUPS, t, nselp), F32)],
        compiler_params=_params("arbitrary", "arbitrary", "arbitrary"),
        name="cmp_select",
    )(qn, kcmp, vcmp, gates)


def _cmp_select_t_body(q_ref, kc_ref, vc_ref, gate_ref, o_ref, sel_ref, *, tq, n_cmp, n_sel, pos0, scale):
    g, qi = pl.program_id(1), pl.program_id(2)
    ncp = kc_ref.shape[2]
    nselp = sel_ref.shape[2]
    qpos = pos0 + qi * tq + lax.broadcasted_iota(jnp.int32, (1, tq), 1)
    c_idx = lax.broadcasted_iota(jnp.int32, (ncp, 1), 0)
    c_valid = (c_idx * CMP_STRIDE + (CMP_BLOCK - 1) <= qpos) & (c_idx < n_cmp)
    kc = kc_ref[0, 0].astype(BF16)
    vc_t = vc_ref[0, 0].T.astype(BF16)
    lane = lax.broadcasted_iota(jnp.int32, (1, LANE), 1)
    pc_sum = jnp.zeros((ncp, tq), F32)
    for r in range(NSA_REP):
        q_t = (q_ref[0, :, r * HEAD_DIM:(r + 1) * HEAD_DIM] * scale).T.astype(BF16)
        s = jnp.where(c_valid, _mm(kc, q_t), NEG_INF)
        m = jnp.max(s, axis=0, keepdims=True)
        p = jnp.where(c_valid, jnp.exp(s - m), 0.0)
        l = jnp.sum(p, axis=0, keepdims=True)
        pc = jnp.where(l > 0.0, p / jnp.where(l > 0.0, l, 1.0), 0.0)
        pc_sum = pc_sum + pc
        col = (g * NSA_REP + r) * NSA_BRANCHES
        gate = jnp.sum(jnp.where(lane == col, gate_ref[0], 0.0), axis=-1, keepdims=True)
        o_ref[0, :, r * HEAD_DIM:(r + 1) * HEAD_DIM] = _mm(vc_t, pc.astype(BF16)).T * gate
    sj = lax.broadcasted_iota(jnp.int32, (nselp, ncp), 0) * SEL_BLOCK
    cj = lax.broadcasted_iota(jnp.int32, (nselp, ncp), 1) * CMP_STRIDE
    overlap_t = ((cj < sj + SEL_BLOCK) & (cj + (CMP_BLOCK - 1) >= sj)).astype(F32)
    imp = jnp.dot(overlap_t, pc_sum, preferred_element_type=F32, precision=lax.Precision.HIGHEST)
    j_idx = lax.broadcasted_iota(jnp.int32, (nselp, 1), 0)
    forced = (j_idx == 0) | (j_idx == qpos // SEL_BLOCK)
    valid = j_idx * SEL_BLOCK <= qpos
    score = jnp.where(valid, jnp.where(forced, BIG, imp), -BIG)
    rank = jnp.zeros((nselp, tq), jnp.int32)
    for i in range(n_sel):
        si = score[i:i + 1, :]
        ahead = (si > score) | ((si == score) & (i < j_idx))
        rank = rank + ahead.astype(jnp.int32)
    sel_ref[0, 0] = ((rank < SEL_TOPK) & valid).astype(F32)


def _cmp_select_t(qn, kcmp, vcmp, gates, n_cmp, n_sel, pos0, tq):
    bx, t, _ = qn.shape
    ncp = kcmp.shape[2]
    gw = NSA_REP * HEAD_DIM
    nselp = -(-n_sel // SUBLANE) * SUBLANE
    return pl.pallas_call(
        functools.partial(_cmp_select_t_body, tq=tq, n_cmp=n_cmp, n_sel=n_sel, pos0=pos0, scale=HEAD_DIM ** -0.5),
        grid=(bx, NSA_GROUPS, t // tq),
        in_specs=[pl.BlockSpec((1, tq, gw), lambda b, g, i: (b, i, g)),
                  pl.BlockSpec((1, 1, ncp, HEAD_DIM), lambda b, g, i: (b, g, 0, 0)),
                  pl.BlockSpec((1, 1, ncp, HEAD_DIM), lambda b, g, i: (b, g, 0, 0)),
                  pl.BlockSpec((1, tq, LANE), lambda b, g, i: (b, i, 0))],
        out_specs=[pl.BlockSpec((1, tq, gw), lambda b, g, i: (b, i, g)),
                   pl.BlockSpec((1, 1, nselp, tq), lambda b, g, i: (b, g, 0, i))],
        out_shape=[jax.ShapeDtypeStruct((bx, t, NSA_HEADS * HEAD_DIM), F32),
                   jax.ShapeDtypeStruct((bx, NSA_GROUPS, nselp, t), F32)],
        compiler_params=_params("arbitrary", "arbitrary", "arbitrary"),
        name="cmp_select_t",
    )(qn, kcmp, vcmp, gates)


def _pages_per_step(n_pages):
    pps = min(PAGES_PER_STEP, n_pages)
    assert n_pages % pps == 0
    return pps


def _page_map(n_pages, pps, i, rank):
    return lambda b, p, pt: (pt[b * n_pages + p * pps + i],) + (0,) * (rank - 1)


def _softmax_step(s, vis, m_scr, l_scr, acc_scr, pv_fn):
    m_prev = m_scr[...]
    m_new = jnp.maximum(m_prev, jnp.max(s, axis=-1, keepdims=True))
    alpha = jnp.exp(m_prev - m_new)
    p = jnp.exp(s - m_new)
    if vis is not None:
        p = jnp.where(vis, p, 0.0)
    l_scr[...] = alpha * l_scr[...] + jnp.sum(p, axis=-1, keepdims=True)
    acc_scr[...] = alpha * acc_scr[...] + pv_fn(p)
    m_scr[...] = m_new


def _new_rows_step(qbd, k_new, v_new, row_t, bias_cols, m_scr, l_scr, acc_scr, tnew):
    cols = []
    for j in range(tnew):
        sj = jnp.sum(qbd * k_new[j:j + 1, :], axis=-1, keepdims=True)
        if bias_cols is not None:
            sj = sj - bias_cols[j]
        cols.append(jnp.where(row_t >= j, sj, NEG_INF))
    m_prev = m_scr[...]
    m_new = m_prev
    for sj in cols:
        m_new = jnp.maximum(m_new, sj)
    alpha = jnp.exp(m_prev - m_new)
    l = alpha * l_scr[...]
    acc = alpha * acc_scr[...]
    for j, sj in enumerate(cols):
        pj = jnp.where(row_t >= j, jnp.exp(sj - m_new), 0.0)
        l = l + pj
        acc = acc + pj * v_new[j:j + 1, :]
    return acc / l


def _init_softmax(m_scr, l_scr, acc_scr):
    m_scr[...] = jnp.full(m_scr.shape, NEG_INF, F32)
    l_scr[...] = jnp.zeros(l_scr.shape, F32)
    acc_scr[...] = jnp.zeros(acc_scr.shape, F32)


def _fox_sample_body(pt_ref, q_ref, kn_ref, vn_ref, fn_ref, bf_ref, *rest, n_steps, pps, tnew, scale):
    del pt_ref
    kt_refs, vt_refs, lf_refs = rest[:pps], rest[pps:2 * pps], rest[2 * pps:3 * pps]
    o_ref, lfo_ref, qbd_scr, m_scr, l_scr, acc_scr, carry_scr = rest[3 * pps:]
    p = pl.program_id(1)
    nh, w = FOX_HEADS, FOX_WIDTH
    rows = tnew * nh
    head_of_lane = lax.broadcasted_iota(jnp.int32, (nh, w), 1) // HEAD_DIM
    hmask = (head_of_lane == lax.broadcasted_iota(jnp.int32, (nh, w), 0)).astype(F32)

    @pl.when(p == 0)
    def _():
        for t in range(tnew):
            qbd_scr[t * nh:(t + 1) * nh, :] = q_ref[0, t:t + 1, :] * scale * hmask
        _init_softmax(m_scr, l_scr, acc_scr)
        carry_scr[...] = jnp.zeros(carry_scr.shape, F32)

    tri = (lax.broadcasted_iota(jnp.int32, (PAGE, PAGE), 0) <= lax.broadcasted_iota(jnp.int32, (PAGE, PAGE), 1)
           ).astype(F32)
    carry = carry_scr[...]
    cums = []
    for i in range(pps):
        within = jnp.dot(lf_refs[i][0], tri, preferred_element_type=F32, precision=lax.Precision.HIGHEST)
        cums.append(carry + within)
        carry = carry + within[:, PAGE - 1:PAGE]
    carry_scr[...] = carry
    bias = jnp.concatenate([jnp.concatenate(cums, axis=1)] * tnew, axis=0)
    kt = jnp.concatenate([r[0].reshape(w, PAGE).astype(BF16) for r in kt_refs], axis=1)
    vt = jnp.concatenate([r[0].reshape(w, PAGE).astype(BF16) for r in vt_refs], axis=1)
    s = _mm(qbd_scr[...].astype(BF16), kt) - bias
    _softmax_step(s, None, m_scr, l_scr, acc_scr, lambda pr: _nt(pr.astype(BF16), vt))

    @pl.when(p == n_steps - 1)
    def _():
        lf_new = jax.nn.log_sigmoid(fn_ref[0] + bf_ref[...])
        lfo_ref[0] = lf_new
        run = carry_scr[...]
        bias_cols = []
        for j in range(tnew):
            run = run + lf_new[:, j:j + 1]
            bias_cols.append(jnp.concatenate([run] * tnew, axis=0))
        row_t = lax.broadcasted_iota(jnp.int32, (rows, 1), 0) // nh
        o = _new_rows_step(qbd_scr[...], kn_ref[0], vn_ref[0], row_t, bias_cols, m_scr, l_scr, acc_scr, tnew)
        o = o * jnp.concatenate([hmask] * tnew, axis=0)
        for t in range(tnew):
            o_ref[0, t:t + 1, :] = jnp.sum(o[t * nh:(t + 1) * nh], axis=0, keepdims=True)


def _fox_sample(q, k_new, v_new, f_new, b_fgate, cache_kt, cache_vt, cache_lft, page_table):
    bx, tnew, w = q.shape
    n_pages = page_table.shape[1]
    pps = _pages_per_step(n_pages)
    nh = FOX_HEADS
    rows = tnew * nh
    new_spec = pl.BlockSpec((1, tnew, w), lambda b, p, pt: (b, 0, 0))
    kv_specs = [pl.BlockSpec((1, nh, HEAD_DIM, PAGE), _page_map(n_pages, pps, i, 4)) for i in range(pps)]
    lf_specs = [pl.BlockSpec((1, nh, PAGE), _page_map(n_pages, pps, i, 3)) for i in range(pps)]
    return pl.pallas_call(
        functools.partial(_fox_sample_body, n_steps=n_pages // pps, pps=pps, tnew=tnew, scale=HEAD_DIM ** -0.5),
        grid_spec=pltpu.PrefetchScalarGridSpec(
            num_scalar_prefetch=1, grid=(bx, n_pages // pps),
            in_specs=[new_spec, new_spec, new_spec,
                      pl.BlockSpec((1, nh, tnew), lambda b, p, pt: (b, 0, 0)),
                      pl.BlockSpec((nh, 1), lambda b, p, pt: (0, 0))] + kv_specs + kv_specs + lf_specs,
            out_specs=[new_spec, pl.BlockSpec((1, nh, tnew), lambda b, p, pt: (b, 0, 0))],
            scratch_shapes=[pltpu.VMEM((rows, w), F32), pltpu.VMEM((rows, 1), F32), pltpu.VMEM((rows, 1), F32),
                            pltpu.VMEM((rows, w), F32), pltpu.VMEM((nh, 1), F32)]),
        out_shape=[jax.ShapeDtypeStruct((bx, tnew, w), F32), jax.ShapeDtypeStruct((bx, nh, tnew), F32)],
        compiler_params=_params("arbitrary", "arbitrary"),
        name="fox_sample",
    )(page_table.reshape(-1), q, k_new, v_new, f_new, b_fgate.reshape(nh, 1),
      *([cache_kt] * pps), *([cache_vt] * pps), *([cache_lft] * pps))


def _fill_group_queries(qbd_scr, q_ref, tnew, scale):
    qbd_scr[...] = jnp.zeros(qbd_scr.shape, F32)
    for h in range(NSA_HEADS):
        g = h // NSA_REP
        qbd_scr[h * tnew:(h + 1) * tnew, g * HEAD_DIM:(g + 1) * HEAD_DIM] = (
            q_ref[0, :, h * HEAD_DIM:(h + 1) * HEAD_DIM] * scale)


def _write_group_heads(o_ref, o, gate_ref, branch, tnew):
    for h in range(NSA_HEADS):
        g = h // NSA_REP
        c = h * NSA_BRANCHES + branch
        o_ref[0, :, h * HEAD_DIM:(h + 1) * HEAD_DIM] = (
            o[h * tnew:(h + 1) * tnew, g * HEAD_DIM:(g + 1) * HEAD_DIM] * gate_ref[0, :, c:c + 1])


def _sel_sample_body(pt_ref, q_ref, kn_ref, vn_ref, selrows_ref, gate_ref, *rest, n_steps, pps, tnew, scale):
    del pt_ref
    kt_refs, vt_refs = rest[:pps], rest[pps:2 * pps]
    o_ref, qbd_scr, m_scr, l_scr, acc_scr = rest[2 * pps:]
    p = pl.program_id(1)
    rows = NSA_HEADS * tnew
    nselp = selrows_ref.shape[-1]
    keys = pps * PAGE

    @pl.when(p == 0)
    def _():
        _fill_group_queries(qbd_scr, q_ref, tnew, scale)
        _init_softmax(m_scr, l_scr, acc_scr)

    blk_of_lane = (p * keys + lax.broadcasted_iota(jnp.int32, (nselp, keys), 1)) // SEL_BLOCK
    expand = (lax.broadcasted_iota(jnp.int32, (nselp, keys), 0) == blk_of_lane).astype(BF16)
    vis = _mm(selrows_ref[0].astype(BF16), expand) > 0.5
    kt = jnp.concatenate([r[0].reshape(NSA_KV_WIDTH, PAGE).astype(BF16) for r in kt_refs], axis=1)
    vt = jnp.concatenate([r[0].reshape(NSA_KV_WIDTH, PAGE).astype(BF16) for r in vt_refs], axis=1)
    s = jnp.where(vis, _mm(qbd_scr[...].astype(BF16), kt), NEG_INF)
    _softmax_step(s, vis, m_scr, l_scr, acc_scr, lambda pr: _nt(pr.astype(BF16), vt))

    @pl.when(p == n_steps - 1)
    def _():
        row_t = lax.broadcasted_iota(jnp.int32, (rows, 1), 0) % tnew
        o = _new_rows_step(qbd_scr[...], kn_ref[0], vn_ref[0], row_t, None, m_scr, l_scr, acc_scr, tnew)
        _write_group_heads(o_ref, o, gate_ref, 1, tnew)


def _sel_sample(q, k_new, v_new, selrows, gates, cache_kt, cache_vt, page_table):
    bx, tnew, wq = q.shape
    n_pages = page_table.shape[1]
    rows = NSA_HEADS * tnew
    wk = NSA_KV_WIDTH
    pps = _pages_per_step(n_pages)
    fixed = lambda b, p, pt: (b, 0, 0)
    kv_specs = [pl.BlockSpec((1, NSA_GROUPS, HEAD_DIM, PAGE), _page_map(n_pages, pps, i, 4)) for i in range(pps)]
    return pl.pallas_call(
        functools.partial(_sel_sample_body, n_steps=n_pages // pps, pps=pps, tnew=tnew, scale=HEAD_DIM ** -0.5),
        grid_spec=pltpu.PrefetchScalarGridSpec(
            num_scalar_prefetch=1, grid=(bx, n_pages // pps),
            in_specs=[pl.BlockSpec((1, tnew, wq), fixed), pl.BlockSpec((1, tnew, wk), fixed),
                      pl.BlockSpec((1, tnew, wk), fixed), pl.BlockSpec((1, rows, selrows.shape[-1]), fixed),
                      pl.BlockSpec((1, tnew, LANE), fixed)] + kv_specs + kv_specs,
            out_specs=pl.BlockSpec((1, tnew, wq), fixed),
            scratch_shapes=[pltpu.VMEM((rows, wk), F32), pltpu.VMEM((rows, 1), F32), pltpu.VMEM((rows, 1), F32),
                            pltpu.VMEM((rows, wk), F32)]),
        out_shape=jax.ShapeDtypeStruct((bx, tnew, wq), F32),
        compiler_params=_params("arbitrary", "arbitrary"),
        name="sel_sample",
    )(page_table.reshape(-1), q, k_new, v_new, selrows, gates, *([cache_kt] * pps), *([cache_vt] * pps))


def _win_sample_body(q_ref, kn_ref, vn_ref, gate_ref, kt_ref, vt_ref, o_ref, qbd_scr, m_scr, l_scr, acc_scr,
                     *, tnew, wbuf, scale):
    rows = NSA_HEADS * tnew
    _fill_group_queries(qbd_scr, q_ref, tnew, scale)
    _init_softmax(m_scr, l_scr, acc_scr)
    row_t = lax.broadcasted_iota(jnp.int32, (rows, 1), 0) % tnew
    vis = lax.broadcasted_iota(jnp.int32, (rows, wbuf), 1) > row_t + (wbuf - WINDOW)
    kt = kt_ref[0].reshape(NSA_KV_WIDTH, wbuf).astype(BF16)
    vt = vt_ref[0].reshape(NSA_KV_WIDTH, wbuf).astype(BF16)
    s = jnp.where(vis, _mm(qbd_scr[...].astype(BF16), kt), NEG_INF)
    _softmax_step(s, vis, m_scr, l_scr, acc_scr, lambda pr: _nt(pr.astype(BF16), vt))
    o = _new_rows_step(qbd_scr[...], kn_ref[0], vn_ref[0], row_t, None, m_scr, l_scr, acc_scr, tnew)
    _write_group_heads(o_ref, o, gate_ref, 2, tnew)


def _win_sample(q, k_new, v_new, gates, buf_kt, buf_vt):
    bx, tnew, wq = q.shape
    wbuf = buf_kt.shape[-1]
    rows = NSA_HEADS * tnew
    wk = NSA_KV_WIDTH
    fixed = lambda b: (b, 0, 0)
    buf = pl.BlockSpec((1, NSA_GROUPS, HEAD_DIM, wbuf), lambda b: (b, 0, 0, 0))
    return pl.pallas_call(
        functools.partial(_win_sample_body, tnew=tnew, wbuf=wbuf, scale=HEAD_DIM ** -0.5),
        grid=(bx,),
        in_specs=[pl.BlockSpec((1, tnew, wq), fixed), pl.BlockSpec((1, tnew, wk), fixed),
                  pl.BlockSpec((1, tnew, wk), fixed), pl.BlockSpec((1, tnew, LANE), fixed), buf, buf],
        out_specs=pl.BlockSpec((1, tnew, wq), fixed),
        out_shape=jax.ShapeDtypeStruct((bx, tnew, wq), F32),
        scratch_shapes=[pltpu.VMEM((rows, wk), F32), pltpu.VMEM((rows, 1), F32), pltpu.VMEM((rows, 1), F32),
                        pltpu.VMEM((rows, wk), F32)],
        compiler_params=_params("arbitrary"),
        name="win_sample",
    )(q, k_new, v_new, gates, buf_kt, buf_vt)


def _tile_heads(v, n):
    return jnp.tile(v.astype(F32), n)


def _mix_ab(x, mods, pos0, past, p, tiles):
    bx, tx, _ = x.shape
    w_in = p["ab_w_in"]
    split_f = 3 * FOX_WIDTH
    w_main = jnp.concatenate([w_in[:, :split_f], w_in[:, split_f + FOX_HEADS:]], axis=1)
    w_f_t = w_in[:, split_f:split_f + FOX_HEADS].T
    y, f_t = _project(x, p["norm_mix"], mods["sc1"], mods["sh1"], w_main, w_main.shape[1], 512, tiles["tm"],
                      w_t=w_f_t)
    gains = jnp.concatenate([_tile_heads(p["ab_q_norm"], FOX_HEADS), _tile_heads(p["ab_k_norm"], FOX_HEADS)])
    n_tiles = 2 * FOX_WIDTH // LANE
    (qk,) = _headnorm(y, 2 * FOX_WIDTH, gains.reshape(1, -1), [(c, 0, c) for c in range(n_tiles)],
                      [2 * FOX_WIDTH], tiles["tm_norm"])
    return y, f_t, qk


def _nsa_project(x, mods, pos_rows, p, tiles):
    w_in = p["nsa_w_in"]
    n_main = NSA_HEADS * HEAD_DIM + 6 * NSA_KV_WIDTH
    n_gate = NSA_BRANCHES * NSA_HEADS
    w_gate = jnp.pad(w_in[:, n_main:], ((0, 0), (0, LANE - n_gate)))
    y, gl = _project(x, p["norm_mix"], mods["sc1"], mods["sh1"], w_in, n_main, 512, tiles["tm"], w_side=w_gate)
    kn3 = p["nsa_k_norm"]
    qw = NSA_HEADS * HEAD_DIM
    gains = jnp.concatenate([
        _tile_heads(p["nsa_q_norm"], NSA_HEADS),
        _tile_heads(kn3[0], NSA_GROUPS), jnp.ones((NSA_KV_WIDTH,), F32),
        _tile_heads(kn3[1], NSA_GROUPS), jnp.ones((NSA_KV_WIDTH,), F32),
        _tile_heads(kn3[2], NSA_GROUPS)])
    width = gains.shape[0]
    qt = qw // LANE
    kt = NSA_KV_WIDTH // LANE
    tile_map = [(c, 0, c) for c in range(qt)]
    for i in range(NSA_BRANCHES):
        tile_map += [(qt + 2 * i * kt + c, 1, i * kt + c) for c in range(kt)]
    b_gate = jnp.pad(p["nsa_b_gate"], (0, LANE - n_gate)).reshape(1, LANE)
    qn, kn, gates = _headnorm(y, width, gains.reshape(1, -1), tile_map, [qw, NSA_BRANCHES * NSA_KV_WIDTH],
                              tiles["tm_norm"], rope_tabs=_rope_tables(pos_rows), gate_logits=gl, gate_bias=b_gate)
    return y, qn, kn, gates


def _layer_params(params, layer):
    e = layer // 2
    p = {"norm_mix": params["norm_mix"][layer], "norm_ffn": params["norm_ffn"][layer],
         "w_up": params["w_up"][layer], "w_down": params["w_down"][layer]}
    prefix = "ab_" if layer % 2 == 0 else "nsa_"
    for k, v in params.items():
        if k.startswith(prefix):
            p[k] = v[e]
    return p


def _heads(a, n):
    return a.reshape(a.shape[0], a.shape[1], n, HEAD_DIM)


def _prompt_trunk(x, mod, params):
    bx, t, d = x.shape
    tiles = {"tm": 512, "tm_norm": 256}
    states = {}
    for layer in range(mod.shape[0]):
        p = _layer_params(params, layer)
        sh1, sc1, g1, sh2, sc2, g2 = [m[:, None, :] for m in jnp.split(mod[layer], 6, axis=-1)]
        mods = {"sc1": sc1, "sh1": sh1}
        if layer % 2 == 0:
            y, f_t, qk = _mix_ab(x, mods, 0, None, p, tiles)
            lf_t, cum = _logf_cumsum(f_t, p["ab_b_fgate"])
            o_fox = _fox_flash(qk, y, 2, cum, 512)
            o_pool = _pool_mixer(y, 3, jnp.zeros((bx, POOL_HALO, POOL_WIDTH), F32), p["ab_pool_map"],
                                 p["ab_pool_scale"], 0, 512)
            x = _out_project([[o_fox], [o_pool]], p["ab_w_out"], x, g1, 512, 512)
            states["fox_k"] = _heads(qk[:, :, FOX_WIDTH:], FOX_HEADS)
            states["fox_v"] = _heads(y[:, :, 2 * FOX_WIDTH:3 * FOX_WIDTH], FOX_HEADS)
            states["fox_logf"] = lf_t.transpose(0, 2, 1)
            states["pool"] = y[:, t - (POOL_HALO - 1):, 3 * FOX_WIDTH:]
        else:
            y, qn, kn, gates = _nsa_project(x, mods, jnp.arange(t), p, tiles)
            qw, kw_ = NSA_HEADS * HEAD_DIM, NSA_KV_WIDTH
            n_chunk = t // CMP_STRIDE
            kcmp = _compress(_chunk_rows(kn, 0, t), p["nsa_cmp_pos_k"], p["nsa_cmp_w1_k"], p["nsa_cmp_w2_k"])
            vcmp = _compress(_chunk_rows(y, (qw + kw_) // kw_, t), p["nsa_cmp_pos_v"], p["nsa_cmp_w1_v"],
                             p["nsa_cmp_w2_v"])
            n_sel = -(-t // SEL_BLOCK)
            o_cmp, sel_t = _cmp_select_t(qn, kcmp, vcmp, gates, n_chunk - 1, n_sel, 0, 256)
            o_sel = _nsa_flash(qn, kn, 1, y, (qw + 3 * kw_) // kw_, gates, 1, 256, "causal", sel_t=sel_t)
            o_win = _nsa_flash(qn, kn, 2, y, (qw + 5 * kw_) // kw_, gates, 2, 256, "window")
            x = _out_project([[o_cmp, o_sel, o_win]], p["nsa_w_out"], x, g1, 512, 512)
            buf = min(WINDOW, t)
            states["nsa_kc"] = _heads(kn[:, :, :kw_], NSA_GROUPS)
            states["nsa_vc"] = _heads(y[:, :, qw + kw_:qw + 2 * kw_], NSA_GROUPS)
            states["nsa_ks"] = _heads(kn[:, :, kw_:2 * kw_], NSA_GROUPS)
            states["nsa_vs"] = _heads(y[:, :, qw + 3 * kw_:qw + 4 * kw_], NSA_GROUPS)
            states["nsa_kw"] = _heads(kn[:, t - buf:, 2 * kw_:], NSA_GROUPS)
            states["nsa_vw"] = _heads(y[:, t - buf:, qw + 5 * kw_:qw + 6 * kw_], NSA_GROUPS)
        x = _ffn(x, p["norm_ffn"], sc2, sh2, g2, p["w_up"], p["w_down"], 512, 256)
    return x, states


def _sample_trunk(x, mod, params, past, page_table):
    bx, tnew, d = x.shape
    rows = bx * tnew
    n_pages = page_table.shape[1]
    pos0 = n_pages * PAGE
    assert tnew < CMP_STRIDE and pos0 % CMP_STRIDE == 0 and pos0 >= WINDOW
    tiles = {"tm": rows, "tm_norm": rows}
    xf = x.reshape(1, rows, d)
    per_batch = lambda a: a.reshape(bx, tnew, a.shape[-1])
    states = {}
    for layer in range(mod.shape[0]):
        p = _layer_params(params, layer)
        e = layer // 2
        sh1, sc1, g1, sh2, sc2, g2 = [jnp.repeat(m, tnew, axis=0)[None] for m in jnp.split(mod[layer], 6, axis=-1)]
        mods = {"sc1": sc1, "sh1": sh1}
        if layer % 2 == 0:
            y, f_t, qk = _mix_ab(xf, mods, pos0, None, p, tiles)
            q_s, k_s = per_batch(qk[0, :, :FOX_WIDTH]), per_batch(qk[0, :, FOX_WIDTH:])
            v_s = per_batch(y[0, :, 2 * FOX_WIDTH:3 * FOX_WIDTH])
            u_s = per_batch(y[0, :, 3 * FOX_WIDTH:])
            f_new = f_t[0].reshape(FOX_HEADS, bx, tnew).transpose(1, 0, 2)
            cache_kt = past["cache_fox_k"][e].transpose(0, 2, 3, 1)
            cache_vt = past["cache_fox_v"][e].transpose(0, 2, 3, 1)
            cache_lft = past["cache_fox_logf"][e].transpose(0, 2, 1)
            o_fox, lf_new = _fox_sample(q_s, k_s, v_s, f_new, p["ab_b_fgate"], cache_kt, cache_vt, cache_lft,
                                        page_table)
            pool_prev = past["state_pool"][e]
            prefix = jnp.pad(pool_prev, ((0, 0), (1, 0), (0, 0)))
            o_pool = _pool_mixer(u_s, 0, prefix, p["ab_pool_map"], p["ab_pool_scale"], pos0, tnew)
            xf = _out_project([[o_fox.reshape(1, rows, -1)], [o_pool.reshape(1, rows, -1)]], p["ab_w_out"], xf, g1,
                              rows, 512)
            states["fox_k"] = _heads(k_s, FOX_HEADS)
            states["fox_v"] = _heads(v_s, FOX_HEADS)
            states["fox_logf"] = lf_new.transpose(0, 2, 1)
            states["pool"] = jnp.concatenate([pool_prev, u_s], axis=1)[:, -(POOL_HALO - 1):]
        else:
            pos_rows = pos0 + jnp.arange(rows) % tnew
            y, qn, kn, gates = _nsa_project(xf, mods, pos_rows, p, tiles)
            qw, kw_ = NSA_HEADS * HEAD_DIM, NSA_KV_WIDTH
            q_s, gates_s = per_batch(qn[0]), per_batch(gates[0])
            kn_s, y_s = per_batch(kn[0]), per_batch(y[0])
            kc_s, ks_s, kwn_s = kn_s[..., :kw_], kn_s[..., kw_:2 * kw_], kn_s[..., 2 * kw_:]
            vc_s, vs_s, vwn_s = (y_s[..., qw + kw_:qw + 2 * kw_], y_s[..., qw + 3 * kw_:qw + 4 * kw_],
                                 y_s[..., qw + 5 * kw_:qw + 6 * kw_])
            d_major = lambda a: a.transpose(0, 2, 3, 1)
            kcmp = _compress_pages(d_major(past["cache_nsa_kc"][e]), page_table, p["nsa_cmp_pos_k"],
                                   p["nsa_cmp_w1_k"], p["nsa_cmp_w2_k"])
            vcmp = _compress_pages(d_major(past["cache_nsa_vc"][e]), page_table, p["nsa_cmp_pos_v"],
                                   p["nsa_cmp_w1_v"], p["nsa_cmp_w2_v"])
            total = pos0 + tnew
            n_cmp = total // CMP_STRIDE - 1
            n_sel = -(-total // SEL_BLOCK)
            nselp = -(-n_sel // (2 * LANE)) * (2 * LANE)
            o_cmp, sel = _cmp_select(q_s, kcmp, vcmp, gates_s, n_cmp, n_sel, nselp, pos0, tnew)
            selrows = jnp.repeat(sel, NSA_REP, axis=1).reshape(bx, NSA_HEADS * tnew, nselp)
            o_sel = _sel_sample(q_s, ks_s, vs_s, selrows, gates_s, d_major(past["cache_nsa_ks"][e]),
                                d_major(past["cache_nsa_vs"][e]), page_table)
            kw_prev, vw_prev = past["state_nsa_kw"][e], past["state_nsa_vw"][e]
            o_win = _win_sample(q_s, kwn_s, vwn_s, gates_s, d_major(kw_prev), d_major(vw_prev))
            flat = lambda a: a.reshape(1, rows, -1)
            xf = _out_project([[flat(o_cmp), flat(o_sel), flat(o_win)]], p["nsa_w_out"], xf, g1, rows, 512)
            buf = kw_prev.shape[1]
            states["nsa_kc"] = _heads(kc_s, NSA_GROUPS)
            states["nsa_vc"] = _heads(vc_s, NSA_GROUPS)
            states["nsa_ks"] = _heads(ks_s, NSA_GROUPS)
            states["nsa_vs"] = _heads(vs_s, NSA_GROUPS)
            states["nsa_kw"] = jnp.concatenate([kw_prev, _heads(kwn_s, NSA_GROUPS)], axis=1)[:, -buf:]
            states["nsa_vw"] = jnp.concatenate([vw_prev, _heads(vwn_s, NSA_GROUPS)], axis=1)[:, -buf:]
        xf = _ffn(xf, p["norm_ffn"], sc2, sh2, g2, p["w_up"], p["w_down"], rows, 256)
    return xf.reshape(bx, tnew, d), states


_STATE_NAMES = ("fox_k", "fox_v", "fox_logf", "pool", "nsa_kc", "nsa_vc", "nsa_ks", "nsa_vs", "nsa_kw", "nsa_vw")


def kernel(x_prompt, x_sample, cache_fox_k, cache_fox_v, cache_fox_logf, state_pool, cache_nsa_kc, cache_nsa_vc,
           cache_nsa_ks, cache_nsa_vs, state_nsa_kw, state_nsa_vw, page_table, c_prompt, c_sample, w_mod, b_mod,
           norm_mix, norm_ffn, w_up, w_down, ab_w_in, ab_b_fgate, ab_q_norm, ab_k_norm, ab_pool_map, ab_pool_scale,
           ab_w_out, nsa_w_in, nsa_b_gate, nsa_q_norm, nsa_k_norm, nsa_cmp_pos_k, nsa_cmp_w1_k, nsa_cmp_w2_k,
           nsa_cmp_pos_v, nsa_cmp_w1_v, nsa_cmp_w2_v, nsa_w_out):
    assert w_mod.shape[0] == 2, "one forgetting/pooling layer followed by one sparse-attention layer"
    params = {
        "norm_mix": norm_mix, "norm_ffn": norm_ffn, "w_up": w_up, "w_down": w_down,
        "ab_w_in": ab_w_in, "ab_b_fgate": ab_b_fgate, "ab_q_norm": ab_q_norm, "ab_k_norm": ab_k_norm,
        "ab_pool_map": ab_pool_map, "ab_pool_scale": ab_pool_scale, "ab_w_out": ab_w_out,
        "nsa_w_in": nsa_w_in, "nsa_b_gate": nsa_b_gate, "nsa_q_norm": nsa_q_norm, "nsa_k_norm": nsa_k_norm,
        "nsa_cmp_pos_k": nsa_cmp_pos_k, "nsa_cmp_w1_k": nsa_cmp_w1_k, "nsa_cmp_w2_k": nsa_cmp_w2_k,
        "nsa_cmp_pos_v": nsa_cmp_pos_v, "nsa_cmp_w1_v": nsa_cmp_w1_v, "nsa_cmp_w2_v": nsa_cmp_w2_v,
        "nsa_w_out": nsa_w_out,
    }
    past = {
        "cache_fox_k": cache_fox_k, "cache_fox_v": cache_fox_v, "cache_fox_logf": cache_fox_logf,
        "state_pool": state_pool, "cache_nsa_kc": cache_nsa_kc, "cache_nsa_vc": cache_nsa_vc,
        "cache_nsa_ks": cache_nsa_ks, "cache_nsa_vs": cache_nsa_vs,
        "state_nsa_kw": state_nsa_kw, "state_nsa_vw": state_nsa_vw,
    }
    n_prompt = c_prompt.shape[0]
    mod = _modulation(jnp.concatenate([c_prompt, c_sample], axis=0), w_mod, b_mod)
    y_prompt, sp = _prompt_trunk(x_prompt, mod[:, :n_prompt], params)
    y_sample, ss = _sample_trunk(x_sample, mod[:, n_prompt:], params, past, page_table)
    return (y_prompt, y_sample, *[sp[n][None] for n in _STATE_NAMES], *[ss[n][None] for n in _STATE_NAMES])
```

```python
import functools

import jax
import jax.numpy as jnp
from jax import lax
from jax.experimental import pallas as pl
from jax.experimental.pallas import tpu as pltpu

F32 = jnp.float32
BF16 = jnp.bfloat16

D_MODEL = 1024
HEAD_DIM = 64
EPS = 1e-6
ROPE_THETA = 10000.0
NEG_INF = -1e30
MASKED = -2e30
BIG = 1e9
PAGE = 128
FOX_HEADS = 8
FOX_WIDTH = FOX_HEADS * HEAD_DIM
POOL_WINDOWS = (2, 4, 8, 16)
POOL_WIDTH = 512
POOL_GROUP_CH = 128
POOL_HALO = 16
NSA_HEADS = 16
NSA_GROUPS = 4
NSA_REP = NSA_HEADS // NSA_GROUPS
NSA_KV_WIDTH = NSA_GROUPS * HEAD_DIM
NSA_BRANCHES = 3
CMP_BLOCK = 32
CMP_STRIDE = 16
SEL_BLOCK = 64
SEL_TOPK = 16
WINDOW = 512
D_FF = 2816
LANE = 128
SUBLANE = 8
GROUPS_PER_TILE = LANE // HEAD_DIM
PAGES_PER_STEP = 8
VMEM_LIMIT = 48 * 1024 * 1024


def _params(*sem):
    return pltpu.CompilerParams(dimension_semantics=sem, vmem_limit_bytes=VMEM_LIMIT)


def _nt(a, b):
    return lax.dot_general(a, b, (((1,), (1,)), ((), ())), preferred_element_type=F32)


def _mm(a, b):
    return jnp.dot(a, b, preferred_element_type=F32)


def _mod_body(c_ref, w_ref, b_ref, o_ref):
    c = c_ref[...]
    a = (c * jax.nn.sigmoid(c)).astype(BF16)
    o_ref[0] = _mm(a, w_ref[0].astype(BF16)) + b_ref[0]


def _modulation(c_all, w_mod, b_mod):
    n_layers, d, n = w_mod.shape
    rows = c_all.shape[0]
    tn = 1536
    return pl.pallas_call(
        _mod_body,
        grid=(n_layers, n // tn),
        in_specs=[
            pl.BlockSpec((rows, d), lambda l, j: (0, 0)),
            pl.BlockSpec((1, d, tn), lambda l, j: (l, 0, j)),
            pl.BlockSpec((1, 1, tn), lambda l, j: (l, 0, j)),
        ],
        out_specs=pl.BlockSpec((1, rows, tn), lambda l, j: (l, 0, j)),
        out_shape=jax.ShapeDtypeStruct((n_layers, rows, n), F32),
        compiler_params=_params("arbitrary", "arbitrary"),
        name="modulation",
    )(c_all, w_mod, b_mod.reshape(n_layers, 1, n))


def _modulated_norm(x, gain, sc, sh):
    xn = x * lax.rsqrt(jnp.mean(x * x, axis=-1, keepdims=True) + EPS) * gain
    return xn * (1.0 + sc) + sh


def _proj_body(*refs, has_side, has_t):
    x_ref, g_ref, sc_ref, sh_ref, w_ref = refs[:5]
    k = 5
    side_ref = t_ref = None
    if has_side:
        side_ref = refs[k]; k += 1
    if has_t:
        t_ref = refs[k]; k += 1
    y_ref = refs[k]; k += 1
    hb = _modulated_norm(x_ref[0], g_ref[...], sc_ref[0], sh_ref[0]).astype(BF16)
    y_ref[0] = _mm(hb, w_ref[...])
    if has_side:
        refs[k][0] = _mm(hb, side_ref[...]); k += 1
    if has_t:
        refs[k][0] = _nt(t_ref[...], hb)


def _resident(shape):
    return pl.BlockSpec(shape, lambda *_: (0,) * len(shape), pipeline_mode=pl.Buffered(1))


def _project(x, gain, sc, sh, w, tm, w_side=None, w_t=None):
    bx, tx, d = x.shape
    n_cols = w.shape[1]
    r = sc.shape[1]
    mod_spec = (pl.BlockSpec((1, tm, d), lambda b, i: (b, i, 0)) if r == tx
                else pl.BlockSpec((1, 1, d), lambda b, i: (b, 0, 0)))
    in_specs = [pl.BlockSpec((1, tm, d), lambda b, i: (b, i, 0)), _resident((1, d)), mod_spec, mod_spec,
                _resident((d, n_cols))]
    args = [x, gain.reshape(1, d), sc, sh, w]
    out_specs = [pl.BlockSpec((1, tm, n_cols), lambda b, i: (b, i, 0))]
    out_shape = [jax.ShapeDtypeStruct((bx, tx, n_cols), F32)]
    if w_side is not None:
        ns = w_side.shape[1]
        in_specs.append(_resident((d, ns)))
        args.append(w_side)
        out_specs.append(pl.BlockSpec((1, tm, ns), lambda b, i: (b, i, 0)))
        out_shape.append(jax.ShapeDtypeStruct((bx, tx, ns), F32))
    if w_t is not None:
        nt = w_t.shape[0]
        in_specs.append(_resident((nt, d)))
        args.append(w_t)
        out_specs.append(pl.BlockSpec((1, nt, tm), lambda b, i: (b, 0, i)))
        out_shape.append(jax.ShapeDtypeStruct((bx, nt, tx), F32))
    return pl.pallas_call(
        functools.partial(_proj_body, has_side=w_side is not None, has_t=w_t is not None),
        grid=(bx, tx // tm),
        in_specs=in_specs, out_specs=out_specs, out_shape=out_shape,
        compiler_params=_params("arbitrary", "arbitrary"),
        name="project",
    )(*args)


def _headnorm_body(*refs, tiles, rope, has_gate, n_out):
    y_ref, gain_ref = refs[:2]
    k = 2
    cos_ref = sin_ref = gl_ref = bg_ref = None
    if rope:
        cos_ref, sin_ref = refs[k], refs[k + 1]; k += 2
    if has_gate:
        gl_ref, bg_ref = refs[k], refs[k + 1]; k += 2
    outs = refs[k:k + n_out]
    gate_out = refs[k + n_out] if has_gate else None
    lane = lax.broadcasted_iota(jnp.int32, (1, LANE), 1)
    low_head = lane < HEAD_DIM
    first_half = (lane % HEAD_DIM) < (HEAD_DIM // 2)
    for src, oi, dst in tiles:
        y = y_ref[0, :, src * LANE:(src + 1) * LANE]
        y2 = y * y
        s_lo = jnp.sum(jnp.where(low_head, y2, 0.0), axis=-1, keepdims=True)
        s_hi = jnp.sum(jnp.where(low_head, 0.0, y2), axis=-1, keepdims=True)
        ms = jnp.where(low_head, s_lo, s_hi) * (1.0 / HEAD_DIM)
        yn = y * lax.rsqrt(ms + EPS) * gain_ref[:, src * LANE:(src + 1) * LANE]
        if rope:
            partner = jnp.where(first_half, pltpu.roll(yn, LANE - HEAD_DIM // 2, 1), pltpu.roll(yn, HEAD_DIM // 2, 1))
            yn = yn * cos_ref[...] + partner * sin_ref[...]
        outs[oi][0, :, dst * LANE:(dst + 1) * LANE] = yn
    if has_gate:
        gate_out[0] = jax.nn.sigmoid(gl_ref[0] + bg_ref[...])


def _headnorm(y, width, gains, tiles, out_widths, tm, rope_tabs=None, gate_logits=None, gate_bias=None):
    bx, tx, _ = y.shape
    rope = rope_tabs is not None
    has_gate = gate_logits is not None
    in_specs = [pl.BlockSpec((1, tm, width), lambda b, i: (b, i, 0)),
                pl.BlockSpec((1, width), lambda b, i: (0, 0))]
    args = [y, gains]
    if rope:
        in_specs += [pl.BlockSpec((tm, LANE), lambda b, i: (i, 0))] * 2
        args += list(rope_tabs)
    if has_gate:
        in_specs += [pl.BlockSpec((1, tm, LANE), lambda b, i: (b, i, 0)), pl.BlockSpec((1, LANE), lambda b, i: (0, 0))]
        args += [gate_logits, gate_bias]
    out_specs = [pl.BlockSpec((1, tm, w), lambda b, i: (b, i, 0)) for w in out_widths]
    out_shape = [jax.ShapeDtypeStruct((bx, tx, w), F32) for w in out_widths]
    if has_gate:
        out_specs.append(pl.BlockSpec((1, tm, LANE), lambda b, i: (b, i, 0)))
        out_shape.append(jax.ShapeDtypeStruct((bx, tx, LANE), F32))
    return pl.pallas_call(
        functools.partial(_headnorm_body, tiles=tuple(tiles), rope=rope, has_gate=has_gate, n_out=len(out_widths)),
        grid=(bx, tx // tm),
        in_specs=in_specs, out_specs=out_specs, out_shape=out_shape,
        compiler_params=_params("arbitrary", "arbitrary"),
        name="headnorm",
    )(*args)


def _rope_tables(pos):
    half = HEAD_DIM // 2
    inv_freq = ROPE_THETA ** (-jnp.arange(half, dtype=F32) / half)
    ang = pos.astype(F32)[:, None] * inv_freq[None, :]
    cos, sin = jnp.cos(ang), jnp.sin(ang)
    reps = LANE // HEAD_DIM
    return (jnp.tile(jnp.concatenate([cos, cos], axis=1), (1, reps)),
            jnp.tile(jnp.concatenate([-sin, sin], axis=1), (1, reps)))


def _lane_cumsum(x):
    n = x.shape[-1]
    lane = lax.broadcasted_iota(jnp.int32, x.shape, x.ndim - 1)
    s = 1
    while s < n:
        x = x + jnp.where(lane >= s, pltpu.roll(x, s, x.ndim - 1), 0.0)
        s *= 2
    return x


def _logf_body(f_ref, b_ref, lf_ref, cum_ref):
    lf = jax.nn.log_sigmoid(f_ref[0] + b_ref[...])
    lf_ref[0] = lf
    cum_ref[0] = _lane_cumsum(lf)


def _logf_cumsum(f_t, b_fgate):
    bx, h, t = f_t.shape
    spec = pl.BlockSpec((1, h, t), lambda b: (b, 0, 0))
    return pl.pallas_call(
        _logf_body, grid=(bx,),
        in_specs=[spec, pl.BlockSpec((h, 1), lambda b: (0, 0))],
        out_specs=[spec, spec],
        out_shape=[jax.ShapeDtypeStruct(f_t.shape, F32)] * 2,
        compiler_params=_params("arbitrary"),
        name="logf_cumsum",
    )(f_t, b_fgate.reshape(h, 1))


def _half_mask(shape, half):
    lane = lax.broadcasted_iota(jnp.int32, shape, len(shape) - 1)
    return (lane % LANE) // HEAD_DIM == half


def _fox_flash_body(q_ref, k_ref, v_ref, cum_ref, o_ref, qm_scr, m_scr, l_scr, acc_scr, *, tq, nk, scale):
    qi, ki = pl.program_id(1), pl.program_id(2)
    n_pairs = FOX_HEADS // 2
    reps = tq // LANE

    @pl.when(ki == 0)
    def _():
        for h in range(FOX_HEADS):
            c = h // 2
            qpair = q_ref[0, :, c * LANE:(c + 1) * LANE] * scale
            qm_scr[h] = jnp.where(_half_mask(qpair.shape, h % 2), qpair, 0.0).astype(BF16)
        m_scr[...] = jnp.full(m_scr.shape, NEG_INF, F32)
        l_scr[...] = jnp.zeros(l_scr.shape, F32)
        acc_scr[...] = jnp.zeros(acc_scr.shape, F32)

    @pl.when(ki <= qi)
    def _():
        qpos = qi * tq + lax.broadcasted_iota(jnp.int32, (tq, tq), 0)
        kpos = ki * tq + lax.broadcasted_iota(jnp.int32, (tq, tq), 1)
        vis = kpos <= qpos
        for c in range(n_pairs):
            k_pair = k_ref[0, :, c * LANE:(c + 1) * LANE].astype(BF16)
            v_pair = v_ref[0, :, c * LANE:(c + 1) * LANE]
            pv = []
            alphas = []
            for half in range(2):
                h = 2 * c + half
                s = _nt(qm_scr[h], k_pair) - cum_ref[0, h:h + 1, :]
                s = jnp.where(vis, s, MASKED)
                m_prev = m_scr[h]
                m_new = jnp.maximum(m_prev, jnp.max(s, axis=1, keepdims=True))
                alpha = jnp.exp(m_prev - m_new)
                p = jnp.exp(s - jnp.concatenate([m_new] * reps, axis=1))
                l_scr[h] = alpha * l_scr[h] + jnp.sum(p, axis=1, keepdims=True)
                m_scr[h] = m_new
                v_half = jnp.where(_half_mask(v_pair.shape, half), v_pair, 0.0).astype(BF16)
                pv.append(_mm(p.astype(BF16), v_half))
                alphas.append(alpha)
            alpha_pair = jnp.where(_half_mask(alphas[0].shape, 0), alphas[0], alphas[1])
            acc_scr[c] = alpha_pair * acc_scr[c] + (pv[0] + pv[1])

    @pl.when(ki == nk - 1)
    def _():
        for c in range(n_pairs):
            l_pair = jnp.where(_half_mask((tq, LANE), 0), l_scr[2 * c], l_scr[2 * c + 1])
            o_ref[0, :, c * LANE:(c + 1) * LANE] = acc_scr[c] / l_pair


def _fox_flash(qk, v_arr, vcb, cum, tq):
    bx, t, _ = qk.shape
    w = FOX_WIDTH
    nq = t // tq
    kmap = lambda qi, ki: jnp.minimum(ki, qi)
    return pl.pallas_call(
        functools.partial(_fox_flash_body, tq=tq, nk=nq, scale=HEAD_DIM ** -0.5),
        grid=(bx, nq, nq),
        in_specs=[pl.BlockSpec((1, tq, w), lambda b, qi, ki: (b, qi, 0)),
                  pl.BlockSpec((1, tq, w), lambda b, qi, ki: (b, kmap(qi, ki), 1)),
                  pl.BlockSpec((1, tq, w), lambda b, qi, ki: (b, kmap(qi, ki), vcb)),
                  pl.BlockSpec((1, FOX_HEADS, tq), lambda b, qi, ki: (b, 0, kmap(qi, ki)))],
        out_specs=pl.BlockSpec((1, tq, w), lambda b, qi, ki: (b, qi, 0)),
        out_shape=jax.ShapeDtypeStruct((bx, t, w), F32),
        scratch_shapes=[pltpu.VMEM((FOX_HEADS, tq, LANE), BF16), pltpu.VMEM((FOX_HEADS, tq, LANE), F32),
                        pltpu.VMEM((FOX_HEADS, tq, LANE), F32), pltpu.VMEM((FOX_HEADS // 2, tq, LANE), F32)],
        compiler_params=_params("arbitrary", "arbitrary", "arbitrary"),
        name="fox_flash",
    )(qk, qk, v_arr, cum)


def _nsa_flash_body(*refs, tq, nk, mode, has_sel, gate_branch, scale):
    q_ref, k_ref, v_ref = refs[:3]
    i = 3
    sel_ref = None
    if has_sel:
        sel_ref = refs[i]; i += 1
    gate_ref, o_ref, qt_scr, m_scr, l_scr, acc_scr = refs[i:i + 6]
    qi, ki = pl.program_id(1), pl.program_id(2)
    kt = ki if mode == "causal" else qi - (nk - 1) + ki
    active = (ki <= qi) if mode == "causal" else (kt >= 0)

    @pl.when(ki == 0)
    def _():
        zeros = jnp.zeros((HEAD_DIM, tq), F32)
        for pair in range(NSA_HEADS // 2):
            q_pair_t = (q_ref[0, :, pair * LANE:(pair + 1) * LANE] * scale).T
            for half in range(2):
                g, r = divmod(2 * pair + half, NSA_REP)
                q_t = q_pair_t[half * HEAD_DIM:(half + 1) * HEAD_DIM]
                parts = [q_t, zeros] if g % GROUPS_PER_TILE == 0 else [zeros, q_t]
                qt_scr[g, :, r * tq:(r + 1) * tq] = jnp.concatenate(parts, axis=0).astype(BF16)
        m_scr[...] = jnp.full(m_scr.shape, NEG_INF, F32)
        l_scr[...] = jnp.zeros(l_scr.shape, F32)
        acc_scr[...] = jnp.zeros(acc_scr.shape, F32)

    @pl.when(active)
    def _():
        kpos = kt * tq + lax.broadcasted_iota(jnp.int32, (tq, tq), 0)
        qpos = qi * tq + lax.broadcasted_iota(jnp.int32, (tq, tq), 1)
        vis = kpos <= qpos
        if mode == "window":
            vis = vis & (kpos > qpos - WINDOW)
        if has_sel:
            nsel = sel_ref.shape[2]
            kblk = (kt * tq + lax.broadcasted_iota(jnp.int32, (tq, nsel), 0)) // SEL_BLOCK
            expand = (kblk == lax.broadcasted_iota(jnp.int32, (tq, nsel), 1)).astype(BF16)
        for c in range(NSA_GROUPS // GROUPS_PER_TILE):
            k_pair = k_ref[0, :, c * LANE:(c + 1) * LANE].astype(BF16)
            v_pair_t = v_ref[0, :, c * LANE:(c + 1) * LANE].T
            for gg in range(GROUPS_PER_TILE):
                g = c * GROUPS_PER_TILE + gg
                v_t = v_pair_t[gg * HEAD_DIM:(gg + 1) * HEAD_DIM].astype(BF16)
                vis_g = vis
                if has_sel:
                    vis_g = vis & (_mm(expand, sel_ref[0, g].astype(BF16)) > 0.5)
                vis_w = jnp.concatenate([vis_g] * NSA_REP, axis=1)
                s = jnp.where(vis_w, _mm(k_pair, qt_scr[g]), MASKED)
                m_prev = m_scr[g]
                m_new = jnp.maximum(m_prev, jnp.max(s, axis=0, keepdims=True))
                alpha = jnp.exp(m_prev - m_new)
                p = jnp.exp(s - m_new[0:1])
                l_scr[g] = alpha * l_scr[g] + jnp.sum(p, axis=0, keepdims=True)
                acc_scr[g] = alpha[0:1] * acc_scr[g] + _mm(v_t, p.astype(BF16))
                m_scr[g] = m_new

    @pl.when(ki == nk - 1)
    def _():
        low = _half_mask((tq, LANE), 0)
        for pair in range(NSA_HEADS // 2):
            parts, gate_cols = [], []
            for half in range(2):
                h = 2 * pair + half
                g, r = divmod(h, NSA_REP)
                lanes = slice(r * tq, (r + 1) * tq)
                parts.append(acc_scr[g, :, lanes] / l_scr[g, 0:1, lanes])
                c = h * NSA_BRANCHES + gate_branch
                gate_cols.append(gate_ref[0, :, c:c + 1])
            o_pair = jnp.concatenate(parts, axis=0).T
            o_ref[0, :, pair * LANE:(pair + 1) * LANE] = o_pair * jnp.where(low, gate_cols[0], gate_cols[1])


def _nsa_flash(qn, kn, kcb, y, vcb, gates, gate_branch, tq, mode, sel_t=None):
    bx, t, wq = qn.shape
    wk = NSA_KV_WIDTH
    nq = t // tq
    nk = nq if mode == "causal" else WINDOW // tq + 1
    if mode == "causal":
        kmap = lambda qi, ki: jnp.minimum(ki, qi)
    else:
        kmap = lambda qi, ki: jnp.maximum(qi - (nk - 1) + ki, 0)
    in_specs = [pl.BlockSpec((1, tq, wq), lambda b, qi, ki: (b, qi, 0)),
                pl.BlockSpec((1, tq, wk), lambda b, qi, ki: (b, kmap(qi, ki), kcb)),
                pl.BlockSpec((1, tq, wk), lambda b, qi, ki: (b, kmap(qi, ki), vcb))]
    args = [qn, kn, y]
    if sel_t is not None:
        in_specs.append(pl.BlockSpec((1, NSA_GROUPS, sel_t.shape[2], tq), lambda b, qi, ki: (b, 0, 0, qi)))
        args.append(sel_t)
    in_specs.append(pl.BlockSpec((1, tq, LANE), lambda b, qi, ki: (b, qi, 0)))
    args.append(gates)
    return pl.pallas_call(
        functools.partial(_nsa_flash_body, tq=tq, nk=nk, mode=mode, has_sel=sel_t is not None,
                          gate_branch=gate_branch, scale=HEAD_DIM ** -0.5),
        grid=(bx, nq, nk),
        in_specs=in_specs,
        out_specs=pl.BlockSpec((1, tq, wq), lambda b, qi, ki: (b, qi, 0)),
        out_shape=jax.ShapeDtypeStruct((bx, t, wq), F32),
        scratch_shapes=[pltpu.VMEM((NSA_GROUPS, LANE, NSA_REP * tq), BF16),
                        pltpu.VMEM((NSA_GROUPS, SUBLANE, NSA_REP * tq), F32),
                        pltpu.VMEM((NSA_GROUPS, SUBLANE, NSA_REP * tq), F32),
                        pltpu.VMEM((NSA_GROUPS, HEAD_DIM, NSA_REP * tq), F32)],
        compiler_params=_params("arbitrary", "arbitrary", "arbitrary"),
        name="nsa_flash_" + mode,
    )(*args)


def _pool_body(u_ref, pre_ref, map_ref, scale_ref, o_ref, ext_scr, *, tm, pos0):
    j = pl.program_id(1)

    @pl.when(j == 0)
    def _():
        ext_scr[0:POOL_HALO] = pre_ref[0]

    @pl.when(j > 0)
    def _():
        ext_scr[0:POOL_HALO] = ext_scr[tm:tm + POOL_HALO]

    ext_scr[POOL_HALO:POOL_HALO + tm] = u_ref[0]
    qpos = pos0 + j * tm + lax.broadcasted_iota(jnp.int32, (tm, 1), 0)
    for g, w in enumerate(POOL_WINDOWS):
        lo, hi = g * POOL_GROUP_CH, (g + 1) * POOL_GROUP_CH
        u_new = ext_scr[POOL_HALO:POOL_HALO + tm, lo:hi]
        tot = u_new
        for d in range(1, w):
            tot = tot + ext_scr[POOL_HALO - d:POOL_HALO - d + tm, lo:hi]
        count = jnp.minimum(w, qpos + 1).astype(F32)
        diff = tot / count - u_new
        y = _mm(diff.astype(BF16), map_ref[g].astype(BF16))
        o_ref[0, :, lo:hi] = y * scale_ref[:, lo:hi]


def _pool_mixer(u_arr, ucb, prefix, w_map, scale, pos0, tm):
    bx, t, _ = u_arr.shape
    c = POOL_WIDTH
    return pl.pallas_call(
        functools.partial(_pool_body, tm=tm, pos0=pos0),
        grid=(bx, t // tm),
        in_specs=[pl.BlockSpec((1, tm, c), lambda b, j: (b, j, ucb)),
                  pl.BlockSpec((1, POOL_HALO, c), lambda b, j: (b, 0, 0)),
                  pl.BlockSpec((len(POOL_WINDOWS), POOL_GROUP_CH, POOL_GROUP_CH), lambda b, j: (0, 0, 0)),
                  pl.BlockSpec((1, c), lambda b, j: (0, 0))],
        out_specs=pl.BlockSpec((1, tm, c), lambda b, j: (b, j, 0)),
        out_shape=jax.ShapeDtypeStruct((bx, t, c), F32),
        scratch_shapes=[pltpu.VMEM((POOL_HALO + tm, c), F32)],
        compiler_params=_params("arbitrary", "arbitrary"),
        name="pool_mixer",
    )(u_arr, prefix, w_map, scale.reshape(1, c))


def _outproj_body(*refs, group_sizes):
    n_a = sum(group_sizes)
    a_refs = refs[:n_a]
    w_refs = refs[n_a:n_a + len(group_sizes)]
    res_ref, gate_ref, o_ref = refs[n_a + len(group_sizes):]
    y = None
    k = 0
    for gi, n in enumerate(group_sizes):
        a = a_refs[k][0]
        for r in a_refs[k + 1:k + n]:
            a = a + r[0]
        k += n
        part = _mm(a.astype(BF16), w_refs[gi][...])
        y = part if y is None else y + part
    o_ref[0] = res_ref[0] + gate_ref[0] * y


def _out_project(groups, w, res, gate, tm):
    bx, tx, d = res.shape
    kg = groups[0][0].shape[-1]
    r = gate.shape[1]
    row = pl.BlockSpec((1, tm, d), lambda b, i: (b, i, 0))
    a_spec = pl.BlockSpec((1, tm, kg), lambda b, i: (b, i, 0))
    in_specs, args = [], []
    for grp in groups:
        for a in grp:
            in_specs.append(a_spec); args.append(a)
    for gi in range(len(groups)):
        in_specs.append(pl.BlockSpec((kg, d), lambda b, i, gi=gi: (gi, 0), pipeline_mode=pl.Buffered(1)))
        args.append(w)
    in_specs += [row, row if r == tx else pl.BlockSpec((1, 1, d), lambda b, i: (b, 0, 0))]
    args += [res, gate]
    return pl.pallas_call(
        functools.partial(_outproj_body, group_sizes=tuple(len(g) for g in groups)),
        grid=(bx, tx // tm),
        in_specs=in_specs,
        out_specs=row,
        out_shape=jax.ShapeDtypeStruct((bx, tx, d), F32),
        compiler_params=_params("arbitrary", "arbitrary"),
        name="out_project",
    )(*args)


FFN_SLICES = 2


def _ffn_body(x_ref, g_ref, sc_ref, sh_ref, gate_ref, wg_ref, wu_ref, wd_ref, o_ref):
    x = x_ref[0]
    h = _modulated_norm(x, g_ref[...], sc_ref[0], sh_ref[0]).astype(BF16)
    width = D_FF // FFN_SLICES
    y = None
    for c in range(FFN_SLICES):
        cols = slice(c * width, (c + 1) * width)
        gt = _mm(h, wg_ref[:, cols])
        up = _mm(h, wu_ref[:, cols])
        act = ((gt * jax.nn.sigmoid(gt)) * up).astype(BF16)
        part = _mm(act, wd_ref[cols, :])
        y = part if y is None else y + part
    o_ref[0] = x + gate_ref[0] * y


def _ffn(x, gain, sc, sh, gate, w_up, w_down, tm):
    bx, tx, d = x.shape
    r = sc.shape[1]
    mod_spec = (pl.BlockSpec((1, tm, d), lambda b, i: (b, i, 0)) if r == tx
                else pl.BlockSpec((1, 1, d), lambda b, i: (b, 0, 0)))
    x_spec = pl.BlockSpec((1, tm, d), lambda b, i: (b, i, 0))
    half = lambda j: pl.BlockSpec((d, D_FF), lambda b, i: (0, j), pipeline_mode=pl.Buffered(1))
    return pl.pallas_call(
        _ffn_body,
        grid=(bx, tx // tm),
        in_specs=[x_spec, _resident((1, d)), mod_spec, mod_spec, mod_spec, half(0), half(1), _resident((D_FF, d))],
        out_specs=x_spec,
        out_shape=jax.ShapeDtypeStruct(x.shape, F32),
        compiler_params=_params("arbitrary", "arbitrary"),
        name="swiglu",
    )(x, gain.reshape(1, d), sc, sh, gate, w_up, w_up, w_down)


def _chunk_rows_body(a_ref, o_ref, *, n_chunks):
    for s in range(CMP_STRIDE):
        rows = a_ref[0, pl.ds(s, n_chunks, stride=CMP_STRIDE), :]
        for g in range(GROUPS_PER_TILE):
            o_ref[0, g, :, s * HEAD_DIM:(s + 1) * HEAD_DIM] = rows[:, g * HEAD_DIM:(g + 1) * HEAD_DIM]


def _chunk_rows(a, acb, tm):
    bx, t, _ = a.shape
    n_chunks = tm // CMP_STRIDE
    tiles = NSA_KV_WIDTH // LANE
    return pl.pallas_call(
        functools.partial(_chunk_rows_body, n_chunks=n_chunks),
        grid=(bx, t // tm, tiles),
        in_specs=[pl.BlockSpec((1, tm, LANE), lambda b, i, c: (b, i, acb * tiles + c))],
        out_specs=pl.BlockSpec((1, GROUPS_PER_TILE, n_chunks, CMP_STRIDE * HEAD_DIM), lambda b, i, c: (b, c, i, 0)),
        out_shape=jax.ShapeDtypeStruct((bx, NSA_GROUPS, t // CMP_STRIDE, CMP_STRIDE * HEAD_DIM), F32),
        compiler_params=_params("arbitrary", "arbitrary", "arbitrary"),
        name="chunk_rows",
    )(a)


def _compress_mlp(a, pos_ref, w1_ref, w2_ref):
    half = CMP_STRIDE * HEAD_DIM
    w1 = w1_ref[...].astype(BF16)
    first = _mm(a, w1[:half])
    second = _mm(a, w1[half:])
    bias = _mm(pos_ref[...].astype(BF16), w1)
    hidden = (first + pltpu.roll(second, a.shape[0] - 1, 0)) + bias
    return _mm(jax.nn.gelu(hidden).astype(BF16), w2_ref[...].astype(BF16))


def _compress_body(a_ref, pos_ref, w1_ref, w2_ref, o_ref):
    o_ref[0, 0] = _compress_mlp(a_ref[0, 0].astype(BF16), pos_ref, w1_ref, w2_ref)


def _compress(chunks, pos_emb, w1, w2):
    bx, g, n_chunks, half = chunks.shape
    hidden = w1.shape[1]
    return pl.pallas_call(
        _compress_body,
        grid=(bx, g),
        in_specs=[pl.BlockSpec((1, 1, n_chunks, half), lambda b, i: (b, i, 0, 0)),
                  pl.BlockSpec((1, 2 * half), lambda b, i: (0, 0)),
                  pl.BlockSpec((2 * half, hidden), lambda b, i: (0, 0)),
                  pl.BlockSpec((hidden, HEAD_DIM), lambda b, i: (0, 0))],
        out_specs=pl.BlockSpec((1, 1, n_chunks, HEAD_DIM), lambda b, i: (b, i, 0, 0)),
        out_shape=jax.ShapeDtypeStruct((bx, g, n_chunks, HEAD_DIM), F32),
        compiler_params=_params("arbitrary", "arbitrary"),
        name="compress",
    )(chunks, pos_emb.reshape(1, 2 * half), w1, w2)


def _compress_pages_body(pt_ref, *rest, n_steps, pps):
    del pt_ref
    page_refs = rest[:pps]
    pos_ref, w1_ref, w2_ref, o_ref, tok_scr, chunk_scr = rest[pps:]
    p = pl.program_id(1)
    cpp = PAGE // CMP_STRIDE
    pairs = NSA_GROUPS // GROUPS_PER_TILE
    for i in range(pps):
        row0 = pl.multiple_of((p * pps + i) * cpp, cpp)
        for c in range(pairs):
            tok = tok_scr.at[i * pairs + c]
            tok[...] = page_refs[i][0, c * GROUPS_PER_TILE:(c + 1) * GROUPS_PER_TILE].reshape(LANE, PAGE).T
            for s in range(CMP_STRIDE):
                rows = tok[pl.ds(s, cpp, stride=CMP_STRIDE), :]
                for g in range(GROUPS_PER_TILE):
                    chunk_scr[c * GROUPS_PER_TILE + g, pl.ds(row0, cpp), s * HEAD_DIM:(s + 1) * HEAD_DIM] = (
                        rows[:, g * HEAD_DIM:(g + 1) * HEAD_DIM])

    @pl.when(p == n_steps - 1)
    def _():
        for g in range(NSA_GROUPS):
            o_ref[0, :, g * HEAD_DIM:(g + 1) * HEAD_DIM] = _compress_mlp(
                chunk_scr[g].astype(BF16), pos_ref, w1_ref, w2_ref)


def _compress_pages(cache_t, page_table, pos_emb, w1, w2):
    bx, n_pages = page_table.shape
    pps = _pages_per_step(n_pages)
    cpp = PAGE // CMP_STRIDE
    n_chunks = n_pages * cpp
    half = CMP_STRIDE * HEAD_DIM
    hidden = w1.shape[1]
    const = lambda b, p, pt: (0, 0)
    return pl.pallas_call(
        functools.partial(_compress_pages_body, n_steps=n_pages // pps, pps=pps),
        grid_spec=pltpu.PrefetchScalarGridSpec(
            num_scalar_prefetch=1, grid=(bx, n_pages // pps),
            in_specs=[pl.BlockSpec((1, NSA_GROUPS, HEAD_DIM, PAGE), _page_map(n_pages, pps, i, 4)) for i in range(pps)]
            + [pl.BlockSpec((1, 2 * half), const), pl.BlockSpec((2 * half, hidden), const),
               pl.BlockSpec((hidden, HEAD_DIM), const)],
            out_specs=pl.BlockSpec((1, n_chunks, NSA_KV_WIDTH), lambda b, p, pt: (b, 0, 0)),
            scratch_shapes=[pltpu.VMEM((pps * NSA_GROUPS // GROUPS_PER_TILE, PAGE, LANE), F32),
                            pltpu.VMEM((NSA_GROUPS, n_chunks, half), F32)]),
        out_shape=jax.ShapeDtypeStruct((bx, n_chunks, NSA_KV_WIDTH), F32),
        compiler_params=_params("arbitrary", "arbitrary"),
        name="compress_pages",
    )(page_table.reshape(-1), *([cache_t] * pps), pos_emb.reshape(1, 2 * half), w1, w2)


def _bf16_terms(x):
    hi = x.astype(BF16)
    r1 = x - hi.astype(F32)
    mid = r1.astype(BF16)
    lo = (r1 - mid.astype(F32)).astype(BF16)
    return hi, mid, lo


def _cmp_select_body(q_ref, kc_ref, vc_ref, gate_ref, o_ref, sel_ref, qbd_scr, *, tq, n_cmp, n_sel, pos0, scale):
    ncp = kc_ref.shape[1]
    nselp = sel_ref.shape[-1]
    gt = NSA_GROUPS * tq
    rows = NSA_REP * gt
    qbd_scr[...] = jnp.zeros(qbd_scr.shape, F32)
    for h in range(NSA_HEADS):
        g, r = divmod(h, NSA_REP)
        qbd_scr[r * gt + g * tq:r * gt + (g + 1) * tq, g * HEAD_DIM:(g + 1) * HEAD_DIM] = (
            q_ref[0, :, h * HEAD_DIM:(h + 1) * HEAD_DIM] * scale)
    row_pos = pos0 + lax.broadcasted_iota(jnp.int32, (rows, 1), 0) % tq
    c_idx = lax.broadcasted_iota(jnp.int32, (1, ncp), 1)
    c_valid = (c_idx * CMP_STRIDE + (CMP_BLOCK - 1) <= row_pos) & (c_idx < n_cmp)
    s = jnp.where(c_valid, _nt(qbd_scr[...].astype(BF16), kc_ref[0].astype(BF16)), NEG_INF)
    m = jnp.max(s, axis=-1, keepdims=True)
    p = jnp.where(c_valid, jnp.exp(s - m), 0.0)
    l = jnp.sum(p, axis=-1, keepdims=True)
    pc = jnp.where(l > 0.0, p / jnp.where(l > 0.0, l, 1.0), 0.0)
    o = _mm(pc.astype(BF16), vc_ref[0].astype(BF16))
    for h in range(NSA_HEADS):
        g, r = divmod(h, NSA_REP)
        c = h * NSA_BRANCHES
        o_ref[0, :, h * HEAD_DIM:(h + 1) * HEAD_DIM] = (
            o[r * gt + g * tq:r * gt + (g + 1) * tq, g * HEAD_DIM:(g + 1) * HEAD_DIM] * gate_ref[0, :, c:c + 1])
    pc_sum = pc[0:gt]
    for r in range(1, NSA_REP):
        pc_sum = pc_sum + pc[r * gt:(r + 1) * gt]
    cj = lax.broadcasted_iota(jnp.int32, (ncp, nselp), 0) * CMP_STRIDE
    sj = lax.broadcasted_iota(jnp.int32, (ncp, nselp), 1) * SEL_BLOCK
    overlap = ((cj < sj + SEL_BLOCK) & (cj + (CMP_BLOCK - 1) >= sj)).astype(BF16)
    terms = _mm(jnp.concatenate(_bf16_terms(pc_sum), axis=0), overlap)
    imp = (terms[0:gt] + terms[gt:2 * gt]) + terms[2 * gt:3 * gt]
    gpos = pos0 + lax.broadcasted_iota(jnp.int32, (gt, 1), 0) % tq
    j_idx = lax.broadcasted_iota(jnp.int32, (1, nselp), 1)
    forced = (j_idx == 0) | (j_idx == gpos // SEL_BLOCK)
    valid = j_idx * SEL_BLOCK <= gpos
    score = jnp.where(valid, jnp.where(forced, BIG, imp), -BIG)
    rank = jnp.zeros(score.shape, jnp.int32)
    for i in range(n_sel):
        si = score[:, i:i + 1]
        ahead = (si > score) | ((si == score) & (i < j_idx))
        rank = rank + ahead.astype(jnp.int32)
    sel_ref[0] = ((rank < SEL_TOPK) & valid).astype(F32)


def _cmp_select(qn, kcmp, vcmp, gates, n_cmp, n_sel, nselp, pos0):
    bx, t, wq = qn.shape
    ncp = kcmp.shape[1]
    assert (NSA_GROUPS * t) % (2 * SUBLANE) == 0
    whole = lambda b: (b, 0, 0)
    cmp_spec = pl.BlockSpec((1, ncp, NSA_KV_WIDTH), whole)
    return pl.pallas_call(
        functools.partial(_cmp_select_body, tq=t, n_cmp=n_cmp, n_sel=n_sel, pos0=pos0, scale=HEAD_DIM ** -0.5),
        grid=(bx,),
        in_specs=[pl.BlockSpec((1, t, wq), whole), cmp_spec, cmp_spec, pl.BlockSpec((1, t, LANE), whole)],
        out_specs=[pl.BlockSpec((1, t, wq), whole), pl.BlockSpec((1, NSA_GROUPS * t, nselp), whole)],
        out_shape=[jax.ShapeDtypeStruct((bx, t, wq), F32),
                   jax.ShapeDtypeStruct((bx, NSA_GROUPS * t, nselp), F32)],
        scratch_shapes=[pltpu.VMEM((NSA_HEADS * t, NSA_KV_WIDTH), F32)],
        compiler_params=_params("arbitrary"),
        name="cmp_select",
    )(qn, kcmp, vcmp, gates)


def _cmp_select_t_body(q_ref, kc_ref, vc_ref, gate_ref, o_ref, sel_ref, *, tq, n_cmp, n_sel, pos0, scale):
    g, qi = pl.program_id(1), pl.program_id(2)
    ncp = kc_ref.shape[2]
    nselp = sel_ref.shape[2]
    qpos = pos0 + qi * tq + lax.broadcasted_iota(jnp.int32, (1, tq), 1)
    c_idx = lax.broadcasted_iota(jnp.int32, (ncp, 1), 0)
    c_valid = (c_idx * CMP_STRIDE + (CMP_BLOCK - 1) <= qpos) & (c_idx < n_cmp)
    kc = kc_ref[0, 0].astype(BF16)
    vc_t = vc_ref[0, 0].T.astype(BF16)
    lane = lax.broadcasted_iota(jnp.int32, (1, LANE), 1)
    pc_sum = jnp.zeros((ncp, tq), F32)
    for r in range(NSA_REP):
        q_t = (q_ref[0, :, r * HEAD_DIM:(r + 1) * HEAD_DIM] * scale).T.astype(BF16)
        s = jnp.where(c_valid, _mm(kc, q_t), NEG_INF)
        m = jnp.max(s, axis=0, keepdims=True)
        p = jnp.where(c_valid, jnp.exp(s - m), 0.0)
        l = jnp.sum(p, axis=0, keepdims=True)
        pc = jnp.where(l > 0.0, p / jnp.where(l > 0.0, l, 1.0), 0.0)
        pc_sum = pc_sum + pc
        col = (g * NSA_REP + r) * NSA_BRANCHES
        gate = jnp.sum(jnp.where(lane == col, gate_ref[0], 0.0), axis=-1, keepdims=True)
        o_ref[0, :, r * HEAD_DIM:(r + 1) * HEAD_DIM] = _mm(vc_t, pc.astype(BF16)).T * gate
    sj = lax.broadcasted_iota(jnp.int32, (nselp, ncp), 0) * SEL_BLOCK
    cj = lax.broadcasted_iota(jnp.int32, (nselp, ncp), 1) * CMP_STRIDE
    overlap_t = ((cj < sj + SEL_BLOCK) & (cj + (CMP_BLOCK - 1) >= sj)).astype(F32)
    imp = jnp.dot(overlap_t, pc_sum, preferred_element_type=F32, precision=lax.Precision.HIGHEST)
    j_idx = lax.broadcasted_iota(jnp.int32, (nselp, 1), 0)
    forced = (j_idx == 0) | (j_idx == qpos // SEL_BLOCK)
    valid = j_idx * SEL_BLOCK <= qpos
    score = jnp.where(valid, jnp.where(forced, BIG, imp), -BIG)
    rank = jnp.zeros((nselp, tq), jnp.int32)
    for i in range(n_sel):
        si = score[i:i + 1, :]
        ahead = (si > score) | ((si == score) & (i < j_idx))
        rank = rank + ahead.astype(jnp.int32)
    sel_ref[0, 0] = ((rank < SEL_TOPK) & valid).astype(F32)


def _cmp_select_t(qn, kcmp, vcmp, gates, n_cmp, n_sel, pos0, tq):
    bx, t, _ = qn.shape
    ncp = kcmp.shape[2]
    gw = NSA_REP * HEAD_DIM
    nselp = -(-n_sel // SUBLANE) * SUBLANE
    return pl.pallas_call(
        functools.partial(_cmp_select_t_body, tq=tq, n_cmp=n_cmp, n_sel=n_sel, pos0=pos0, scale=HEAD_DIM ** -0.5),
        grid=(bx, NSA_GROUPS, t // tq),
        in_specs=[pl.BlockSpec((1, tq, gw), lambda b, g, i: (b, i, g)),
                  pl.BlockSpec((1, 1, ncp, HEAD_DIM), lambda b, g, i: (b, g, 0, 0)),
                  pl.BlockSpec((1, 1, ncp, HEAD_DIM), lambda b, g, i: (b, g, 0, 0)),
                  pl.BlockSpec((1, tq, LANE), lambda b, g, i: (b, i, 0))],
        out_specs=[pl.BlockSpec((1, tq, gw), lambda b, g, i: (b, i, g)),
                   pl.BlockSpec((1, 1, nselp, tq), lambda b, g, i: (b, g, 0, i))],
        out_shape=[jax.ShapeDtypeStruct((bx, t, NSA_HEADS * HEAD_DIM), F32),
                   jax.ShapeDtypeStruct((bx, NSA_GROUPS, nselp, t), F32)],
        compiler_params=_params("arbitrary", "arbitrary", "arbitrary"),
        name="cmp_select_t",
    )(qn, kcmp, vcmp, gates)


def _pages_per_step(n_pages):
    pps = min(PAGES_PER_STEP, n_pages)
    assert n_pages % pps == 0
    return pps


def _page_map(n_pages, pps, i, rank):
    return lambda b, p, pt: (pt[b * n_pages + p * pps + i],) + (0,) * (rank - 1)


def _softmax_step(s, vis, m_scr, l_scr, acc_scr, pv_fn):
    m_prev = m_scr[...]
    m_new = jnp.maximum(m_prev, jnp.max(s, axis=-1, keepdims=True))
    alpha = jnp.exp(m_prev - m_new)
    p = jnp.exp(s - m_new)
    if vis is not None:
        p = jnp.where(vis, p, 0.0)
    l_scr[...] = alpha * l_scr[...] + jnp.sum(p, axis=-1, keepdims=True)
    acc_scr[...] = alpha * acc_scr[...] + pv_fn(p)
    m_scr[...] = m_new


def _new_rows_step(qbd, k_new, v_new, row_t, bias_cols, m_scr, l_scr, acc_scr, tnew):
    cols = []
    for j in range(tnew):
        sj = jnp.sum(qbd * k_new[j:j + 1, :], axis=-1, keepdims=True)
        if bias_cols is not None:
            sj = sj - bias_cols[j]
        cols.append(jnp.where(row_t >= j, sj, NEG_INF))
    m_prev = m_scr[...]
    m_new = m_prev
    for sj in cols:
        m_new = jnp.maximum(m_new, sj)
    alpha = jnp.exp(m_prev - m_new)
    l = alpha * l_scr[...]
    acc = alpha * acc_scr[...]
    for j, sj in enumerate(cols):
        pj = jnp.where(row_t >= j, jnp.exp(sj - m_new), 0.0)
        l = l + pj
        acc = acc + pj * v_new[j:j + 1, :]
    return acc / l


def _init_softmax(m_scr, l_scr, acc_scr):
    m_scr[...] = jnp.full(m_scr.shape, NEG_INF, F32)
    l_scr[...] = jnp.zeros(l_scr.shape, F32)
    acc_scr[...] = jnp.zeros(acc_scr.shape, F32)


def _fox_sample_body(pt_ref, q_ref, kn_ref, vn_ref, fn_ref, bf_ref, *rest, n_steps, pps, tnew, scale):
    del pt_ref
    kt_refs, vt_refs, lf_refs = rest[:pps], rest[pps:2 * pps], rest[2 * pps:3 * pps]
    o_ref, lfo_ref, qbd_scr, m_scr, l_scr, acc_scr, carry_scr = rest[3 * pps:]
    p = pl.program_id(1)
    nh, w = FOX_HEADS, FOX_WIDTH
    rows = tnew * nh
    head_of_lane = lax.broadcasted_iota(jnp.int32, (nh, w), 1) // HEAD_DIM
    hmask = (head_of_lane == lax.broadcasted_iota(jnp.int32, (nh, w), 0)).astype(F32)

    @pl.when(p == 0)
    def _():
        for t in range(tnew):
            qbd_scr[t * nh:(t + 1) * nh, :] = q_ref[0, t:t + 1, :] * scale * hmask
        _init_softmax(m_scr, l_scr, acc_scr)
        carry_scr[...] = jnp.zeros(carry_scr.shape, F32)

    tri = (lax.broadcasted_iota(jnp.int32, (PAGE, PAGE), 0) <= lax.broadcasted_iota(jnp.int32, (PAGE, PAGE), 1)
           ).astype(F32)
    carry = carry_scr[...]
    cums = []
    for i in range(pps):
        within = jnp.dot(lf_refs[i][0], tri, preferred_element_type=F32, precision=lax.Precision.HIGHEST)
        cums.append(carry + within)
        carry = carry + within[:, PAGE - 1:PAGE]
    carry_scr[...] = carry
    bias = jnp.concatenate([jnp.concatenate(cums, axis=1)] * tnew, axis=0)
    kt = jnp.concatenate([r[0].reshape(w, PAGE).astype(BF16) for r in kt_refs], axis=1)
    vt = jnp.concatenate([r[0].reshape(w, PAGE).astype(BF16) for r in vt_refs], axis=1)
    s = _mm(qbd_scr[...].astype(BF16), kt) - bias
    _softmax_step(s, None, m_scr, l_scr, acc_scr, lambda pr: _nt(pr.astype(BF16), vt))

    @pl.when(p == n_steps - 1)
    def _():
        lf_new = jax.nn.log_sigmoid(fn_ref[0] + bf_ref[...])
        lfo_ref[0] = lf_new
        run = carry_scr[...]
        bias_cols = []
        for j in range(tnew):
            run = run + lf_new[:, j:j + 1]
            bias_cols.append(jnp.concatenate([run] * tnew, axis=0))
        row_t = lax.broadcasted_iota(jnp.int32, (rows, 1), 0) // nh
        o = _new_rows_step(qbd_scr[...], kn_ref[0], vn_ref[0], row_t, bias_cols, m_scr, l_scr, acc_scr, tnew)
        o = o * jnp.concatenate([hmask] * tnew, axis=0)
        for t in range(tnew):
            o_ref[0, t:t + 1, :] = jnp.sum(o[t * nh:(t + 1) * nh], axis=0, keepdims=True)


def _fox_sample(q, k_new, v_new, f_new, b_fgate, cache_kt, cache_vt, cache_lft, page_table):
    bx, tnew, w = q.shape
    n_pages = page_table.shape[1]
    pps = _pages_per_step(n_pages)
    nh = FOX_HEADS
    rows = tnew * nh
    new_spec = pl.BlockSpec((1, tnew, w), lambda b, p, pt: (b, 0, 0))
    kv_specs = [pl.BlockSpec((1, nh, HEAD_DIM, PAGE), _page_map(n_pages, pps, i, 4)) for i in range(pps)]
    lf_specs = [pl.BlockSpec((1, nh, PAGE), _page_map(n_pages, pps, i, 3)) for i in range(pps)]
    return pl.pallas_call(
        functools.partial(_fox_sample_body, n_steps=n_pages // pps, pps=pps, tnew=tnew, scale=HEAD_DIM ** -0.5),
        grid_spec=pltpu.PrefetchScalarGridSpec(
            num_scalar_prefetch=1, grid=(bx, n_pages // pps),
            in_specs=[new_spec, new_spec, new_spec,
                      pl.BlockSpec((1, nh, tnew), lambda b, p, pt: (b, 0, 0)),
                      pl.BlockSpec((nh, 1), lambda b, p, pt: (0, 0))] + kv_specs + kv_specs + lf_specs,
            out_specs=[new_spec, pl.BlockSpec((1, nh, tnew), lambda b, p, pt: (b, 0, 0))],
            scratch_shapes=[pltpu.VMEM((rows, w), F32), pltpu.VMEM((rows, 1), F32), pltpu.VMEM((rows, 1), F32),
                            pltpu.VMEM((rows, w), F32), pltpu.VMEM((nh, 1), F32)]),
        out_shape=[jax.ShapeDtypeStruct((bx, tnew, w), F32), jax.ShapeDtypeStruct((bx, nh, tnew), F32)],
        compiler_params=_params("arbitrary", "arbitrary"),
        name="fox_sample",
    )(page_table.reshape(-1), q, k_new, v_new, f_new, b_fgate.reshape(nh, 1),
      *([cache_kt] * pps), *([cache_vt] * pps), *([cache_lft] * pps))


def _fill_group_queries(qbd_scr, q_ref, tnew, scale):
    qbd_scr[...] = jnp.zeros(qbd_scr.shape, F32)
    for h in range(NSA_HEADS):
        g = h // NSA_REP
        qbd_scr[h * tnew:(h + 1) * tnew, g * HEAD_DIM:(g + 1) * HEAD_DIM] = (
            q_ref[0, :, h * HEAD_DIM:(h + 1) * HEAD_DIM] * scale)


def _write_group_heads(o_ref, o, gate_ref, branch, tnew):
    for h in range(NSA_HEADS):
        g = h // NSA_REP
        c = h * NSA_BRANCHES + branch
        o_ref[0, :, h * HEAD_DIM:(h + 1) * HEAD_DIM] = (
            o[h * tnew:(h + 1) * tnew, g * HEAD_DIM:(g + 1) * HEAD_DIM] * gate_ref[0, :, c:c + 1])


def _sel_sample_body(pt_ref, q_ref, kn_ref, vn_ref, selrows_ref, gate_ref, *rest, n_steps, pps, tnew, scale):
    del pt_ref
    kt_refs, vt_refs = rest[:pps], rest[pps:2 * pps]
    o_ref, qbd_scr, m_scr, l_scr, acc_scr = rest[2 * pps:]
    p = pl.program_id(1)
    rows = NSA_HEADS * tnew
    nselp = selrows_ref.shape[-1]
    keys = pps * PAGE

    @pl.when(p == 0)
    def _():
        _fill_group_queries(qbd_scr, q_ref, tnew, scale)
        _init_softmax(m_scr, l_scr, acc_scr)

    blk_of_lane = (p * keys + lax.broadcasted_iota(jnp.int32, (nselp, keys), 1)) // SEL_BLOCK
    expand = (lax.broadcasted_iota(jnp.int32, (nselp, keys), 0) == blk_of_lane).astype(BF16)
    vis = _mm(selrows_ref[0].astype(BF16), expand) > 0.5
    kt = jnp.concatenate([r[0].reshape(NSA_KV_WIDTH, PAGE).astype(BF16) for r in kt_refs], axis=1)
    vt = jnp.concatenate([r[0].reshape(NSA_KV_WIDTH, PAGE).astype(BF16) for r in vt_refs], axis=1)
    s = jnp.where(vis, _mm(qbd_scr[...].astype(BF16), kt), NEG_INF)
    _softmax_step(s, vis, m_scr, l_scr, acc_scr, lambda pr: _nt(pr.astype(BF16), vt))

    @pl.when(p == n_steps - 1)
    def _():
        row_t = lax.broadcasted_iota(jnp.int32, (rows, 1), 0) % tnew
        o = _new_rows_step(qbd_scr[...], kn_ref[0], vn_ref[0], row_t, None, m_scr, l_scr, acc_scr, tnew)
        _write_group_heads(o_ref, o, gate_ref, 1, tnew)


def _sel_sample(q, k_new, v_new, selrows, gates, cache_kt, cache_vt, page_table):
    bx, tnew, wq = q.shape
    n_pages = page_table.shape[1]
    rows = NSA_HEADS * tnew
    wk = NSA_KV_WIDTH
    pps = _pages_per_step(n_pages)
    fixed = lambda b, p, pt: (b, 0, 0)
    kv_specs = [pl.BlockSpec((1, NSA_GROUPS, HEAD_DIM, PAGE), _page_map(n_pages, pps, i, 4)) for i in range(pps)]
    return pl.pallas_call(
        functools.partial(_sel_sample_body, n_steps=n_pages // pps, pps=pps, tnew=tnew, scale=HEAD_DIM ** -0.5),
        grid_spec=pltpu.PrefetchScalarGridSpec(
            num_scalar_prefetch=1, grid=(bx, n_pages // pps),
            in_specs=[pl.BlockSpec((1, tnew, wq), fixed), pl.BlockSpec((1, tnew, wk), fixed),
                      pl.BlockSpec((1, tnew, wk), fixed), pl.BlockSpec((1, rows, selrows.shape[-1]), fixed),
                      pl.BlockSpec((1, tnew, LANE), fixed)] + kv_specs + kv_specs,
            out_specs=pl.BlockSpec((1, tnew, wq), fixed),
            scratch_shapes=[pltpu.VMEM((rows, wk), F32), pltpu.VMEM((rows, 1), F32), pltpu.VMEM((rows, 1), F32),
                            pltpu.VMEM((rows, wk), F32)]),
        out_shape=jax.ShapeDtypeStruct((bx, tnew, wq), F32),
        compiler_params=_params("arbitrary", "arbitrary"),
        name="sel_sample",
    )(page_table.reshape(-1), q, k_new, v_new, selrows, gates, *([cache_kt] * pps), *([cache_vt] * pps))


def _win_sample_body(q_ref, kn_ref, vn_ref, gate_ref, kt_ref, vt_ref, o_ref, qbd_scr, m_scr, l_scr, acc_scr,
                     *, tnew, wbuf, scale):
    rows = NSA_HEADS * tnew
    _fill_group_queries(qbd_scr, q_ref, tnew, scale)
    _init_softmax(m_scr, l_scr, acc_scr)
    row_t = lax.broadcasted_iota(jnp.int32, (rows, 1), 0) % tnew
    vis = lax.broadcasted_iota(jnp.int32, (rows, wbuf), 1) > row_t + (wbuf - WINDOW)
    kt = kt_ref[0].reshape(NSA_KV_WIDTH, wbuf).astype(BF16)
    vt = vt_ref[0].reshape(NSA_KV_WIDTH, wbuf).astype(BF16)
    s = jnp.where(vis, _mm(qbd_scr[...].astype(BF16), kt), NEG_INF)
    _softmax_step(s, vis, m_scr, l_scr, acc_scr, lambda pr: _nt(pr.astype(BF16), vt))
    o = _new_rows_step(qbd_scr[...], kn_ref[0], vn_ref[0], row_t, None, m_scr, l_scr, acc_scr, tnew)
    _write_group_heads(o_ref, o, gate_ref, 2, tnew)


def _win_sample(q, k_new, v_new, gates, buf_kt, buf_vt):
    bx, tnew, wq = q.shape
    wbuf = buf_kt.shape[-1]
    rows = NSA_HEADS * tnew
    wk = NSA_KV_WIDTH
    fixed = lambda b: (b, 0, 0)
    buf = pl.BlockSpec((1, NSA_GROUPS, HEAD_DIM, wbuf), lambda b: (b, 0, 0, 0))
    return pl.pallas_call(
        functools.partial(_win_sample_body, tnew=tnew, wbuf=wbuf, scale=HEAD_DIM ** -0.5),
        grid=(bx,),
        in_specs=[pl.BlockSpec((1, tnew, wq), fixed), pl.BlockSpec((1, tnew, wk), fixed),
                  pl.BlockSpec((1, tnew, wk), fixed), pl.BlockSpec((1, tnew, LANE), fixed), buf, buf],
        out_specs=pl.BlockSpec((1, tnew, wq), fixed),
        out_shape=jax.ShapeDtypeStruct((bx, tnew, wq), F32),
        scratch_shapes=[pltpu.VMEM((rows, wk), F32), pltpu.VMEM((rows, 1), F32), pltpu.VMEM((rows, 1), F32),
                        pltpu.VMEM((rows, wk), F32)],
        compiler_params=_params("arbitrary"),
        name="win_sample",
    )(q, k_new, v_new, gates, buf_kt, buf_vt)


def _tile_heads(v, n):
    return jnp.tile(v.astype(F32), n)


def _mix_ab(x, mods, pos0, past, p, tiles):
    y, f_t = _project(x, p["norm_mix"], mods["sc1"], mods["sh1"], p["w_in_main"], tiles["tm"], w_t=p["w_in_f_t"])
    gains = jnp.concatenate([_tile_heads(p["ab_q_norm"], FOX_HEADS), _tile_heads(p["ab_k_norm"], FOX_HEADS)])
    n_tiles = 2 * FOX_WIDTH // LANE
    (qk,) = _headnorm(y, 2 * FOX_WIDTH, gains.reshape(1, -1), [(c, 0, c) for c in range(n_tiles)],
                      [2 * FOX_WIDTH], tiles["tm_norm"])
    return y, f_t, qk


def _nsa_project(x, mods, pos_rows, p, tiles):
    n_gate = NSA_BRANCHES * NSA_HEADS
    y, gl = _project(x, p["norm_mix"], mods["sc1"], mods["sh1"], p["w_in_main"], tiles["tm"], w_side=p["w_in_gate"])
    kn3 = p["nsa_k_norm"]
    qw = NSA_HEADS * HEAD_DIM
    gains = jnp.concatenate([
        _tile_heads(p["nsa_q_norm"], NSA_HEADS),
        _tile_heads(kn3[0], NSA_GROUPS), jnp.ones((NSA_KV_WIDTH,), F32),
        _tile_heads(kn3[1], NSA_GROUPS), jnp.ones((NSA_KV_WIDTH,), F32),
        _tile_heads(kn3[2], NSA_GROUPS)])
    width = gains.shape[0]
    qt = qw // LANE
    kt = NSA_KV_WIDTH // LANE
    tile_map = [(c, 0, c) for c in range(qt)]
    for i in range(NSA_BRANCHES):
        tile_map += [(qt + 2 * i * kt + c, 1, i * kt + c) for c in range(kt)]
    b_gate = jnp.pad(p["nsa_b_gate"], (0, LANE - n_gate)).reshape(1, LANE)
    qn, kn, gates = _headnorm(y, width, gains.reshape(1, -1), tile_map, [qw, NSA_BRANCHES * NSA_KV_WIDTH],
                              tiles["tm_norm"], rope_tabs=_rope_tables(pos_rows), gate_logits=gl, gate_bias=b_gate)
    return y, qn, kn, gates


def _layer_params(params, layer):
    e = layer // 2
    p = {"norm_mix": params["norm_mix"][layer], "norm_ffn": params["norm_ffn"][layer],
         "w_up": params["w_up"][layer].astype(BF16), "w_down": params["w_down"][layer].astype(BF16)}
    prefix = "ab_" if layer % 2 == 0 else "nsa_"
    for k, v in params.items():
        if k.startswith(prefix):
            p[k] = v[e]
    if layer % 2 == 0:
        w_in = p["ab_w_in"]
        split_f = 3 * FOX_WIDTH
        p["w_in_main"] = jnp.concatenate([w_in[:, :split_f], w_in[:, split_f + FOX_HEADS:]], axis=1).astype(BF16)
        p["w_in_f_t"] = w_in[:, split_f:split_f + FOX_HEADS].T.astype(BF16)
        p["w_out"] = p["ab_w_out"].astype(BF16)
    else:
        w_in = p["nsa_w_in"]
        n_main = NSA_HEADS * HEAD_DIM + 6 * NSA_KV_WIDTH
        n_gate = NSA_BRANCHES * NSA_HEADS
        p["w_in_main"] = w_in[:, :n_main].astype(BF16)
        p["w_in_gate"] = jnp.pad(w_in[:, n_main:], ((0, 0), (0, LANE - n_gate))).astype(BF16)
        p["w_out"] = p["nsa_w_out"].astype(BF16)
    return p


def _heads(a, n):
    return a.reshape(a.shape[0], a.shape[1], n, HEAD_DIM)


def _prompt_trunk(x, mod, layers):
    bx, t, d = x.shape
    tiles = {"tm": 512, "tm_norm": 256}
    states = {}
    for layer, p in enumerate(layers):
        sh1, sc1, g1, sh2, sc2, g2 = [m[:, None, :] for m in jnp.split(mod[layer], 6, axis=-1)]
        mods = {"sc1": sc1, "sh1": sh1}
        if layer % 2 == 0:
            y, f_t, qk = _mix_ab(x, mods, 0, None, p, tiles)
            lf_t, cum = _logf_cumsum(f_t, p["ab_b_fgate"])
            o_fox = _fox_flash(qk, y, 2, cum, 512)
            o_pool = _pool_mixer(y, 3, jnp.zeros((bx, POOL_HALO, POOL_WIDTH), F32), p["ab_pool_map"],
                                 p["ab_pool_scale"], 0, 512)
            x = _out_project([[o_fox], [o_pool]], p["w_out"], x, g1, tiles["tm"])
            states["fox_k"] = _heads(qk[:, :, FOX_WIDTH:], FOX_HEADS)
            states["fox_v"] = _heads(y[:, :, 2 * FOX_WIDTH:3 * FOX_WIDTH], FOX_HEADS)
            states["fox_logf"] = lf_t.transpose(0, 2, 1)
            states["pool"] = y[:, t - (POOL_HALO - 1):, 3 * FOX_WIDTH:]
        else:
            y, qn, kn, gates = _nsa_project(x, mods, jnp.arange(t), p, tiles)
            qw, kw_ = NSA_HEADS * HEAD_DIM, NSA_KV_WIDTH
            n_chunk = t // CMP_STRIDE
            kcmp = _compress(_chunk_rows(kn, 0, t), p["nsa_cmp_pos_k"], p["nsa_cmp_w1_k"], p["nsa_cmp_w2_k"])
            vcmp = _compress(_chunk_rows(y, (qw + kw_) // kw_, t), p["nsa_cmp_pos_v"], p["nsa_cmp_w1_v"],
                             p["nsa_cmp_w2_v"])
            n_sel = -(-t // SEL_BLOCK)
            o_cmp, sel_t = _cmp_select_t(qn, kcmp, vcmp, gates, n_chunk - 1, n_sel, 0, 256)
            o_sel = _nsa_flash(qn, kn, 1, y, (qw + 3 * kw_) // kw_, gates, 1, 256, "causal", sel_t=sel_t)
            o_win = _nsa_flash(qn, kn, 2, y, (qw + 5 * kw_) // kw_, gates, 2, 256, "window")
            x = _out_project([[o_cmp, o_sel, o_win]], p["w_out"], x, g1, tiles["tm"])
            buf = min(WINDOW, t)
            states["nsa_kc"] = _heads(kn[:, :, :kw_], NSA_GROUPS)
            states["nsa_vc"] = _heads(y[:, :, qw + kw_:qw + 2 * kw_], NSA_GROUPS)
            states["nsa_ks"] = _heads(kn[:, :, kw_:2 * kw_], NSA_GROUPS)
            states["nsa_vs"] = _heads(y[:, :, qw + 3 * kw_:qw + 4 * kw_], NSA_GROUPS)
            states["nsa_kw"] = _heads(kn[:, t - buf:, 2 * kw_:], NSA_GROUPS)
            states["nsa_vw"] = _heads(y[:, t - buf:, qw + 5 * kw_:qw + 6 * kw_], NSA_GROUPS)
        x = _ffn(x, p["norm_ffn"], sc2, sh2, g2, p["w_up"], p["w_down"], tiles["tm"])
    return x, states


def _sample_trunk(x, mod, layers, past, page_table):
    bx, tnew, d = x.shape
    rows = bx * tnew
    n_pages = page_table.shape[1]
    pos0 = n_pages * PAGE
    assert tnew < CMP_STRIDE and pos0 % CMP_STRIDE == 0 and pos0 >= WINDOW
    tiles = {"tm": rows, "tm_norm": rows}
    xf = x.reshape(1, rows, d)
    per_batch = lambda a: a.reshape(bx, tnew, a.shape[-1])
    states = {}
    for layer, p in enumerate(layers):
        e = layer // 2
        sh1, sc1, g1, sh2, sc2, g2 = [jnp.repeat(m, tnew, axis=0)[None] for m in jnp.split(mod[layer], 6, axis=-1)]
        mods = {"sc1": sc1, "sh1": sh1}
        if layer % 2 == 0:
            y, f_t, qk = _mix_ab(xf, mods, pos0, None, p, tiles)
            q_s, k_s = per_batch(qk[0, :, :FOX_WIDTH]), per_batch(qk[0, :, FOX_WIDTH:])
            v_s = per_batch(y[0, :, 2 * FOX_WIDTH:3 * FOX_WIDTH])
            u_s = per_batch(y[0, :, 3 * FOX_WIDTH:])
            f_new = f_t[0].reshape(FOX_HEADS, bx, tnew).transpose(1, 0, 2)
            cache_kt = past["cache_fox_k"][e].transpose(0, 2, 3, 1)
            cache_vt = past["cache_fox_v"][e].transpose(0, 2, 3, 1)
            cache_lft = past["cache_fox_logf"][e].transpose(0, 2, 1)
            o_fox, lf_new = _fox_sample(q_s, k_s, v_s, f_new, p["ab_b_fgate"], cache_kt, cache_vt, cache_lft,
                                        page_table)
            pool_prev = past["state_pool"][e]
            prefix = jnp.pad(pool_prev, ((0, 0), (1, 0), (0, 0)))
            o_pool = _pool_mixer(u_s, 0, prefix, p["ab_pool_map"], p["ab_pool_scale"], pos0, tnew)
            xf = _out_project([[o_fox.reshape(1, rows, -1)], [o_pool.reshape(1, rows, -1)]], p["w_out"], xf, g1, rows)
            states["fox_k"] = _heads(k_s, FOX_HEADS)
            states["fox_v"] = _heads(v_s, FOX_HEADS)
            states["fox_logf"] = lf_new.transpose(0, 2, 1)
            states["pool"] = jnp.concatenate([pool_prev, u_s], axis=1)[:, -(POOL_HALO - 1):]
        else:
            pos_rows = pos0 + jnp.arange(rows) % tnew
            y, qn, kn, gates = _nsa_project(xf, mods, pos_rows, p, tiles)
            qw, kw_ = NSA_HEADS * HEAD_DIM, NSA_KV_WIDTH
            q_s, gates_s = per_batch(qn[0]), per_batch(gates[0])
            kn_s, y_s = per_batch(kn[0]), per_batch(y[0])
            kc_s, ks_s, kwn_s = kn_s[..., :kw_], kn_s[..., kw_:2 * kw_], kn_s[..., 2 * kw_:]
            vc_s, vs_s, vwn_s = (y_s[..., qw + kw_:qw + 2 * kw_], y_s[..., qw + 3 * kw_:qw + 4 * kw_],
                                 y_s[..., qw + 5 * kw_:qw + 6 * kw_])
            d_major = lambda a: a.transpose(0, 2, 3, 1)
            kcmp = _compress_pages(d_major(past["cache_nsa_kc"][e]), page_table, p["nsa_cmp_pos_k"],
                                   p["nsa_cmp_w1_k"], p["nsa_cmp_w2_k"])
            vcmp = _compress_pages(d_major(past["cache_nsa_vc"][e]), page_table, p["nsa_cmp_pos_v"],
                                   p["nsa_cmp_w1_v"], p["nsa_cmp_w2_v"])
            total = pos0 + tnew
            n_cmp = total // CMP_STRIDE - 1
            n_sel = -(-total // SEL_BLOCK)
            nselp = -(-n_sel // (2 * LANE)) * (2 * LANE)
            o_cmp, sel = _cmp_select(q_s, kcmp, vcmp, gates_s, n_cmp, n_sel, nselp, pos0)
            selrows = jnp.repeat(sel.reshape(bx, NSA_GROUPS, tnew, nselp), NSA_REP, axis=1).reshape(
                bx, NSA_HEADS * tnew, nselp)
            o_sel = _sel_sample(q_s, ks_s, vs_s, selrows, gates_s, d_major(past["cache_nsa_ks"][e]),
                                d_major(past["cache_nsa_vs"][e]), page_table)
            kw_prev, vw_prev = past["state_nsa_kw"][e], past["state_nsa_vw"][e]
            o_win = _win_sample(q_s, kwn_s, vwn_s, gates_s, d_major(kw_prev), d_major(vw_prev))
            flat = lambda a: a.reshape(1, rows, -1)
            xf = _out_project([[flat(o_cmp), flat(o_sel), flat(o_win)]], p["w_out"], xf, g1, rows)
            buf = kw_prev.shape[1]
            states["nsa_kc"] = _heads(kc_s, NSA_GROUPS)
            states["nsa_vc"] = _heads(vc_s, NSA_GROUPS)
            states["nsa_ks"] = _heads(ks_s, NSA_GROUPS)
            states["nsa_vs"] = _heads(vs_s, NSA_GROUPS)
            states["nsa_kw"] = jnp.concatenate([kw_prev, _heads(kwn_s, NSA_GROUPS)], axis=1)[:, -buf:]
            states["nsa_vw"] = jnp.concatenate([vw_prev, _heads(vwn_s, NSA_GROUPS)], axis=1)[:, -buf:]
        xf = _ffn(xf, p["norm_ffn"], sc2, sh2, g2, p["w_up"], p["w_down"], rows)
    return xf.reshape(bx, tnew, d), states


_STATE_NAMES = ("fox_k", "fox_v", "fox_logf", "pool", "nsa_kc", "nsa_vc", "nsa_ks", "nsa_vs", "nsa_kw", "nsa_vw")


def kernel(x_prompt, x_sample, cache_fox_k, cache_fox_v, cache_fox_logf, state_pool, cache_nsa_kc, cache_nsa_vc,
           cache_nsa_ks, cache_nsa_vs, state_nsa_kw, state_nsa_vw, page_table, c_prompt, c_sample, w_mod, b_mod,
           norm_mix, norm_ffn, w_up, w_down, ab_w_in, ab_b_fgate, ab_q_norm, ab_k_norm, ab_pool_map, ab_pool_scale,
           ab_w_out, nsa_w_in, nsa_b_gate, nsa_q_norm, nsa_k_norm, nsa_cmp_pos_k, nsa_cmp_w1_k, nsa_cmp_w2_k,
           nsa_cmp_pos_v, nsa_cmp_w1_v, nsa_cmp_w2_v, nsa_w_out):
    assert w_mod.shape[0] == 2, "one forgetting/pooling layer followed by one sparse-attention layer"
    params = {
        "norm_mix": norm_mix, "norm_ffn": norm_ffn, "w_up": w_up, "w_down": w_down,
        "ab_w_in": ab_w_in, "ab_b_fgate": ab_b_fgate, "ab_q_norm": ab_q_norm, "ab_k_norm": ab_k_norm,
        "ab_pool_map": ab_pool_map, "ab_pool_scale": ab_pool_scale, "ab_w_out": ab_w_out,
        "nsa_w_in": nsa_w_in, "nsa_b_gate": nsa_b_gate, "nsa_q_norm": nsa_q_norm, "nsa_k_norm": nsa_k_norm,
        "nsa_cmp_pos_k": nsa_cmp_pos_k, "nsa_cmp_w1_k": nsa_cmp_w1_k, "nsa_cmp_w2_k": nsa_cmp_w2_k,
        "nsa_cmp_pos_v": nsa_cmp_pos_v, "nsa_cmp_w1_v": nsa_cmp_w1_v, "nsa_cmp_w2_v": nsa_cmp_w2_v,
        "nsa_w_out": nsa_w_out,
    }
    past = {
        "cache_fox_k": cache_fox_k, "cache_fox_v": cache_fox_v, "cache_fox_logf": cache_fox_logf,
        "state_pool": state_pool, "cache_nsa_kc": cache_nsa_kc, "cache_nsa_vc": cache_nsa_vc,
        "cache_nsa_ks": cache_nsa_ks, "cache_nsa_vs": cache_nsa_vs,
        "state_nsa_kw": state_nsa_kw, "state_nsa_vw": state_nsa_vw,
    }
    n_prompt = c_prompt.shape[0]
    mod = _modulation(jnp.concatenate([c_prompt, c_sample], axis=0), w_mod, b_mod)
    layers = [_layer_params(params, layer) for layer in range(w_mod.shape[0])]
    y_prompt, sp = _prompt_trunk(x_prompt, mod[:, :n_prompt], layers)
    y_sample, ss = _sample_trunk(x_sample, mod[:, n_prompt:], layers, past, page_table)
    return (y_prompt, y_sample, *[sp[n][None] for n in _STATE_NAMES], *[ss[n][None] for n in _STATE_NAMES])
```

```python
import functools

import jax
import jax.numpy as jnp
from jax import lax
from jax.experimental import pallas as pl
from jax.experimental.pallas import tpu as pltpu

F32 = jnp.float32
BF16 = jnp.bfloat16

D_MODEL = 1024
HEAD_DIM = 64
EPS = 1e-6
ROPE_THETA = 10000.0
NEG_INF = -1e30
MASKED = -2e30
BIG = 1e9
PAGE = 128
FOX_HEADS = 8
FOX_WIDTH = FOX_HEADS * HEAD_DIM
POOL_WINDOWS = (2, 4, 8, 16)
POOL_WIDTH = 512
POOL_GROUP_CH = 128
POOL_HALO = 16
NSA_HEADS = 16
NSA_GROUPS = 4
NSA_REP = NSA_HEADS // NSA_GROUPS
NSA_KV_WIDTH = NSA_GROUPS * HEAD_DIM
NSA_BRANCHES = 3
CMP_BLOCK = 32
CMP_STRIDE = 16
SEL_BLOCK = 64
SEL_TOPK = 16
WINDOW = 512
D_FF = 2816
LANE = 128
SUBLANE = 8
GROUPS_PER_TILE = LANE // HEAD_DIM
PAGES_PER_STEP = 8
ATTN_PAGES_PER_STEP = 16
VMEM_LIMIT = 48 * 1024 * 1024


def _params(*sem):
    return pltpu.CompilerParams(dimension_semantics=sem, vmem_limit_bytes=VMEM_LIMIT)


def _nt(a, b):
    return lax.dot_general(a, b, (((1,), (1,)), ((), ())), preferred_element_type=F32)


def _mm(a, b):
    return jnp.dot(a, b, preferred_element_type=F32)


def _mod_body(c_ref, w_ref, b_ref, o_ref):
    c = c_ref[...]
    a = (c * jax.nn.sigmoid(c)).astype(BF16)
    o_ref[0] = _mm(a, w_ref[0].astype(BF16)) + b_ref[0]


def _modulation(c_all, w_mod, b_mod):
    n_layers, d, n = w_mod.shape
    rows = c_all.shape[0]
    tn = 1536
    return pl.pallas_call(
        _mod_body,
        grid=(n_layers, n // tn),
        in_specs=[
            pl.BlockSpec((rows, d), lambda l, j: (0, 0)),
            pl.BlockSpec((1, d, tn), lambda l, j: (l, 0, j)),
            pl.BlockSpec((1, 1, tn), lambda l, j: (l, 0, j)),
        ],
        out_specs=pl.BlockSpec((1, rows, tn), lambda l, j: (l, 0, j)),
        out_shape=jax.ShapeDtypeStruct((n_layers, rows, n), F32),
        compiler_params=_params("arbitrary", "arbitrary"),
        name="modulation",
    )(c_all, w_mod, b_mod.reshape(n_layers, 1, n))


def _modulated_norm(x, gain, sc, sh):
    xn = x * lax.rsqrt(jnp.mean(x * x, axis=-1, keepdims=True) + EPS) * gain
    return xn * (1.0 + sc) + sh


def _proj_body(*refs, has_side, has_t):
    x_ref, g_ref, sc_ref, sh_ref, w_ref = refs[:5]
    k = 5
    side_ref = t_ref = None
    if has_side:
        side_ref = refs[k]; k += 1
    if has_t:
        t_ref = refs[k]; k += 1
    y_ref = refs[k]; k += 1
    hb = _modulated_norm(x_ref[0], g_ref[...], sc_ref[0], sh_ref[0]).astype(BF16)
    y_ref[0] = _mm(hb, w_ref[...])
    if has_side:
        refs[k][0] = _mm(hb, side_ref[...]); k += 1
    if has_t:
        refs[k][0] = _nt(t_ref[...], hb)


def _resident(shape):
    return pl.BlockSpec(shape, lambda *_: (0,) * len(shape), pipeline_mode=pl.Buffered(1))


def _project(x, gain, sc, sh, w, tm, w_side=None, w_t=None):
    bx, tx, d = x.shape
    n_cols = w.shape[1]
    r = sc.shape[1]
    mod_spec = (pl.BlockSpec((1, tm, d), lambda b, i: (b, i, 0)) if r == tx
                else pl.BlockSpec((1, 1, d), lambda b, i: (b, 0, 0)))
    in_specs = [pl.BlockSpec((1, tm, d), lambda b, i: (b, i, 0)), _resident((1, d)), mod_spec, mod_spec,
                _resident((d, n_cols))]
    args = [x, gain.reshape(1, d), sc, sh, w]
    out_specs = [pl.BlockSpec((1, tm, n_cols), lambda b, i: (b, i, 0))]
    out_shape = [jax.ShapeDtypeStruct((bx, tx, n_cols), F32)]
    if w_side is not None:
        ns = w_side.shape[1]
        in_specs.append(_resident((d, ns)))
        args.append(w_side)
        out_specs.append(pl.BlockSpec((1, tm, ns), lambda b, i: (b, i, 0)))
        out_shape.append(jax.ShapeDtypeStruct((bx, tx, ns), F32))
    if w_t is not None:
        nt = w_t.shape[0]
        in_specs.append(_resident((nt, d)))
        args.append(w_t)
        out_specs.append(pl.BlockSpec((1, nt, tm), lambda b, i: (b, 0, i)))
        out_shape.append(jax.ShapeDtypeStruct((bx, nt, tx), F32))
    return pl.pallas_call(
        functools.partial(_proj_body, has_side=w_side is not None, has_t=w_t is not None),
        grid=(bx, tx // tm),
        in_specs=in_specs, out_specs=out_specs, out_shape=out_shape,
        compiler_params=_params("arbitrary", "arbitrary"),
        name="project",
    )(*args)


def _headnorm_body(*refs, tiles, rope, has_gate, n_out):
    y_ref, gain_ref = refs[:2]
    k = 2
    cos_ref = sin_ref = gl_ref = bg_ref = None
    if rope:
        cos_ref, sin_ref = refs[k], refs[k + 1]; k += 2
    if has_gate:
        gl_ref, bg_ref = refs[k], refs[k + 1]; k += 2
    outs = refs[k:k + n_out]
    gate_out = refs[k + n_out] if has_gate else None
    lane = lax.broadcasted_iota(jnp.int32, (1, LANE), 1)
    low_head = lane < HEAD_DIM
    first_half = (lane % HEAD_DIM) < (HEAD_DIM // 2)
    for src, oi, dst in tiles:
        y = y_ref[0, :, src * LANE:(src + 1) * LANE]
        y2 = y * y
        s_lo = jnp.sum(jnp.where(low_head, y2, 0.0), axis=-1, keepdims=True)
        s_hi = jnp.sum(jnp.where(low_head, 0.0, y2), axis=-1, keepdims=True)
        ms = jnp.where(low_head, s_lo, s_hi) * (1.0 / HEAD_DIM)
        yn = y * lax.rsqrt(ms + EPS) * gain_ref[:, src * LANE:(src + 1) * LANE]
        if rope:
            partner = jnp.where(first_half, pltpu.roll(yn, LANE - HEAD_DIM // 2, 1), pltpu.roll(yn, HEAD_DIM // 2, 1))
            yn = yn * cos_ref[...] + partner * sin_ref[...]
        outs[oi][0, :, dst * LANE:(dst + 1) * LANE] = yn
    if has_gate:
        gate_out[0] = jax.nn.sigmoid(gl_ref[0] + bg_ref[...])


def _headnorm(y, width, gains, tiles, out_widths, tm, rope_tabs=None, gate_logits=None, gate_bias=None):
    bx, tx, _ = y.shape
    rope = rope_tabs is not None
    has_gate = gate_logits is not None
    in_specs = [pl.BlockSpec((1, tm, width), lambda b, i: (b, i, 0)),
                pl.BlockSpec((1, width), lambda b, i: (0, 0))]
    args = [y, gains]
    if rope:
        in_specs += [pl.BlockSpec((tm, LANE), lambda b, i: (i, 0))] * 2
        args += list(rope_tabs)
    if has_gate:
        in_specs += [pl.BlockSpec((1, tm, LANE), lambda b, i: (b, i, 0)), pl.BlockSpec((1, LANE), lambda b, i: (0, 0))]
        args += [gate_logits, gate_bias]
    out_specs = [pl.BlockSpec((1, tm, w), lambda b, i: (b, i, 0)) for w in out_widths]
    out_shape = [jax.ShapeDtypeStruct((bx, tx, w), F32) for w in out_widths]
    if has_gate:
        out_specs.append(pl.BlockSpec((1, tm, LANE), lambda b, i: (b, i, 0)))
        out_shape.append(jax.ShapeDtypeStruct((bx, tx, LANE), F32))
    return pl.pallas_call(
        functools.partial(_headnorm_body, tiles=tuple(tiles), rope=rope, has_gate=has_gate, n_out=len(out_widths)),
        grid=(bx, tx // tm),
        in_specs=in_specs, out_specs=out_specs, out_shape=out_shape,
        compiler_params=_params("arbitrary", "arbitrary"),
        name="headnorm",
    )(*args)


def _rope_tables(pos):
    half = HEAD_DIM // 2
    inv_freq = ROPE_THETA ** (-jnp.arange(half, dtype=F32) / half)
    ang = pos.astype(F32)[:, None] * inv_freq[None, :]
    cos, sin = jnp.cos(ang), jnp.sin(ang)
    reps = LANE // HEAD_DIM
    return (jnp.tile(jnp.concatenate([cos, cos], axis=1), (1, reps)),
            jnp.tile(jnp.concatenate([-sin, sin], axis=1), (1, reps)))


def _lane_cumsum(x):
    n = x.shape[-1]
    lane = lax.broadcasted_iota(jnp.int32, x.shape, x.ndim - 1)
    s = 1
    while s < n:
        x = x + jnp.where(lane >= s, pltpu.roll(x, s, x.ndim - 1), 0.0)
        s *= 2
    return x


def _logf_body(f_ref, b_ref, lf_ref, cum_ref):
    lf = jax.nn.log_sigmoid(f_ref[0] + b_ref[...])
    lf_ref[0] = lf
    cum_ref[0] = _lane_cumsum(lf)


def _logf_cumsum(f_t, b_fgate):
    bx, h, t = f_t.shape
    spec = pl.BlockSpec((1, h, t), lambda b: (b, 0, 0))
    return pl.pallas_call(
        _logf_body, grid=(bx,),
        in_specs=[spec, pl.BlockSpec((h, 1), lambda b: (0, 0))],
        out_specs=[spec, spec],
        out_shape=[jax.ShapeDtypeStruct(f_t.shape, F32)] * 2,
        compiler_params=_params("arbitrary"),
        name="logf_cumsum",
    )(f_t, b_fgate.reshape(h, 1))


def _half_mask(shape, half):
    lane = lax.broadcasted_iota(jnp.int32, shape, len(shape) - 1)
    return (lane % LANE) // HEAD_DIM == half


def _fox_flash_body(q_ref, k_ref, v_ref, cum_ref, o_ref, qm_scr, m_scr, l_scr, acc_scr, *, tq, nk, scale):
    qi, ki = pl.program_id(1), pl.program_id(2)
    n_pairs = FOX_HEADS // 2
    reps = tq // LANE

    @pl.when(ki == 0)
    def _():
        for h in range(FOX_HEADS):
            c = h // 2
            qpair = q_ref[0, :, c * LANE:(c + 1) * LANE] * scale
            qm_scr[h] = jnp.where(_half_mask(qpair.shape, h % 2), qpair, 0.0).astype(BF16)
        m_scr[...] = jnp.full(m_scr.shape, NEG_INF, F32)
        l_scr[...] = jnp.zeros(l_scr.shape, F32)
        acc_scr[...] = jnp.zeros(acc_scr.shape, F32)

    @pl.when(ki <= qi)
    def _():
        qpos = qi * tq + lax.broadcasted_iota(jnp.int32, (tq, tq), 0)
        kpos = ki * tq + lax.broadcasted_iota(jnp.int32, (tq, tq), 1)
        vis = kpos <= qpos
        for c in range(n_pairs):
            k_pair = k_ref[0, :, c * LANE:(c + 1) * LANE].astype(BF16)
            v_pair = v_ref[0, :, c * LANE:(c + 1) * LANE]
            pv = []
            alphas = []
            for half in range(2):
                h = 2 * c + half
                s = _nt(qm_scr[h], k_pair) - cum_ref[0, h:h + 1, :]
                s = jnp.where(vis, s, MASKED)
                m_prev = m_scr[h]
                m_new = jnp.maximum(m_prev, jnp.max(s, axis=1, keepdims=True))
                alpha = jnp.exp(m_prev - m_new)
                p = jnp.exp(s - jnp.concatenate([m_new] * reps, axis=1))
                l_scr[h] = alpha * l_scr[h] + jnp.sum(p, axis=1, keepdims=True)
                m_scr[h] = m_new
                v_half = jnp.where(_half_mask(v_pair.shape, half), v_pair, 0.0).astype(BF16)
                pv.append(_mm(p.astype(BF16), v_half))
                alphas.append(alpha)
            alpha_pair = jnp.where(_half_mask(alphas[0].shape, 0), alphas[0], alphas[1])
            acc_scr[c] = alpha_pair * acc_scr[c] + (pv[0] + pv[1])

    @pl.when(ki == nk - 1)
    def _():
        for c in range(n_pairs):
            l_pair = jnp.where(_half_mask((tq, LANE), 0), l_scr[2 * c], l_scr[2 * c + 1])
            o_ref[0, :, c * LANE:(c + 1) * LANE] = acc_scr[c] / l_pair


def _fox_flash(qk, v_arr, vcb, cum, tq):
    bx, t, _ = qk.shape
    w = FOX_WIDTH
    nq = t // tq
    kmap = lambda qi, ki: jnp.minimum(ki, qi)
    return pl.pallas_call(
        functools.partial(_fox_flash_body, tq=tq, nk=nq, scale=HEAD_DIM ** -0.5),
        grid=(bx, nq, nq),
        in_specs=[pl.BlockSpec((1, tq, w), lambda b, qi, ki: (b, qi, 0)),
                  pl.BlockSpec((1, tq, w), lambda b, qi, ki: (b, kmap(qi, ki), 1)),
                  pl.BlockSpec((1, tq, w), lambda b, qi, ki: (b, kmap(qi, ki), vcb)),
                  pl.BlockSpec((1, FOX_HEADS, tq), lambda b, qi, ki: (b, 0, kmap(qi, ki)))],
        out_specs=pl.BlockSpec((1, tq, w), lambda b, qi, ki: (b, qi, 0)),
        out_shape=jax.ShapeDtypeStruct((bx, t, w), F32),
        scratch_shapes=[pltpu.VMEM((FOX_HEADS, tq, LANE), BF16), pltpu.VMEM((FOX_HEADS, tq, LANE), F32),
                        pltpu.VMEM((FOX_HEADS, tq, LANE), F32), pltpu.VMEM((FOX_HEADS // 2, tq, LANE), F32)],
        compiler_params=_params("arbitrary", "arbitrary", "arbitrary"),
        name="fox_flash",
    )(qk, qk, v_arr, cum)


def _nsa_flash_body(*refs, tq, nk, mode, has_sel, gate_branch, scale):
    q_ref, k_ref, v_ref = refs[:3]
    i = 3
    sel_ref = None
    if has_sel:
        sel_ref = refs[i]; i += 1
    gate_ref, o_ref, qt_scr, m_scr, l_scr, acc_scr = refs[i:i + 6]
    qi, ki = pl.program_id(1), pl.program_id(2)
    kt = ki if mode == "causal" else qi - (nk - 1) + ki
    active = (ki <= qi) if mode == "causal" else (kt >= 0)

    @pl.when(ki == 0)
    def _():
        zeros = jnp.zeros((HEAD_DIM, tq), F32)
        for pair in range(NSA_HEADS // 2):
            q_pair_t = (q_ref[0, :, pair * LANE:(pair + 1) * LANE] * scale).T
            for half in range(2):
                g, r = divmod(2 * pair + half, NSA_REP)
                q_t = q_pair_t[half * HEAD_DIM:(half + 1) * HEAD_DIM]
                parts = [q_t, zeros] if g % GROUPS_PER_TILE == 0 else [zeros, q_t]
                qt_scr[g, :, r * tq:(r + 1) * tq] = jnp.concatenate(parts, axis=0).astype(BF16)
        m_scr[...] = jnp.full(m_scr.shape, NEG_INF, F32)
        l_scr[...] = jnp.zeros(l_scr.shape, F32)
        acc_scr[...] = jnp.zeros(acc_scr.shape, F32)

    @pl.when(active)
    def _():
        kpos = kt * tq + lax.broadcasted_iota(jnp.int32, (tq, tq), 0)
        qpos = qi * tq + lax.broadcasted_iota(jnp.int32, (tq, tq), 1)
        vis = kpos <= qpos
        if mode == "window":
            vis = vis & (kpos > qpos - WINDOW)
        if has_sel:
            nsel = sel_ref.shape[2]
            kblk = (kt * tq + lax.broadcasted_iota(jnp.int32, (tq, nsel), 0)) // SEL_BLOCK
            expand = (kblk == lax.broadcasted_iota(jnp.int32, (tq, nsel), 1)).astype(BF16)
        logits, v_ts = [], []
        for c in range(NSA_GROUPS // GROUPS_PER_TILE):
            k_pair = k_ref[0, :, c * LANE:(c + 1) * LANE].astype(BF16)
            v_pair_t = v_ref[0, :, c * LANE:(c + 1) * LANE].T
            for gg in range(GROUPS_PER_TILE):
                g = c * GROUPS_PER_TILE + gg
                v_ts.append(v_pair_t[gg * HEAD_DIM:(gg + 1) * HEAD_DIM].astype(BF16))
                vis_g = vis
                if has_sel:
                    vis_g = vis & (_mm(expand, sel_ref[0, g].astype(BF16)) > 0.5)
                vis_w = jnp.concatenate([vis_g] * NSA_REP, axis=1)
                logits.append(jnp.where(vis_w, _mm(k_pair, qt_scr[g]), MASKED))
        for g in range(NSA_GROUPS):
            s = logits[g]
            m_prev = m_scr[g]
            m_new = jnp.maximum(m_prev, jnp.max(s, axis=0, keepdims=True))
            alpha = jnp.exp(m_prev - m_new)
            p = jnp.exp(s - m_new[0:1])
            l_scr[g] = alpha * l_scr[g] + jnp.sum(p, axis=0, keepdims=True)
            acc_scr[g] = alpha[0:1] * acc_scr[g] + _mm(v_ts[g], p.astype(BF16))
            m_scr[g] = m_new

    @pl.when(ki == nk - 1)
    def _():
        low = _half_mask((tq, LANE), 0)
        for pair in range(NSA_HEADS // 2):
            parts, gate_cols = [], []
            for half in range(2):
                h = 2 * pair + half
                g, r = divmod(h, NSA_REP)
                lanes = slice(r * tq, (r + 1) * tq)
                parts.append(acc_scr[g, :, lanes] / l_scr[g, 0:1, lanes])
                c = h * NSA_BRANCHES + gate_branch
                gate_cols.append(gate_ref[0, :, c:c + 1])
            o_pair = jnp.concatenate(parts, axis=0).T
            o_ref[0, :, pair * LANE:(pair + 1) * LANE] = o_pair * jnp.where(low, gate_cols[0], gate_cols[1])


def _nsa_flash(qn, kn, kcb, y, vcb, gates, gate_branch, tq, mode, sel_t=None):
    bx, t, wq = qn.shape
    wk = NSA_KV_WIDTH
    nq = t // tq
    nk = nq if mode == "causal" else WINDOW // tq + 1
    if mode == "causal":
        kmap = lambda qi, ki: jnp.minimum(ki, qi)
    else:
        kmap = lambda qi, ki: jnp.maximum(qi - (nk - 1) + ki, 0)
    in_specs = [pl.BlockSpec((1, tq, wq), lambda b, qi, ki: (b, qi, 0)),
                pl.BlockSpec((1, tq, wk), lambda b, qi, ki: (b, kmap(qi, ki), kcb)),
                pl.BlockSpec((1, tq, wk), lambda b, qi, ki: (b, kmap(qi, ki), vcb))]
    args = [qn, kn, y]
    if sel_t is not None:
        in_specs.append(pl.BlockSpec((1, NSA_GROUPS, sel_t.shape[2], tq), lambda b, qi, ki: (b, 0, 0, qi)))
        args.append(sel_t)
    in_specs.append(pl.BlockSpec((1, tq, LANE), lambda b, qi, ki: (b, qi, 0)))
    args.append(gates)
    return pl.pallas_call(
        functools.partial(_nsa_flash_body, tq=tq, nk=nk, mode=mode, has_sel=sel_t is not None,
                          gate_branch=gate_branch, scale=HEAD_DIM ** -0.5),
        grid=(bx, nq, nk),
        in_specs=in_specs,
        out_specs=pl.BlockSpec((1, tq, wq), lambda b, qi, ki: (b, qi, 0)),
        out_shape=jax.ShapeDtypeStruct((bx, t, wq), F32),
        scratch_shapes=[pltpu.VMEM((NSA_GROUPS, LANE, NSA_REP * tq), BF16),
                        pltpu.VMEM((NSA_GROUPS, SUBLANE, NSA_REP * tq), F32),
                        pltpu.VMEM((NSA_GROUPS, SUBLANE, NSA_REP * tq), F32),
                        pltpu.VMEM((NSA_GROUPS, HEAD_DIM, NSA_REP * tq), F32)],
        compiler_params=_params("arbitrary", "arbitrary", "arbitrary"),
        name="nsa_flash_" + mode,
    )(*args)


def _pool_body(u_ref, pre_ref, map_ref, scale_ref, o_ref, ext_scr, *, tm, pos0):
    j = pl.program_id(1)

    @pl.when(j == 0)
    def _():
        ext_scr[0:POOL_HALO] = pre_ref[0]

    @pl.when(j > 0)
    def _():
        ext_scr[0:POOL_HALO] = ext_scr[tm:tm + POOL_HALO]

    ext_scr[POOL_HALO:POOL_HALO + tm] = u_ref[0]
    qpos = pos0 + j * tm + lax.broadcasted_iota(jnp.int32, (tm, 1), 0)
    for g, w in enumerate(POOL_WINDOWS):
        lo, hi = g * POOL_GROUP_CH, (g + 1) * POOL_GROUP_CH
        u_new = ext_scr[POOL_HALO:POOL_HALO + tm, lo:hi]
        tot = u_new
        for d in range(1, w):
            tot = tot + ext_scr[POOL_HALO - d:POOL_HALO - d + tm, lo:hi]
        count = jnp.minimum(w, qpos + 1).astype(F32)
        diff = tot / count - u_new
        y = _mm(diff.astype(BF16), map_ref[g].astype(BF16))
        o_ref[0, :, lo:hi] = y * scale_ref[:, lo:hi]


def _pool_mixer(u_arr, ucb, prefix, w_map, scale, pos0, tm):
    bx, t, _ = u_arr.shape
    c = POOL_WIDTH
    return pl.pallas_call(
        functools.partial(_pool_body, tm=tm, pos0=pos0),
        grid=(bx, t // tm),
        in_specs=[pl.BlockSpec((1, tm, c), lambda b, j: (b, j, ucb)),
                  pl.BlockSpec((1, POOL_HALO, c), lambda b, j: (b, 0, 0)),
                  pl.BlockSpec((len(POOL_WINDOWS), POOL_GROUP_CH, POOL_GROUP_CH), lambda b, j: (0, 0, 0)),
                  pl.BlockSpec((1, c), lambda b, j: (0, 0))],
        out_specs=pl.BlockSpec((1, tm, c), lambda b, j: (b, j, 0)),
        out_shape=jax.ShapeDtypeStruct((bx, t, c), F32),
        scratch_shapes=[pltpu.VMEM((POOL_HALO + tm, c), F32)],
        compiler_params=_params("arbitrary", "arbitrary"),
        name="pool_mixer",
    )(u_arr, prefix, w_map, scale.reshape(1, c))


def _outproj_body(*refs, group_sizes):
    n_a = sum(group_sizes)
    a_refs = refs[:n_a]
    w_refs = refs[n_a:n_a + len(group_sizes)]
    res_ref, gate_ref, o_ref = refs[n_a + len(group_sizes):]
    y = None
    k = 0
    for gi, n in enumerate(group_sizes):
        a = a_refs[k][0]
        for r in a_refs[k + 1:k + n]:
            a = a + r[0]
        k += n
        part = _mm(a.astype(BF16), w_refs[gi][...])
        y = part if y is None else y + part
    o_ref[0] = res_ref[0] + gate_ref[0] * y


def _out_project(groups, w, res, gate, tm):
    bx, tx, d = res.shape
    kg = groups[0][0].shape[-1]
    r = gate.shape[1]
    row = pl.BlockSpec((1, tm, d), lambda b, i: (b, i, 0))
    a_spec = pl.BlockSpec((1, tm, kg), lambda b, i: (b, i, 0))
    in_specs, args = [], []
    for grp in groups:
        for a in grp:
            in_specs.append(a_spec); args.append(a)
    for gi in range(len(groups)):
        in_specs.append(pl.BlockSpec((kg, d), lambda b, i, gi=gi: (gi, 0), pipeline_mode=pl.Buffered(1)))
        args.append(w)
    in_specs += [row, row if r == tx else pl.BlockSpec((1, 1, d), lambda b, i: (b, 0, 0))]
    args += [res, gate]
    return pl.pallas_call(
        functools.partial(_outproj_body, group_sizes=tuple(len(g) for g in groups)),
        grid=(bx, tx // tm),
        in_specs=in_specs,
        out_specs=row,
        out_shape=jax.ShapeDtypeStruct((bx, tx, d), F32),
        compiler_params=_params("arbitrary", "arbitrary"),
        name="out_project",
    )(*args)


FFN_SLICES = 2


def _ffn_body(x_ref, g_ref, sc_ref, sh_ref, gate_ref, wg_ref, wu_ref, wd_ref, o_ref):
    x = x_ref[0]
    h = _modulated_norm(x, g_ref[...], sc_ref[0], sh_ref[0]).astype(BF16)
    width = D_FF // FFN_SLICES
    y = None
    for c in range(FFN_SLICES):
        cols = slice(c * width, (c + 1) * width)
        gt = _mm(h, wg_ref[:, cols])
        up = _mm(h, wu_ref[:, cols])
        act = ((gt * jax.nn.sigmoid(gt)) * up).astype(BF16)
        part = _mm(act, wd_ref[cols, :])
        y = part if y is None else y + part
    o_ref[0] = x + gate_ref[0] * y


def _ffn(x, gain, sc, sh, gate, w_up, w_down, tm):
    bx, tx, d = x.shape
    r = sc.shape[1]
    mod_spec = (pl.BlockSpec((1, tm, d), lambda b, i: (b, i, 0)) if r == tx
                else pl.BlockSpec((1, 1, d), lambda b, i: (b, 0, 0)))
    x_spec = pl.BlockSpec((1, tm, d), lambda b, i: (b, i, 0))
    half = lambda j: pl.BlockSpec((d, D_FF), lambda b, i: (0, j), pipeline_mode=pl.Buffered(1))
    return pl.pallas_call(
        _ffn_body,
        grid=(bx, tx // tm),
        in_specs=[x_spec, _resident((1, d)), mod_spec, mod_spec, mod_spec, half(0), half(1), _resident((D_FF, d))],
        out_specs=x_spec,
        out_shape=jax.ShapeDtypeStruct(x.shape, F32),
        compiler_params=_params("arbitrary", "arbitrary"),
        name="swiglu",
    )(x, gain.reshape(1, d), sc, sh, gate, w_up, w_up, w_down)


def _chunk_rows_body(a_ref, o_ref, *, n_chunks):
    for s in range(CMP_STRIDE):
        rows = a_ref[0, pl.ds(s, n_chunks, stride=CMP_STRIDE), :]
        for g in range(GROUPS_PER_TILE):
            o_ref[0, g, :, s * HEAD_DIM:(s + 1) * HEAD_DIM] = rows[:, g * HEAD_DIM:(g + 1) * HEAD_DIM]


def _chunk_rows(a, acb, tm):
    bx, t, _ = a.shape
    n_chunks = tm // CMP_STRIDE
    tiles = NSA_KV_WIDTH // LANE
    return pl.pallas_call(
        functools.partial(_chunk_rows_body, n_chunks=n_chunks),
        grid=(bx, t // tm, tiles),
        in_specs=[pl.BlockSpec((1, tm, LANE), lambda b, i, c: (b, i, acb * tiles + c))],
        out_specs=pl.BlockSpec((1, GROUPS_PER_TILE, n_chunks, CMP_STRIDE * HEAD_DIM), lambda b, i, c: (b, c, i, 0)),
        out_shape=jax.ShapeDtypeStruct((bx, NSA_GROUPS, t // CMP_STRIDE, CMP_STRIDE * HEAD_DIM), F32),
        compiler_params=_params("arbitrary", "arbitrary", "arbitrary"),
        name="chunk_rows",
    )(a)


def _compress_mlp(a, pos_ref, w1_ref, w2_ref):
    half = CMP_STRIDE * HEAD_DIM
    w1 = w1_ref[...].astype(BF16)
    first = _mm(a, w1[:half])
    second = _mm(a, w1[half:])
    bias = _mm(pos_ref[...].astype(BF16), w1)
    hidden = (first + pltpu.roll(second, a.shape[0] - 1, 0)) + bias
    return _mm(jax.nn.gelu(hidden).astype(BF16), w2_ref[...].astype(BF16))


def _compress_body(a_ref, pos_ref, w1_ref, w2_ref, o_ref):
    o_ref[0, 0] = _compress_mlp(a_ref[0, 0].astype(BF16), pos_ref, w1_ref, w2_ref)


def _compress(chunks, pos_emb, w1, w2):
    bx, g, n_chunks, half = chunks.shape
    hidden = w1.shape[1]
    return pl.pallas_call(
        _compress_body,
        grid=(bx, g),
        in_specs=[pl.BlockSpec((1, 1, n_chunks, half), lambda b, i: (b, i, 0, 0)),
                  pl.BlockSpec((1, 2 * half), lambda b, i: (0, 0)),
                  pl.BlockSpec((2 * half, hidden), lambda b, i: (0, 0)),
                  pl.BlockSpec((hidden, HEAD_DIM), lambda b, i: (0, 0))],
        out_specs=pl.BlockSpec((1, 1, n_chunks, HEAD_DIM), lambda b, i: (b, i, 0, 0)),
        out_shape=jax.ShapeDtypeStruct((bx, g, n_chunks, HEAD_DIM), F32),
        compiler_params=_params("arbitrary", "arbitrary"),
        name="compress",
    )(chunks, pos_emb.reshape(1, 2 * half), w1, w2)


def _compress_pages_body(pt_ref, *rest, n_steps, pps):
    del pt_ref
    page_refs = rest[:pps]
    pos_ref, w1_ref, w2_ref, o_ref, tok_scr, chunk_scr = rest[pps:]
    p = pl.program_id(1)
    cpp = PAGE // CMP_STRIDE
    pairs = NSA_GROUPS // GROUPS_PER_TILE
    for i in range(pps):
        row0 = pl.multiple_of((p * pps + i) * cpp, cpp)
        for c in range(pairs):
            tok = tok_scr.at[i * pairs + c]
            tok[...] = page_refs[i][0, c * GROUPS_PER_TILE:(c + 1) * GROUPS_PER_TILE].reshape(LANE, PAGE).T
            for s in range(CMP_STRIDE):
                rows = tok[pl.ds(s, cpp, stride=CMP_STRIDE), :]
                for g in range(GROUPS_PER_TILE):
                    chunk_scr[c * GROUPS_PER_TILE + g, pl.ds(row0, cpp), s * HEAD_DIM:(s + 1) * HEAD_DIM] = (
                        rows[:, g * HEAD_DIM:(g + 1) * HEAD_DIM])

    @pl.when(p == n_steps - 1)
    def _():
        for g in range(NSA_GROUPS):
            o_ref[0, :, g * HEAD_DIM:(g + 1) * HEAD_DIM] = _compress_mlp(
                chunk_scr[g].astype(BF16), pos_ref, w1_ref, w2_ref)


def _compress_pages(cache_t, page_table, pos_emb, w1, w2):
    bx, n_pages = page_table.shape
    pps = _pages_per_step(n_pages)
    cpp = PAGE // CMP_STRIDE
    n_chunks = n_pages * cpp
    half = CMP_STRIDE * HEAD_DIM
    hidden = w1.shape[1]
    const = lambda b, p, pt: (0, 0)
    return pl.pallas_call(
        functools.partial(_compress_pages_body, n_steps=n_pages // pps, pps=pps),
        grid_spec=pltpu.PrefetchScalarGridSpec(
            num_scalar_prefetch=1, grid=(bx, n_pages // pps),
            in_specs=[pl.BlockSpec((1, NSA_GROUPS, HEAD_DIM, PAGE), _page_map(n_pages, pps, i, 4)) for i in range(pps)]
            + [pl.BlockSpec((1, 2 * half), const), pl.BlockSpec((2 * half, hidden), const),
               pl.BlockSpec((hidden, HEAD_DIM), const)],
            out_specs=pl.BlockSpec((1, n_chunks, NSA_KV_WIDTH), lambda b, p, pt: (b, 0, 0)),
            scratch_shapes=[pltpu.VMEM((pps * NSA_GROUPS // GROUPS_PER_TILE, PAGE, LANE), F32),
                            pltpu.VMEM((NSA_GROUPS, n_chunks, half), F32)]),
        out_shape=jax.ShapeDtypeStruct((bx, n_chunks, NSA_KV_WIDTH), F32),
        compiler_params=_params("arbitrary", "arbitrary"),
        name="compress_pages",
    )(page_table.reshape(-1), *([cache_t] * pps), pos_emb.reshape(1, 2 * half), w1, w2)


def _bf16_terms(x):
    hi = x.astype(BF16)
    r1 = x - hi.astype(F32)
    mid = r1.astype(BF16)
    lo = (r1 - mid.astype(F32)).astype(BF16)
    return hi, mid, lo


def _cmp_select_body(q_ref, kc_ref, vc_ref, gate_ref, o_ref, sel_ref, qbd_scr, *, tq, n_cmp, n_sel, pos0, scale):
    ncp = kc_ref.shape[1]
    nselp = sel_ref.shape[-1]
    gt = NSA_GROUPS * tq
    rows = NSA_REP * gt
    qbd_scr[...] = jnp.zeros(qbd_scr.shape, F32)
    for h in range(NSA_HEADS):
        g, r = divmod(h, NSA_REP)
        qbd_scr[r * gt + g * tq:r * gt + (g + 1) * tq, g * HEAD_DIM:(g + 1) * HEAD_DIM] = (
            q_ref[0, :, h * HEAD_DIM:(h + 1) * HEAD_DIM] * scale)
    row_pos = pos0 + lax.broadcasted_iota(jnp.int32, (rows, 1), 0) % tq
    c_idx = lax.broadcasted_iota(jnp.int32, (1, ncp), 1)
    c_valid = (c_idx * CMP_STRIDE + (CMP_BLOCK - 1) <= row_pos) & (c_idx < n_cmp)
    s = jnp.where(c_valid, _nt(qbd_scr[...].astype(BF16), kc_ref[0].astype(BF16)), NEG_INF)
    m = jnp.max(s, axis=-1, keepdims=True)
    p = jnp.where(c_valid, jnp.exp(s - m), 0.0)
    l = jnp.sum(p, axis=-1, keepdims=True)
    pc = jnp.where(l > 0.0, p / jnp.where(l > 0.0, l, 1.0), 0.0)
    o = _mm(pc.astype(BF16), vc_ref[0].astype(BF16))
    for h in range(NSA_HEADS):
        g, r = divmod(h, NSA_REP)
        c = h * NSA_BRANCHES
        o_ref[0, :, h * HEAD_DIM:(h + 1) * HEAD_DIM] = (
            o[r * gt + g * tq:r * gt + (g + 1) * tq, g * HEAD_DIM:(g + 1) * HEAD_DIM] * gate_ref[0, :, c:c + 1])
    pc_sum = pc[0:gt]
    for r in range(1, NSA_REP):
        pc_sum = pc_sum + pc[r * gt:(r + 1) * gt]
    cj = lax.broadcasted_iota(jnp.int32, (ncp, nselp), 0) * CMP_STRIDE
    sj = lax.broadcasted_iota(jnp.int32, (ncp, nselp), 1) * SEL_BLOCK
    overlap = ((cj < sj + SEL_BLOCK) & (cj + (CMP_BLOCK - 1) >= sj)).astype(BF16)
    terms = _mm(jnp.concatenate(_bf16_terms(pc_sum), axis=0), overlap)
    imp = (terms[0:gt] + terms[gt:2 * gt]) + terms[2 * gt:3 * gt]
    gpos = pos0 + lax.broadcasted_iota(jnp.int32, (gt, 1), 0) % tq
    j_idx = lax.broadcasted_iota(jnp.int32, (1, nselp), 1)
    forced = (j_idx == 0) | (j_idx == gpos // SEL_BLOCK)
    valid = j_idx * SEL_BLOCK <= gpos
    score = jnp.where(valid, jnp.where(forced, BIG, imp), -BIG)
    rank = jnp.zeros(score.shape, jnp.int32)
    for i in range(n_sel):
        si = score[:, i:i + 1]
        ahead = (si > score) | ((si == score) & (i < j_idx))
        rank = rank + ahead.astype(jnp.int32)
    sel_ref[0] = ((rank < SEL_TOPK) & valid).astype(F32)


def _cmp_select(qn, kcmp, vcmp, gates, n_cmp, n_sel, nselp, pos0):
    bx, t, wq = qn.shape
    ncp = kcmp.shape[1]
    assert (NSA_GROUPS * t) % (2 * SUBLANE) == 0
    whole = lambda b: (b, 0, 0)
    cmp_spec = pl.BlockSpec((1, ncp, NSA_KV_WIDTH), whole)
    return pl.pallas_call(
        functools.partial(_cmp_select_body, tq=t, n_cmp=n_cmp, n_sel=n_sel, pos0=pos0, scale=HEAD_DIM ** -0.5),
        grid=(bx,),
        in_specs=[pl.BlockSpec((1, t, wq), whole), cmp_spec, cmp_spec, pl.BlockSpec((1, t, LANE), whole)],
        out_specs=[pl.BlockSpec((1, t, wq), whole), pl.BlockSpec((1, NSA_GROUPS * t, nselp), whole)],
        out_shape=[jax.ShapeDtypeStruct((bx, t, wq), F32),
                   jax.ShapeDtypeStruct((bx, NSA_GROUPS * t, nselp), F32)],
        scratch_shapes=[pltpu.VMEM((NSA_HEADS * t, NSA_KV_WIDTH), F32)],
        compiler_params=_params("arbitrary"),
        name="cmp_select",
    )(qn, kcmp, vcmp, gates)


def _cmp_select_t_body(q_ref, kc_ref, vc_ref, gate_ref, o_ref, sel_ref, *, tq, n_cmp, n_sel, pos0, scale):
    g, qi = pl.program_id(1), pl.program_id(2)
    ncp = kc_ref.shape[2]
    nselp = sel_ref.shape[2]
    qpos = pos0 + qi * tq + lax.broadcasted_iota(jnp.int32, (1, tq), 1)
    c_idx = lax.broadcasted_iota(jnp.int32, (ncp, 1), 0)
    c_valid = (c_idx * CMP_STRIDE + (CMP_BLOCK - 1) <= qpos) & (c_idx < n_cmp)
    kc = kc_ref[0, 0].astype(BF16)
    vc_t = vc_ref[0, 0].T.astype(BF16)
    lane = lax.broadcasted_iota(jnp.int32, (1, LANE), 1)
    q_t = jnp.concatenate([(q_ref[0, :, p * LANE:(p + 1) * LANE] * scale).T
                           for p in range(NSA_REP // GROUPS_PER_TILE)], axis=0).astype(BF16)
    q_wide = jnp.concatenate([q_t[r * HEAD_DIM:(r + 1) * HEAD_DIM] for r in range(NSA_REP)], axis=1)
    valid_w = jnp.concatenate([c_valid] * NSA_REP, axis=1)
    s = jnp.where(valid_w, _mm(kc, q_wide), NEG_INF)
    m = jnp.max(s, axis=0, keepdims=True)
    p = jnp.where(valid_w, jnp.exp(s - m), 0.0)
    l = jnp.sum(p, axis=0, keepdims=True)
    pc = jnp.where(l > 0.0, p / jnp.where(l > 0.0, l, 1.0), 0.0)
    o_t = _mm(vc_t, pc.astype(BF16))
    pc_sum = pc[:, 0:tq]
    for r in range(1, NSA_REP):
        pc_sum = pc_sum + pc[:, r * tq:(r + 1) * tq]
    low = _half_mask((tq, LANE), 0)
    for pair in range(NSA_REP // GROUPS_PER_TILE):
        gate_cols = []
        for half in range(GROUPS_PER_TILE):
            col = (g * NSA_REP + pair * GROUPS_PER_TILE + half) * NSA_BRANCHES
            gate_cols.append(jnp.sum(jnp.where(lane == col, gate_ref[0], 0.0), axis=-1, keepdims=True))
        r0 = pair * GROUPS_PER_TILE
        o_pair = jnp.concatenate([o_t[:, r0 * tq:(r0 + 1) * tq], o_t[:, (r0 + 1) * tq:(r0 + 2) * tq]], axis=0).T
        o_ref[0, :, pair * LANE:(pair + 1) * LANE] = o_pair * jnp.where(low, gate_cols[0], gate_cols[1])
    sj = lax.broadcasted_iota(jnp.int32, (nselp, ncp), 0) * SEL_BLOCK
    cj = lax.broadcasted_iota(jnp.int32, (nselp, ncp), 1) * CMP_STRIDE
    overlap_t = ((cj < sj + SEL_BLOCK) & (cj + (CMP_BLOCK - 1) >= sj)).astype(F32)
    imp = jnp.dot(overlap_t, pc_sum, preferred_element_type=F32, precision=lax.Precision.HIGHEST)
    j_idx = lax.broadcasted_iota(jnp.int32, (nselp, 1), 0)
    forced = (j_idx == 0) | (j_idx == qpos // SEL_BLOCK)
    valid = j_idx * SEL_BLOCK <= qpos
    score = jnp.where(valid, jnp.where(forced, BIG, imp), -BIG)
    rank = jnp.zeros((nselp, tq), jnp.int32)
    for i in range(n_sel):
        si = score[i:i + 1, :]
        ahead = (si > score) | ((si == score) & (i < j_idx))
        rank = rank + ahead.astype(jnp.int32)
    sel_ref[0, 0] = ((rank < SEL_TOPK) & valid).astype(F32)


def _cmp_select_t(qn, kcmp, vcmp, gates, n_cmp, n_sel, pos0, tq):
    bx, t, _ = qn.shape
    ncp = kcmp.shape[2]
    gw = NSA_REP * HEAD_DIM
    nselp = -(-n_sel // SUBLANE) * SUBLANE
    return pl.pallas_call(
        functools.partial(_cmp_select_t_body, tq=tq, n_cmp=n_cmp, n_sel=n_sel, pos0=pos0, scale=HEAD_DIM ** -0.5),
        grid=(bx, NSA_GROUPS, t // tq),
        in_specs=[pl.BlockSpec((1, tq, gw), lambda b, g, i: (b, i, g)),
                  pl.BlockSpec((1, 1, ncp, HEAD_DIM), lambda b, g, i: (b, g, 0, 0)),
                  pl.BlockSpec((1, 1, ncp, HEAD_DIM), lambda b, g, i: (b, g, 0, 0)),
                  pl.BlockSpec((1, tq, LANE), lambda b, g, i: (b, i, 0))],
        out_specs=[pl.BlockSpec((1, tq, gw), lambda b, g, i: (b, i, g)),
                   pl.BlockSpec((1, 1, nselp, tq), lambda b, g, i: (b, g, 0, i))],
        out_shape=[jax.ShapeDtypeStruct((bx, t, NSA_HEADS * HEAD_DIM), F32),
                   jax.ShapeDtypeStruct((bx, NSA_GROUPS, nselp, t), F32)],
        compiler_params=_params("arbitrary", "arbitrary", "arbitrary"),
        name="cmp_select_t",
    )(qn, kcmp, vcmp, gates)


def _pages_per_step(n_pages, want=PAGES_PER_STEP):
    pps = min(want, n_pages)
    assert n_pages % pps == 0
    return pps


def _page_map(n_pages, pps, i, rank):
    return lambda b, p, pt: (pt[b * n_pages + p * pps + i],) + (0,) * (rank - 1)


def _softmax_step(s, vis, m_scr, l_scr, acc_scr, pv_fn):
    m_prev = m_scr[...]
    m_new = jnp.maximum(m_prev, jnp.max(s, axis=-1, keepdims=True))
    alpha = jnp.exp(m_prev - m_new)
    p = jnp.exp(s - m_new)
    if vis is not None:
        p = jnp.where(vis, p, 0.0)
    l_scr[...] = alpha * l_scr[...] + jnp.sum(p, axis=-1, keepdims=True)
    acc_scr[...] = alpha * acc_scr[...] + pv_fn(p)
    m_scr[...] = m_new


def _new_rows_step(qbd, k_new, v_new, row_t, bias_cols, m_scr, l_scr, acc_scr, tnew):
    cols = []
    for j in range(tnew):
        sj = jnp.sum(qbd * k_new[j:j + 1, :], axis=-1, keepdims=True)
        if bias_cols is not None:
            sj = sj - bias_cols[j]
        cols.append(jnp.where(row_t >= j, sj, NEG_INF))
    m_prev = m_scr[...]
    m_new = m_prev
    for sj in cols:
        m_new = jnp.maximum(m_new, sj)
    alpha = jnp.exp(m_prev - m_new)
    l = alpha * l_scr[...]
    acc = alpha * acc_scr[...]
    for j, sj in enumerate(cols):
        pj = jnp.where(row_t >= j, jnp.exp(sj - m_new), 0.0)
        l = l + pj
        acc = acc + pj * v_new[j:j + 1, :]
    return acc / l


def _init_softmax(m_scr, l_scr, acc_scr):
    m_scr[...] = jnp.full(m_scr.shape, NEG_INF, F32)
    l_scr[...] = jnp.zeros(l_scr.shape, F32)
    acc_scr[...] = jnp.zeros(acc_scr.shape, F32)


def _fox_sample_body(pt_ref, q_ref, kn_ref, vn_ref, fn_ref, bf_ref, *rest, n_steps, pps, tnew, scale):
    del pt_ref
    kt_refs, vt_refs, lf_refs = rest[:pps], rest[pps:2 * pps], rest[2 * pps:3 * pps]
    o_ref, lfo_ref, qbd_scr, m_scr, l_scr, acc_scr, carry_scr = rest[3 * pps:]
    p = pl.program_id(1)
    nh, w = FOX_HEADS, FOX_WIDTH
    rows = tnew * nh
    head_of_lane = lax.broadcasted_iota(jnp.int32, (nh, w), 1) // HEAD_DIM
    hmask = (head_of_lane == lax.broadcasted_iota(jnp.int32, (nh, w), 0)).astype(F32)

    @pl.when(p == 0)
    def _():
        for t in range(tnew):
            qbd_scr[t * nh:(t + 1) * nh, :] = q_ref[0, t:t + 1, :] * scale * hmask
        _init_softmax(m_scr, l_scr, acc_scr)
        carry_scr[...] = jnp.zeros(carry_scr.shape, F32)

    tri = (lax.broadcasted_iota(jnp.int32, (PAGE, PAGE), 0) <= lax.broadcasted_iota(jnp.int32, (PAGE, PAGE), 1)
           ).astype(F32)
    carry = carry_scr[...]
    cums = []
    for i in range(pps):
        within = jnp.dot(lf_refs[i][0], tri, preferred_element_type=F32, precision=lax.Precision.HIGHEST)
        cums.append(carry + within)
        carry = carry + within[:, PAGE - 1:PAGE]
    carry_scr[...] = carry
    bias = jnp.concatenate([jnp.concatenate(cums, axis=1)] * tnew, axis=0)
    kt = jnp.concatenate([r[0].reshape(w, PAGE).astype(BF16) for r in kt_refs], axis=1)
    vt = jnp.concatenate([r[0].reshape(w, PAGE).astype(BF16) for r in vt_refs], axis=1)
    s = _mm(qbd_scr[...].astype(BF16), kt) - bias
    _softmax_step(s, None, m_scr, l_scr, acc_scr, lambda pr: _nt(pr.astype(BF16), vt))

    @pl.when(p == n_steps - 1)
    def _():
        lf_new = jax.nn.log_sigmoid(fn_ref[0] + bf_ref[...])
        lfo_ref[0] = lf_new
        run = carry_scr[...]
        bias_cols = []
        for j in range(tnew):
            run = run + lf_new[:, j:j + 1]
            bias_cols.append(jnp.concatenate([run] * tnew, axis=0))
        row_t = lax.broadcasted_iota(jnp.int32, (rows, 1), 0) // nh
        o = _new_rows_step(qbd_scr[...], kn_ref[0], vn_ref[0], row_t, bias_cols, m_scr, l_scr, acc_scr, tnew)
        o = o * jnp.concatenate([hmask] * tnew, axis=0)
        for t in range(tnew):
            o_ref[0, t:t + 1, :] = jnp.sum(o[t * nh:(t + 1) * nh], axis=0, keepdims=True)


def _fox_sample(q, k_new, v_new, f_new, b_fgate, cache_kt, cache_vt, cache_lft, page_table):
    bx, tnew, w = q.shape
    n_pages = page_table.shape[1]
    pps = _pages_per_step(n_pages, ATTN_PAGES_PER_STEP)
    nh = FOX_HEADS
    rows = tnew * nh
    new_spec = pl.BlockSpec((1, tnew, w), lambda b, p, pt: (b, 0, 0))
    kv_specs = [pl.BlockSpec((1, nh, HEAD_DIM, PAGE), _page_map(n_pages, pps, i, 4)) for i in range(pps)]
    lf_specs = [pl.BlockSpec((1, nh, PAGE), _page_map(n_pages, pps, i, 3)) for i in range(pps)]
    return pl.pallas_call(
        functools.partial(_fox_sample_body, n_steps=n_pages // pps, pps=pps, tnew=tnew, scale=HEAD_DIM ** -0.5),
        grid_spec=pltpu.PrefetchScalarGridSpec(
            num_scalar_prefetch=1, grid=(bx, n_pages // pps),
            in_specs=[new_spec, new_spec, new_spec,
                      pl.BlockSpec((1, nh, tnew), lambda b, p, pt: (b, 0, 0)),
                      pl.BlockSpec((nh, 1), lambda b, p, pt: (0, 0))] + kv_specs + kv_specs + lf_specs,
            out_specs=[new_spec, pl.BlockSpec((1, nh, tnew), lambda b, p, pt: (b, 0, 0))],
            scratch_shapes=[pltpu.VMEM((rows, w), F32), pltpu.VMEM((rows, 1), F32), pltpu.VMEM((rows, 1), F32),
                            pltpu.VMEM((rows, w), F32), pltpu.VMEM((nh, 1), F32)]),
        out_shape=[jax.ShapeDtypeStruct((bx, tnew, w), F32), jax.ShapeDtypeStruct((bx, nh, tnew), F32)],
        compiler_params=_params("arbitrary", "arbitrary"),
        name="fox_sample",
    )(page_table.reshape(-1), q, k_new, v_new, f_new, b_fgate.reshape(nh, 1),
      *([cache_kt] * pps), *([cache_vt] * pps), *([cache_lft] * pps))


def _fill_group_queries(qbd_scr, q_ref, tnew, scale):
    qbd_scr[...] = jnp.zeros(qbd_scr.shape, F32)
    for h in range(NSA_HEADS):
        g = h // NSA_REP
        qbd_scr[h * tnew:(h + 1) * tnew, g * HEAD_DIM:(g + 1) * HEAD_DIM] = (
            q_ref[0, :, h * HEAD_DIM:(h + 1) * HEAD_DIM] * scale)


def _write_group_heads(o_ref, o, gate_ref, branch, tnew):
    for h in range(NSA_HEADS):
        g = h // NSA_REP
        c = h * NSA_BRANCHES + branch
        o_ref[0, :, h * HEAD_DIM:(h + 1) * HEAD_DIM] = (
            o[h * tnew:(h + 1) * tnew, g * HEAD_DIM:(g + 1) * HEAD_DIM] * gate_ref[0, :, c:c + 1])


def _sel_sample_body(pt_ref, q_ref, kn_ref, vn_ref, selrows_ref, gate_ref, *rest, n_steps, pps, tnew, scale):
    del pt_ref
    kt_refs, vt_refs = rest[:pps], rest[pps:2 * pps]
    o_ref, qbd_scr, m_scr, l_scr, acc_scr = rest[2 * pps:]
    p = pl.program_id(1)
    rows = NSA_HEADS * tnew
    nselp = selrows_ref.shape[-1]
    keys = pps * PAGE

    @pl.when(p == 0)
    def _():
        _fill_group_queries(qbd_scr, q_ref, tnew, scale)
        _init_softmax(m_scr, l_scr, acc_scr)

    blk_of_lane = (p * keys + lax.broadcasted_iota(jnp.int32, (nselp, keys), 1)) // SEL_BLOCK
    expand = (lax.broadcasted_iota(jnp.int32, (nselp, keys), 0) == blk_of_lane).astype(BF16)
    vis = _mm(selrows_ref[0].astype(BF16), expand) > 0.5
    kt = jnp.concatenate([r[0].reshape(NSA_KV_WIDTH, PAGE).astype(BF16) for r in kt_refs], axis=1)
    vt = jnp.concatenate([r[0].reshape(NSA_KV_WIDTH, PAGE).astype(BF16) for r in vt_refs], axis=1)
    s = jnp.where(vis, _mm(qbd_scr[...].astype(BF16), kt), NEG_INF)
    _softmax_step(s, vis, m_scr, l_scr, acc_scr, lambda pr: _nt(pr.astype(BF16), vt))

    @pl.when(p == n_steps - 1)
    def _():
        row_t = lax.broadcasted_iota(jnp.int32, (rows, 1), 0) % tnew
        o = _new_rows_step(qbd_scr[...], kn_ref[0], vn_ref[0], row_t, None, m_scr, l_scr, acc_scr, tnew)
        _write_group_heads(o_ref, o, gate_ref, 1, tnew)


def _sel_sample(q, k_new, v_new, selrows, gates, cache_kt, cache_vt, page_table):
    bx, tnew, wq = q.shape
    n_pages = page_table.shape[1]
    rows = NSA_HEADS * tnew
    wk = NSA_KV_WIDTH
    pps = _pages_per_step(n_pages, ATTN_PAGES_PER_STEP)
    fixed = lambda b, p, pt: (b, 0, 0)
    kv_specs = [pl.BlockSpec((1, NSA_GROUPS, HEAD_DIM, PAGE), _page_map(n_pages, pps, i, 4)) for i in range(pps)]
    return pl.pallas_call(
        functools.partial(_sel_sample_body, n_steps=n_pages // pps, pps=pps, tnew=tnew, scale=HEAD_DIM ** -0.5),
        grid_spec=pltpu.PrefetchScalarGridSpec(
            num_scalar_prefetch=1, grid=(bx, n_pages // pps),
            in_specs=[pl.BlockSpec((1, tnew, wq), fixed), pl.BlockSpec((1, tnew, wk), fixed),
                      pl.BlockSpec((1, tnew, wk), fixed), pl.BlockSpec((1, rows, selrows.shape[-1]), fixed),
                      pl.BlockSpec((1, tnew, LANE), fixed)] + kv_specs + kv_specs,
            out_specs=pl.BlockSpec((1, tnew, wq), fixed),
            scratch_shapes=[pltpu.VMEM((rows, wk), F32), pltpu.VMEM((rows, 1), F32), pltpu.VMEM((rows, 1), F32),
                            pltpu.VMEM((rows, wk), F32)]),
        out_shape=jax.ShapeDtypeStruct((bx, tnew, wq), F32),
        compiler_params=_params("arbitrary", "arbitrary"),
        name="sel_sample",
    )(page_table.reshape(-1), q, k_new, v_new, selrows, gates, *([cache_kt] * pps), *([cache_vt] * pps))


def _win_sample_body(q_ref, kn_ref, vn_ref, gate_ref, kt_ref, vt_ref, o_ref, qbd_scr, m_scr, l_scr, acc_scr,
                     *, tnew, wbuf, scale):
    rows = NSA_HEADS * tnew
    _fill_group_queries(qbd_scr, q_ref, tnew, scale)
    _init_softmax(m_scr, l_scr, acc_scr)
    row_t = lax.broadcasted_iota(jnp.int32, (rows, 1), 0) % tnew
    vis = lax.broadcasted_iota(jnp.int32, (rows, wbuf), 1) > row_t + (wbuf - WINDOW)
    kt = kt_ref[0].reshape(NSA_KV_WIDTH, wbuf).astype(BF16)
    vt = vt_ref[0].reshape(NSA_KV_WIDTH, wbuf).astype(BF16)
    s = jnp.where(vis, _mm(qbd_scr[...].astype(BF16), kt), NEG_INF)
    _softmax_step(s, vis, m_scr, l_scr, acc_scr, lambda pr: _nt(pr.astype(BF16), vt))
    o = _new_rows_step(qbd_scr[...], kn_ref[0], vn_ref[0], row_t, None, m_scr, l_scr, acc_scr, tnew)
    _write_group_heads(o_ref, o, gate_ref, 2, tnew)


def _win_sample(q, k_new, v_new, gates, buf_kt, buf_vt):
    bx, tnew, wq = q.shape
    wbuf = buf_kt.shape[-1]
    rows = NSA_HEADS * tnew
    wk = NSA_KV_WIDTH
    fixed = lambda b: (b, 0, 0)
    buf = pl.BlockSpec((1, NSA_GROUPS, HEAD_DIM, wbuf), lambda b: (b, 0, 0, 0))
    return pl.pallas_call(
        functools.partial(_win_sample_body, tnew=tnew, wbuf=wbuf, scale=HEAD_DIM ** -0.5),
        grid=(bx,),
        in_specs=[pl.BlockSpec((1, tnew, wq), fixed), pl.BlockSpec((1, tnew, wk), fixed),
                  pl.BlockSpec((1, tnew, wk), fixed), pl.BlockSpec((1, tnew, LANE), fixed), buf, buf],
        out_specs=pl.BlockSpec((1, tnew, wq), fixed),
        out_shape=jax.ShapeDtypeStruct((bx, tnew, wq), F32),
        scratch_shapes=[pltpu.VMEM((rows, wk), F32), pltpu.VMEM((rows, 1), F32), pltpu.VMEM((rows, 1), F32),
                        pltpu.VMEM((rows, wk), F32)],
        compiler_params=_params("arbitrary"),
        name="win_sample",
    )(q, k_new, v_new, gates, buf_kt, buf_vt)


def _tile_heads(v, n):
    return jnp.tile(v.astype(F32), n)


def _mix_ab(x, mods, pos0, past, p, tiles):
    y, f_t = _project(x, p["norm_mix"], mods["sc1"], mods["sh1"], p["w_in_main"], tiles["tm"], w_t=p["w_in_f_t"])
    gains = jnp.concatenate([_tile_heads(p["ab_q_norm"], FOX_HEADS), _tile_heads(p["ab_k_norm"], FOX_HEADS)])
    n_tiles = 2 * FOX_WIDTH // LANE
    (qk,) = _headnorm(y, 2 * FOX_WIDTH, gains.reshape(1, -1), [(c, 0, c) for c in range(n_tiles)],
                      [2 * FOX_WIDTH], tiles["tm_norm"])
    return y, f_t, qk


def _nsa_project(x, mods, pos_rows, p, tiles):
    n_gate = NSA_BRANCHES * NSA_HEADS
    y, gl = _project(x, p["norm_mix"], mods["sc1"], mods["sh1"], p["w_in_main"], tiles["tm"], w_side=p["w_in_gate"])
    kn3 = p["nsa_k_norm"]
    qw = NSA_HEADS * HEAD_DIM
    gains = jnp.concatenate([
        _tile_heads(p["nsa_q_norm"], NSA_HEADS),
        _tile_heads(kn3[0], NSA_GROUPS), jnp.ones((NSA_KV_WIDTH,), F32),
        _tile_heads(kn3[1], NSA_GROUPS), jnp.ones((NSA_KV_WIDTH,), F32),
        _tile_heads(kn3[2], NSA_GROUPS)])
    width = gains.shape[0]
    qt = qw // LANE
    kt = NSA_KV_WIDTH // LANE
    tile_map = [(c, 0, c) for c in range(qt)]
    for i in range(NSA_BRANCHES):
        tile_map += [(qt + 2 * i * kt + c, 1, i * kt + c) for c in range(kt)]
    b_gate = jnp.pad(p["nsa_b_gate"], (0, LANE - n_gate)).reshape(1, LANE)
    qn, kn, gates = _headnorm(y, width, gains.reshape(1, -1), tile_map, [qw, NSA_BRANCHES * NSA_KV_WIDTH],
                              tiles["tm_norm"], rope_tabs=_rope_tables(pos_rows), gate_logits=gl, gate_bias=b_gate)
    return y, qn, kn, gates


def _layer_params(params, layer):
    e = layer // 2
    p = {"norm_mix": params["norm_mix"][layer], "norm_ffn": params["norm_ffn"][layer],
         "w_up": params["w_up"][layer].astype(BF16), "w_down": params["w_down"][layer].astype(BF16)}
    prefix = "ab_" if layer % 2 == 0 else "nsa_"
    for k, v in params.items():
        if k.startswith(prefix):
            p[k] = v[e]
    if layer % 2 == 0:
        w_in = p["ab_w_in"]
        split_f = 3 * FOX_WIDTH
        p["w_in_main"] = jnp.concatenate([w_in[:, :split_f], w_in[:, split_f + FOX_HEADS:]], axis=1).astype(BF16)
        p["w_in_f_t"] = w_in[:, split_f:split_f + FOX_HEADS].T.astype(BF16)
        p["w_out"] = p["ab_w_out"].astype(BF16)
    else:
        w_in = p["nsa_w_in"]
        n_main = NSA_HEADS * HEAD_DIM + 6 * NSA_KV_WIDTH
        n_gate = NSA_BRANCHES * NSA_HEADS
        p["w_in_main"] = w_in[:, :n_main].astype(BF16)
        p["w_in_gate"] = jnp.pad(w_in[:, n_main:], ((0, 0), (0, LANE - n_gate))).astype(BF16)
        p["w_out"] = p["nsa_w_out"].astype(BF16)
    return p


def _heads(a, n):
    return a.reshape(a.shape[0], a.shape[1], n, HEAD_DIM)


def _prompt_trunk(x, mod, layers):
    bx, t, d = x.shape
    tiles = {"tm": 512, "tm_norm": 256}
    states = {}
    for layer, p in enumerate(layers):
        sh1, sc1, g1, sh2, sc2, g2 = [m[:, None, :] for m in jnp.split(mod[layer], 6, axis=-1)]
        mods = {"sc1": sc1, "sh1": sh1}
        if layer % 2 == 0:
            y, f_t, qk = _mix_ab(x, mods, 0, None, p, tiles)
            lf_t, cum = _logf_cumsum(f_t, p["ab_b_fgate"])
            o_fox = _fox_flash(qk, y, 2, cum, 512)
            o_pool = _pool_mixer(y, 3, jnp.zeros((bx, POOL_HALO, POOL_WIDTH), F32), p["ab_pool_map"],
                                 p["ab_pool_scale"], 0, 512)
            x = _out_project([[o_fox], [o_pool]], p["w_out"], x, g1, tiles["tm"])
            states["fox_k"] = _heads(qk[:, :, FOX_WIDTH:], FOX_HEADS)
            states["fox_v"] = _heads(y[:, :, 2 * FOX_WIDTH:3 * FOX_WIDTH], FOX_HEADS)
            states["fox_logf"] = lf_t.transpose(0, 2, 1)
            states["pool"] = y[:, t - (POOL_HALO - 1):, 3 * FOX_WIDTH:]
        else:
            y, qn, kn, gates = _nsa_project(x, mods, jnp.arange(t), p, tiles)
            qw, kw_ = NSA_HEADS * HEAD_DIM, NSA_KV_WIDTH
            n_chunk = t // CMP_STRIDE
            kcmp = _compress(_chunk_rows(kn, 0, t), p["nsa_cmp_pos_k"], p["nsa_cmp_w1_k"], p["nsa_cmp_w2_k"])
            vcmp = _compress(_chunk_rows(y, (qw + kw_) // kw_, t), p["nsa_cmp_pos_v"], p["nsa_cmp_w1_v"],
                             p["nsa_cmp_w2_v"])
            n_sel = -(-t // SEL_BLOCK)
            o_cmp, sel_t = _cmp_select_t(qn, kcmp, vcmp, gates, n_chunk - 1, n_sel, 0, 256)
            o_sel = _nsa_flash(qn, kn, 1, y, (qw + 3 * kw_) // kw_, gates, 1, 256, "causal", sel_t=sel_t)
            o_win = _nsa_flash(qn, kn, 2, y, (qw + 5 * kw_) // kw_, gates, 2, 256, "window")
            x = _out_project([[o_cmp, o_sel, o_win]], p["w_out"], x, g1, tiles["tm"])
            buf = min(WINDOW, t)
            states["nsa_kc"] = _heads(kn[:, :, :kw_], NSA_GROUPS)
            states["nsa_vc"] = _heads(y[:, :, qw + kw_:qw + 2 * kw_], NSA_GROUPS)
            states["nsa_ks"] = _heads(kn[:, :, kw_:2 * kw_], NSA_GROUPS)
            states["nsa_vs"] = _heads(y[:, :, qw + 3 * kw_:qw + 4 * kw_], NSA_GROUPS)
            states["nsa_kw"] = _heads(kn[:, t - buf:, 2 * kw_:], NSA_GROUPS)
            states["nsa_vw"] = _heads(y[:, t - buf:, qw + 5 * kw_:qw + 6 * kw_], NSA_GROUPS)
        x = _ffn(x, p["norm_ffn"], sc2, sh2, g2, p["w_up"], p["w_down"], tiles["tm"])
    return x, states


def _sample_trunk(x, mod, layers, past, page_table):
    bx, tnew, d = x.shape
    rows = bx * tnew
    n_pages = page_table.shape[1]
    pos0 = n_pages * PAGE
    assert tnew < CMP_STRIDE and pos0 % CMP_STRIDE == 0 and pos0 >= WINDOW
    tiles = {"tm": rows, "tm_norm": rows}
    xf = x.reshape(1, rows, d)
    per_batch = lambda a: a.reshape(bx, tnew, a.shape[-1])
    states = {}
    for layer, p in enumerate(layers):
        e = layer // 2
        sh1, sc1, g1, sh2, sc2, g2 = [jnp.repeat(m, tnew, axis=0)[None] for m in jnp.split(mod[layer], 6, axis=-1)]
        mods = {"sc1": sc1, "sh1": sh1}
        if layer % 2 == 0:
            y, f_t, qk = _mix_ab(xf, mods, pos0, None, p, tiles)
            q_s, k_s = per_batch(qk[0, :, :FOX_WIDTH]), per_batch(qk[0, :, FOX_WIDTH:])
            v_s = per_batch(y[0, :, 2 * FOX_WIDTH:3 * FOX_WIDTH])
            u_s = per_batch(y[0, :, 3 * FOX_WIDTH:])
            f_new = f_t[0].reshape(FOX_HEADS, bx, tnew).transpose(1, 0, 2)
            cache_kt = past["cache_fox_k"][e].transpose(0, 2, 3, 1)
            cache_vt = past["cache_fox_v"][e].transpose(0, 2, 3, 1)
            cache_lft = past["cache_fox_logf"][e].transpose(0, 2, 1)
            o_fox, lf_new = _fox_sample(q_s, k_s, v_s, f_new, p["ab_b_fgate"], cache_kt, cache_vt, cache_lft,
                                        page_table)
            pool_prev = past["state_pool"][e]
            prefix = jnp.pad(pool_prev, ((0, 0), (1, 0), (0, 0)))
            o_pool = _pool_mixer(u_s, 0, prefix, p["ab_pool_map"], p["ab_pool_scale"], pos0, tnew)
            xf = _out_project([[o_fox.reshape(1, rows, -1)], [o_pool.reshape(1, rows, -1)]], p["w_out"], xf, g1, rows)
            states["fox_k"] = _heads(k_s, FOX_HEADS)
            states["fox_v"] = _heads(v_s, FOX_HEADS)
            states["fox_logf"] = lf_new.transpose(0, 2, 1)
            states["pool"] = jnp.concatenate([pool_prev, u_s], axis=1)[:, -(POOL_HALO - 1):]
        else:
            pos_rows = pos0 + jnp.arange(rows) % tnew
            y, qn, kn, gates = _nsa_project(xf, mods, pos_rows, p, tiles)
            qw, kw_ = NSA_HEADS * HEAD_DIM, NSA_KV_WIDTH
            q_s, gates_s = per_batch(qn[0]), per_batch(gates[0])
            kn_s, y_s = per_batch(kn[0]), per_batch(y[0])
            kc_s, ks_s, kwn_s = kn_s[..., :kw_], kn_s[..., kw_:2 * kw_], kn_s[..., 2 * kw_:]
            vc_s, vs_s, vwn_s = (y_s[..., qw + kw_:qw + 2 * kw_], y_s[..., qw + 3 * kw_:qw + 4 * kw_],
                                 y_s[..., qw + 5 * kw_:qw + 6 * kw_])
            d_major = lambda a: a.transpose(0, 2, 3, 1)
            kcmp = _compress_pages(d_major(past["cache_nsa_kc"][e]), page_table, p["nsa_cmp_pos_k"],
                                   p["nsa_cmp_w1_k"], p["nsa_cmp_w2_k"])
            vcmp = _compress_pages(d_major(past["cache_nsa_vc"][e]), page_table, p["nsa_cmp_pos_v"],
                                   p["nsa_cmp_w1_v"], p["nsa_cmp_w2_v"])
            total = pos0 + tnew
            n_cmp = total // CMP_STRIDE - 1
            n_sel = -(-total // SEL_BLOCK)
            nselp = -(-n_sel // (2 * LANE)) * (2 * LANE)
            o_cmp, sel = _cmp_select(q_s, kcmp, vcmp, gates_s, n_cmp, n_sel, nselp, pos0)
            selrows = jnp.repeat(sel.reshape(bx, NSA_GROUPS, tnew, nselp), NSA_REP, axis=1).reshape(
                bx, NSA_HEADS * tnew, nselp)
            o_sel = _sel_sample(q_s, ks_s, vs_s, selrows, gates_s, d_major(past["cache_nsa_ks"][e]),
                                d_major(past["cache_nsa_vs"][e]), page_table)
            kw_prev, vw_prev = past["state_nsa_kw"][e], past["state_nsa_vw"][e]
            o_win = _win_sample(q_s, kwn_s, vwn_s, gates_s, d_major(kw_prev), d_major(vw_prev))
            flat = lambda a: a.reshape(1, rows, -1)
            xf = _out_project([[flat(o_cmp), flat(o_sel), flat(o_win)]], p["w_out"], xf, g1, rows)
            buf = kw_prev.shape[1]
            states["nsa_kc"] = _heads(kc_s, NSA_GROUPS)
            states["nsa_vc"] = _heads(vc_s, NSA_GROUPS)
            states["nsa_ks"] = _heads(ks_s, NSA_GROUPS)
            states["nsa_vs"] = _heads(vs_s, NSA_GROUPS)
            states["nsa_kw"] = jnp.concatenate([kw_prev, _heads(kwn_s, NSA_GROUPS)], axis=1)[:, -buf:]
            states["nsa_vw"] = jnp.concatenate([vw_prev, _heads(vwn_s, NSA_GROUPS)], axis=1)[:, -buf:]
        xf = _ffn(xf, p["norm_ffn"], sc2, sh2, g2, p["w_up"], p["w_down"], rows)
    return xf.reshape(bx, tnew, d), states


_STATE_NAMES = ("fox_k", "fox_v", "fox_logf", "pool", "nsa_kc", "nsa_vc", "nsa_ks", "nsa_vs", "nsa_kw", "nsa_vw")


def kernel(x_prompt, x_sample, cache_fox_k, cache_fox_v, cache_fox_logf, state_pool, cache_nsa_kc, cache_nsa_vc,
           cache_nsa_ks, cache_nsa_vs, state_nsa_kw, state_nsa_vw, page_table, c_prompt, c_sample, w_mod, b_mod,
           norm_mix, norm_ffn, w_up, w_down, ab_w_in, ab_b_fgate, ab_q_norm, ab_k_norm, ab_pool_map, ab_pool_scale,
           ab_w_out, nsa_w_in, nsa_b_gate, nsa_q_norm, nsa_k_norm, nsa_cmp_pos_k, nsa_cmp_w1_k, nsa_cmp_w2_k,
           nsa_cmp_pos_v, nsa_cmp_w1_v, nsa_cmp_w2_v, nsa_w_out):
    assert w_mod.shape[0] == 2, "one forgetting/pooling layer followed by one sparse-attention layer"
    params = {
        "norm_mix": norm_mix, "norm_ffn": norm_ffn, "w_up": w_up, "w_down": w_down,
        "ab_w_in": ab_w_in, "ab_b_fgate": ab_b_fgate, "ab_q_norm": ab_q_norm, "ab_k_norm": ab_k_norm,
        "ab_pool_map": ab_pool_map, "ab_pool_scale": ab_pool_scale, "ab_w_out": ab_w_out,
        "nsa_w_in": nsa_w_in, "nsa_b_gate": nsa_b_gate, "nsa_q_norm": nsa_q_norm, "nsa_k_norm": nsa_k_norm,
        "nsa_cmp_pos_k": nsa_cmp_pos_k, "nsa_cmp_w1_k": nsa_cmp_w1_k, "nsa_cmp_w2_k": nsa_cmp_w2_k,
        "nsa_cmp_pos_v": nsa_cmp_pos_v, "nsa_cmp_w1_v": nsa_cmp_w1_v, "nsa_cmp_w2_v": nsa_cmp_w2_v,
        "nsa_w_out": nsa_w_out,
    }
    past = {
        "cache_fox_k": cache_fox_k, "cache_fox_v": cache_fox_v, "cache_fox_logf": cache_fox_logf,
        "state_pool": state_pool, "cache_nsa_kc": cache_nsa_kc, "cache_nsa_vc": cache_nsa_vc,
        "cache_nsa_ks": cache_nsa_ks, "cache_nsa_vs": cache_nsa_vs,
        "state_nsa_kw": state_nsa_kw, "state_nsa_vw": state_nsa_vw,
    }
    n_prompt = c_prompt.shape[0]
    mod = _modulation(jnp.concatenate([c_prompt, c_sample], axis=0), w_mod, b_mod)
    layers = [_layer_params(params, layer) for layer in range(w_mod.shape[0])]
    y_prompt, sp = _prompt_trunk(x_prompt, mod[:, :n_prompt], layers)
    y_sample, ss = _sample_trunk(x_sample, mod[:, n_prompt:], layers, past, page_table)
    return (y_prompt, y_sample, *[sp[n][None] for n in _STATE_NAMES], *[ss[n][None] for n in _STATE_NAMES])
```

```python
import functools

import jax
import jax.numpy as jnp
from jax import lax
from jax.experimental import pallas as pl
from jax.experimental.pallas import tpu as pltpu

F32 = jnp.float32
BF16 = jnp.bfloat16

D_MODEL = 1024
HEAD_DIM = 64
EPS = 1e-6
ROPE_THETA = 10000.0
NEG_INF = -1e30
MASKED = -2e30
BIG = 1e9
LOG2E = 1.4426950408889634
PAGE = 128
FOX_HEADS = 8
FOX_WIDTH = FOX_HEADS * HEAD_DIM
POOL_WINDOWS = (2, 4, 8, 16)
POOL_WIDTH = 512
POOL_GROUP_CH = 128
POOL_HALO = 16
NSA_HEADS = 16
NSA_GROUPS = 4
NSA_REP = NSA_HEADS // NSA_GROUPS
NSA_KV_WIDTH = NSA_GROUPS * HEAD_DIM
NSA_BRANCHES = 3
CMP_BLOCK = 32
CMP_STRIDE = 16
SEL_BLOCK = 64
SEL_TOPK = 16
WINDOW = 512
D_FF = 2816
LANE = 128
SUBLANE = 8
GROUPS_PER_TILE = LANE // HEAD_DIM
PAGES_PER_STEP = 8
ATTN_PAGES_PER_STEP = 16
VMEM_LIMIT = 48 * 1024 * 1024


def _params(*sem):
    return pltpu.CompilerParams(dimension_semantics=sem, vmem_limit_bytes=VMEM_LIMIT)


def _nt(a, b):
    return lax.dot_general(a, b, (((1,), (1,)), ((), ())), preferred_element_type=F32)


def _mm(a, b):
    return jnp.dot(a, b, preferred_element_type=F32)


def _mod_body(c_ref, w_ref, b_ref, o_ref):
    c = c_ref[...]
    a = (c * jax.nn.sigmoid(c)).astype(BF16)
    o_ref[0] = _mm(a, w_ref[0].astype(BF16)) + b_ref[0]


def _modulation(c_all, w_mod, b_mod):
    n_layers, d, n = w_mod.shape
    rows = c_all.shape[0]
    tn = 1536
    return pl.pallas_call(
        _mod_body,
        grid=(n_layers, n // tn),
        in_specs=[
            pl.BlockSpec((rows, d), lambda l, j: (0, 0)),
            pl.BlockSpec((1, d, tn), lambda l, j: (l, 0, j)),
            pl.BlockSpec((1, 1, tn), lambda l, j: (l, 0, j)),
        ],
        out_specs=pl.BlockSpec((1, rows, tn), lambda l, j: (l, 0, j)),
        out_shape=jax.ShapeDtypeStruct((n_layers, rows, n), F32),
        compiler_params=_params("arbitrary", "arbitrary"),
        name="modulation",
    )(c_all, w_mod, b_mod.reshape(n_layers, 1, n))


def _modulated_norm(x, gain, sc, sh):
    xn = x * lax.rsqrt(jnp.mean(x * x, axis=-1, keepdims=True) + EPS) * gain
    return xn * (1.0 + sc) + sh


def _proj_body(*refs, has_side, has_t):
    x_ref, g_ref, sc_ref, sh_ref, w_ref = refs[:5]
    k = 5
    side_ref = t_ref = None
    if has_side:
        side_ref = refs[k]; k += 1
    if has_t:
        t_ref = refs[k]; k += 1
    y_ref = refs[k]; k += 1
    hb = _modulated_norm(x_ref[0], g_ref[...], sc_ref[0], sh_ref[0]).astype(BF16)
    y_ref[0] = _mm(hb, w_ref[...])
    if has_side:
        refs[k][0] = _mm(hb, side_ref[...]); k += 1
    if has_t:
        refs[k][0] = _nt(t_ref[...], hb)


def _resident(shape):
    return pl.BlockSpec(shape, lambda *_: (0,) * len(shape), pipeline_mode=pl.Buffered(1))


def _project(x, gain, sc, sh, w, tm, w_side=None, w_t=None):
    bx, tx, d = x.shape
    n_cols = w.shape[1]
    r = sc.shape[1]
    mod_spec = (pl.BlockSpec((1, tm, d), lambda b, i: (b, i, 0)) if r == tx
                else pl.BlockSpec((1, 1, d), lambda b, i: (b, 0, 0)))
    in_specs = [pl.BlockSpec((1, tm, d), lambda b, i: (b, i, 0)), _resident((1, d)), mod_spec, mod_spec,
                _resident((d, n_cols))]
    args = [x, gain.reshape(1, d), sc, sh, w]
    out_specs = [pl.BlockSpec((1, tm, n_cols), lambda b, i: (b, i, 0))]
    out_shape = [jax.ShapeDtypeStruct((bx, tx, n_cols), F32)]
    if w_side is not None:
        ns = w_side.shape[1]
        in_specs.append(_resident((d, ns)))
        args.append(w_side)
        out_specs.append(pl.BlockSpec((1, tm, ns), lambda b, i: (b, i, 0)))
        out_shape.append(jax.ShapeDtypeStruct((bx, tx, ns), F32))
    if w_t is not None:
        nt = w_t.shape[0]
        in_specs.append(_resident((nt, d)))
        args.append(w_t)
        out_specs.append(pl.BlockSpec((1, nt, tm), lambda b, i: (b, 0, i)))
        out_shape.append(jax.ShapeDtypeStruct((bx, nt, tx), F32))
    return pl.pallas_call(
        functools.partial(_proj_body, has_side=w_side is not None, has_t=w_t is not None),
        grid=(bx, tx // tm),
        in_specs=in_specs, out_specs=out_specs, out_shape=out_shape,
        compiler_params=_params("arbitrary", "arbitrary"),
        name="project",
    )(*args)


def _headnorm_body(*refs, tiles, rope, has_gate, n_out):
    y_ref, gain_ref = refs[:2]
    k = 2
    cos_ref = sin_ref = gl_ref = bg_ref = None
    if rope:
        cos_ref, sin_ref = refs[k], refs[k + 1]; k += 2
    if has_gate:
        gl_ref, bg_ref = refs[k], refs[k + 1]; k += 2
    outs = refs[k:k + n_out]
    gate_out = refs[k + n_out] if has_gate else None
    lane = lax.broadcasted_iota(jnp.int32, (1, LANE), 1)
    low_head = lane < HEAD_DIM
    first_half = (lane % HEAD_DIM) < (HEAD_DIM // 2)
    for src, oi, dst in tiles:
        y = y_ref[0, :, src * LANE:(src + 1) * LANE]
        y2 = y * y
        s_lo = jnp.sum(jnp.where(low_head, y2, 0.0), axis=-1, keepdims=True)
        s_hi = jnp.sum(jnp.where(low_head, 0.0, y2), axis=-1, keepdims=True)
        ms = jnp.where(low_head, s_lo, s_hi) * (1.0 / HEAD_DIM)
        yn = y * lax.rsqrt(ms + EPS) * gain_ref[:, src * LANE:(src + 1) * LANE]
        if rope:
            partner = jnp.where(first_half, pltpu.roll(yn, LANE - HEAD_DIM // 2, 1), pltpu.roll(yn, HEAD_DIM // 2, 1))
            yn = yn * cos_ref[...] + partner * sin_ref[...]
        outs[oi][0, :, dst * LANE:(dst + 1) * LANE] = yn
    if has_gate:
        gate_out[0] = jax.nn.sigmoid(gl_ref[0] + bg_ref[...])


def _headnorm(y, width, gains, tiles, out_widths, tm, rope_tabs=None, gate_logits=None, gate_bias=None):
    bx, tx, _ = y.shape
    rope = rope_tabs is not None
    has_gate = gate_logits is not None
    in_specs = [pl.BlockSpec((1, tm, width), lambda b, i: (b, i, 0)),
                pl.BlockSpec((1, width), lambda b, i: (0, 0))]
    args = [y, gains]
    if rope:
        in_specs += [pl.BlockSpec((tm, LANE), lambda b, i: (i, 0))] * 2
        args += list(rope_tabs)
    if has_gate:
        in_specs += [pl.BlockSpec((1, tm, LANE), lambda b, i: (b, i, 0)), pl.BlockSpec((1, LANE), lambda b, i: (0, 0))]
        args += [gate_logits, gate_bias]
    out_specs = [pl.BlockSpec((1, tm, w), lambda b, i: (b, i, 0)) for w in out_widths]
    out_shape = [jax.ShapeDtypeStruct((bx, tx, w), F32) for w in out_widths]
    if has_gate:
        out_specs.append(pl.BlockSpec((1, tm, LANE), lambda b, i: (b, i, 0)))
        out_shape.append(jax.ShapeDtypeStruct((bx, tx, LANE), F32))
    return pl.pallas_call(
        functools.partial(_headnorm_body, tiles=tuple(tiles), rope=rope, has_gate=has_gate, n_out=len(out_widths)),
        grid=(bx, tx // tm),
        in_specs=in_specs, out_specs=out_specs, out_shape=out_shape,
        compiler_params=_params("arbitrary", "arbitrary"),
        name="headnorm",
    )(*args)


def _rope_tables(pos):
    half = HEAD_DIM // 2
    inv_freq = ROPE_THETA ** (-jnp.arange(half, dtype=F32) / half)
    ang = pos.astype(F32)[:, None] * inv_freq[None, :]
    cos, sin = jnp.cos(ang), jnp.sin(ang)
    reps = LANE // HEAD_DIM
    return (jnp.tile(jnp.concatenate([cos, cos], axis=1), (1, reps)),
            jnp.tile(jnp.concatenate([-sin, sin], axis=1), (1, reps)))


def _lane_cumsum(x):
    n = x.shape[-1]
    lane = lax.broadcasted_iota(jnp.int32, x.shape, x.ndim - 1)
    s = 1
    while s < n:
        x = x + jnp.where(lane >= s, pltpu.roll(x, s, x.ndim - 1), 0.0)
        s *= 2
    return x


def _logf_body(f_ref, b_ref, lf_ref, cum_ref):
    lf = jax.nn.log_sigmoid(f_ref[0] + b_ref[...])
    lf_ref[0] = lf
    cum_ref[0] = _lane_cumsum(lf)


def _logf_cumsum(f_t, b_fgate):
    bx, h, t = f_t.shape
    spec = pl.BlockSpec((1, h, t), lambda b: (b, 0, 0))
    return pl.pallas_call(
        _logf_body, grid=(bx,),
        in_specs=[spec, pl.BlockSpec((h, 1), lambda b: (0, 0))],
        out_specs=[spec, spec],
        out_shape=[jax.ShapeDtypeStruct(f_t.shape, F32)] * 2,
        compiler_params=_params("arbitrary"),
        name="logf_cumsum",
    )(f_t, b_fgate.reshape(h, 1))


def _half_mask(shape, half):
    lane = lax.broadcasted_iota(jnp.int32, shape, len(shape) - 1)
    return (lane % LANE) // HEAD_DIM == half


def _fox_flash_body(q_ref, k_ref, v_ref, cum_ref, o_ref, qm_scr, m_scr, l_scr, acc_scr, *, tq, nk, scale):
    qi, ki = pl.program_id(1), pl.program_id(2)
    n_pairs = FOX_HEADS // 2
    reps = tq // LANE

    @pl.when(ki == 0)
    def _():
        for h in range(FOX_HEADS):
            c = h // 2
            qpair = q_ref[0, :, c * LANE:(c + 1) * LANE] * scale
            qm_scr[h] = jnp.where(_half_mask(qpair.shape, h % 2), qpair, 0.0).astype(BF16)
        m_scr[...] = jnp.full(m_scr.shape, NEG_INF, F32)
        l_scr[...] = jnp.zeros(l_scr.shape, F32)
        acc_scr[...] = jnp.zeros(acc_scr.shape, F32)

    def update(diagonal):
        if diagonal:
            vis = (lax.broadcasted_iota(jnp.int32, (tq, tq), 1) <= lax.broadcasted_iota(jnp.int32, (tq, tq), 0))
        for c in range(n_pairs):
            k_pair = k_ref[0, :, c * LANE:(c + 1) * LANE].astype(BF16)
            v_pair = v_ref[0, :, c * LANE:(c + 1) * LANE]
            pv = []
            alphas = []
            for half in range(2):
                h = 2 * c + half
                s = _nt(qm_scr[h], k_pair) - cum_ref[0, h:h + 1, :] * LOG2E
                if diagonal:
                    s = jnp.where(vis, s, MASKED)
                m_prev = m_scr[h]
                m_new = jnp.maximum(m_prev, jnp.max(s, axis=1, keepdims=True))
                alpha = jnp.exp2(m_prev - m_new)
                p = jnp.exp2(s - jnp.concatenate([m_new] * reps, axis=1))
                l_scr[h] = alpha * l_scr[h] + jnp.sum(p, axis=1, keepdims=True)
                m_scr[h] = m_new
                v_half = jnp.where(_half_mask(v_pair.shape, half), v_pair, 0.0).astype(BF16)
                pv.append(_mm(p.astype(BF16), v_half))
                alphas.append(alpha)
            alpha_pair = jnp.where(_half_mask(alphas[0].shape, 0), alphas[0], alphas[1])
            acc_scr[c] = alpha_pair * acc_scr[c] + (pv[0] + pv[1])

    pl.when(ki < qi)(functools.partial(update, False))
    pl.when(ki == qi)(functools.partial(update, True))

    @pl.when(ki == nk - 1)
    def _():
        for c in range(n_pairs):
            l_pair = jnp.where(_half_mask((tq, LANE), 0), l_scr[2 * c], l_scr[2 * c + 1])
            o_ref[0, :, c * LANE:(c + 1) * LANE] = acc_scr[c] / l_pair


def _fox_flash(qk, v_arr, vcb, cum, tq):
    bx, t, _ = qk.shape
    w = FOX_WIDTH
    nq = t // tq
    kmap = lambda qi, ki: jnp.minimum(ki, qi)
    return pl.pallas_call(
        functools.partial(_fox_flash_body, tq=tq, nk=nq, scale=HEAD_DIM ** -0.5 * LOG2E),
        grid=(bx, nq, nq),
        in_specs=[pl.BlockSpec((1, tq, w), lambda b, qi, ki: (b, qi, 0)),
                  pl.BlockSpec((1, tq, w), lambda b, qi, ki: (b, kmap(qi, ki), 1)),
                  pl.BlockSpec((1, tq, w), lambda b, qi, ki: (b, kmap(qi, ki), vcb)),
                  pl.BlockSpec((1, FOX_HEADS, tq), lambda b, qi, ki: (b, 0, kmap(qi, ki)))],
        out_specs=pl.BlockSpec((1, tq, w), lambda b, qi, ki: (b, qi, 0)),
        out_shape=jax.ShapeDtypeStruct((bx, t, w), F32),
        scratch_shapes=[pltpu.VMEM((FOX_HEADS, tq, LANE), BF16), pltpu.VMEM((FOX_HEADS, tq, LANE), F32),
                        pltpu.VMEM((FOX_HEADS, tq, LANE), F32), pltpu.VMEM((FOX_HEADS // 2, tq, LANE), F32)],
        compiler_params=_params("arbitrary", "arbitrary", "arbitrary"),
        name="fox_flash",
    )(qk, qk, v_arr, cum)


def _nsa_flash_body(*refs, tq, nk, mode, has_sel, gate_branch, scale):
    q_ref, k_ref, v_ref = refs[:3]
    i = 3
    sel_ref = None
    if has_sel:
        sel_ref = refs[i]; i += 1
    gate_ref, o_ref, qt_scr, m_scr, l_scr, acc_scr = refs[i:i + 6]
    qi, ki = pl.program_id(1), pl.program_id(2)
    kt = ki if mode == "causal" else qi - (nk - 1) + ki
    active = (ki <= qi) if mode == "causal" else (kt >= 0)

    @pl.when(ki == 0)
    def _():
        zeros = jnp.zeros((HEAD_DIM, tq), F32)
        for pair in range(NSA_HEADS // 2):
            q_pair_t = (q_ref[0, :, pair * LANE:(pair + 1) * LANE] * scale).T
            for half in range(2):
                g, r = divmod(2 * pair + half, NSA_REP)
                q_t = q_pair_t[half * HEAD_DIM:(half + 1) * HEAD_DIM]
                parts = [q_t, zeros] if g % GROUPS_PER_TILE == 0 else [zeros, q_t]
                qt_scr[g, 0:LANE, r * tq:(r + 1) * tq] = jnp.concatenate(parts, axis=0).astype(BF16)
        if has_sel:
            nsel = sel_ref.shape[2]
            for g in range(NSA_GROUPS):
                bias = jnp.where(sel_ref[0, g] > 0.5, 0.0, MASKED)
                bias = jnp.concatenate([bias, jnp.zeros((LANE - nsel, tq), F32)], axis=0).astype(BF16)
                qt_scr[g, LANE:2 * LANE, :] = jnp.concatenate([bias] * NSA_REP, axis=1)
        m_scr[...] = jnp.full(m_scr.shape, NEG_INF, F32)
        l_scr[...] = jnp.zeros(l_scr.shape, F32)
        acc_scr[...] = jnp.zeros(acc_scr.shape, F32)

    def update(mask_kind):
        key = lax.broadcasted_iota(jnp.int32, (tq, tq), 0)
        query = lax.broadcasted_iota(jnp.int32, (tq, tq), 1)
        if mask_kind == "causal":
            vis = key <= query
        elif mask_kind == "tail":
            vis = key + (WINDOW - (nk - 1) * tq) > query
        if has_sel:
            kblk = (kt * tq + lax.broadcasted_iota(jnp.int32, (tq, LANE), 0)) // SEL_BLOCK
            expand = (kblk == lax.broadcasted_iota(jnp.int32, (tq, LANE), 1)).astype(BF16)
        logits, v_ts = [], []
        for c in range(NSA_GROUPS // GROUPS_PER_TILE):
            k_pair = k_ref[0, :, c * LANE:(c + 1) * LANE].astype(BF16)
            v_pair_t = v_ref[0, :, c * LANE:(c + 1) * LANE].T
            if has_sel:
                k_pair = jnp.concatenate([k_pair, expand], axis=1)
            for gg in range(GROUPS_PER_TILE):
                g = c * GROUPS_PER_TILE + gg
                v_ts.append(v_pair_t[gg * HEAD_DIM:(gg + 1) * HEAD_DIM].astype(BF16))
                s = _mm(k_pair, qt_scr[g])
                if mask_kind is not None:
                    s = jnp.where(jnp.concatenate([vis] * NSA_REP, axis=1), s, MASKED)
                logits.append(s)
        for g in range(NSA_GROUPS):
            s = logits[g]
            m_prev = m_scr[g]
            m_new = jnp.maximum(m_prev, jnp.max(s, axis=0, keepdims=True))
            alpha = jnp.exp2(m_prev - m_new)
            p = jnp.exp2(s - m_new[0:1])
            l_scr[g] = alpha * l_scr[g] + jnp.sum(p, axis=0, keepdims=True)
            acc_scr[g] = alpha[0:1] * acc_scr[g] + _mm(v_ts[g], p.astype(BF16))
            m_scr[g] = m_new

    last = ki == nk - 1
    if mode == "causal":
        pl.when(ki < qi)(functools.partial(update, None))
        pl.when(ki == qi)(functools.partial(update, "causal"))
    else:
        assert nk >= 2 and WINDOW % tq == 0
        pl.when(active & (ki == 0))(functools.partial(update, "tail"))
        if nk > 2:
            pl.when(active & (ki > 0) & jnp.logical_not(last))(functools.partial(update, None))
        pl.when(last)(functools.partial(update, "causal"))

    @pl.when(ki == nk - 1)
    def _():
        low = _half_mask((tq, LANE), 0)
        for pair in range(NSA_HEADS // 2):
            parts, gate_cols = [], []
            for half in range(2):
                h = 2 * pair + half
                g, r = divmod(h, NSA_REP)
                lanes = slice(r * tq, (r + 1) * tq)
                parts.append(acc_scr[g, :, lanes] / l_scr[g, 0:1, lanes])
                c = h * NSA_BRANCHES + gate_branch
                gate_cols.append(gate_ref[0, :, c:c + 1])
            o_pair = jnp.concatenate(parts, axis=0).T
            o_ref[0, :, pair * LANE:(pair + 1) * LANE] = o_pair * jnp.where(low, gate_cols[0], gate_cols[1])


def _nsa_flash(qn, kn, kcb, y, vcb, gates, gate_branch, tq, mode, sel_t=None):
    bx, t, wq = qn.shape
    wk = NSA_KV_WIDTH
    nq = t // tq
    nk = nq if mode == "causal" else WINDOW // tq + 1
    if mode == "causal":
        kmap = lambda qi, ki: jnp.minimum(ki, qi)
    else:
        kmap = lambda qi, ki: jnp.maximum(qi - (nk - 1) + ki, 0)
    in_specs = [pl.BlockSpec((1, tq, wq), lambda b, qi, ki: (b, qi, 0)),
                pl.BlockSpec((1, tq, wk), lambda b, qi, ki: (b, kmap(qi, ki), kcb)),
                pl.BlockSpec((1, tq, wk), lambda b, qi, ki: (b, kmap(qi, ki), vcb))]
    args = [qn, kn, y]
    if sel_t is not None:
        in_specs.append(pl.BlockSpec((1, NSA_GROUPS, sel_t.shape[2], tq), lambda b, qi, ki: (b, 0, 0, qi)))
        args.append(sel_t)
    in_specs.append(pl.BlockSpec((1, tq, LANE), lambda b, qi, ki: (b, qi, 0)))
    args.append(gates)
    return pl.pallas_call(
        functools.partial(_nsa_flash_body, tq=tq, nk=nk, mode=mode, has_sel=sel_t is not None,
                          gate_branch=gate_branch, scale=HEAD_DIM ** -0.5 * LOG2E),
        grid=(bx, nq, nk),
        in_specs=in_specs,
        out_specs=pl.BlockSpec((1, tq, wq), lambda b, qi, ki: (b, qi, 0)),
        out_shape=jax.ShapeDtypeStruct((bx, t, wq), F32),
        scratch_shapes=[pltpu.VMEM((NSA_GROUPS, (2 if sel_t is not None else 1) * LANE, NSA_REP * tq), BF16),
                        pltpu.VMEM((NSA_GROUPS, SUBLANE, NSA_REP * tq), F32),
                        pltpu.VMEM((NSA_GROUPS, SUBLANE, NSA_REP * tq), F32),
                        pltpu.VMEM((NSA_GROUPS, HEAD_DIM, NSA_REP * tq), F32)],
        compiler_params=_params("arbitrary", "arbitrary", "arbitrary"),
        name="nsa_flash_" + mode,
    )(*args)


def _pool_body(u_ref, pre_ref, map_ref, scale_ref, o_ref, ext_scr, *, tm, pos0):
    j = pl.program_id(1)

    @pl.when(j == 0)
    def _():
        ext_scr[0:POOL_HALO] = pre_ref[0]

    @pl.when(j > 0)
    def _():
        ext_scr[0:POOL_HALO] = ext_scr[tm:tm + POOL_HALO]

    ext_scr[POOL_HALO:POOL_HALO + tm] = u_ref[0]
    qpos = pos0 + j * tm + lax.broadcasted_iota(jnp.int32, (tm, 1), 0)
    for g, w in enumerate(POOL_WINDOWS):
        lo, hi = g * POOL_GROUP_CH, (g + 1) * POOL_GROUP_CH
        u_new = ext_scr[POOL_HALO:POOL_HALO + tm, lo:hi]
        tot = u_new
        for d in range(1, w):
            tot = tot + ext_scr[POOL_HALO - d:POOL_HALO - d + tm, lo:hi]
        count = jnp.minimum(w, qpos + 1).astype(F32)
        diff = tot / count - u_new
        y = _mm(diff.astype(BF16), map_ref[g].astype(BF16))
        o_ref[0, :, lo:hi] = y * scale_ref[:, lo:hi]


def _pool_mixer(u_arr, ucb, prefix, w_map, scale, pos0, tm):
    bx, t, _ = u_arr.shape
    c = POOL_WIDTH
    return pl.pallas_call(
        functools.partial(_pool_body, tm=tm, pos0=pos0),
        grid=(bx, t // tm),
        in_specs=[pl.BlockSpec((1, tm, c), lambda b, j: (b, j, ucb)),
                  pl.BlockSpec((1, POOL_HALO, c), lambda b, j: (b, 0, 0)),
                  pl.BlockSpec((len(POOL_WINDOWS), POOL_GROUP_CH, POOL_GROUP_CH), lambda b, j: (0, 0, 0)),
                  pl.BlockSpec((1, c), lambda b, j: (0, 0))],
        out_specs=pl.BlockSpec((1, tm, c), lambda b, j: (b, j, 0)),
        out_shape=jax.ShapeDtypeStruct((bx, t, c), F32),
        scratch_shapes=[pltpu.VMEM((POOL_HALO + tm, c), F32)],
        compiler_params=_params("arbitrary", "arbitrary"),
        name="pool_mixer",
    )(u_arr, prefix, w_map, scale.reshape(1, c))


def _outproj_body(*refs, group_sizes):
    n_a = sum(group_sizes)
    a_refs = refs[:n_a]
    w_refs = refs[n_a:n_a + len(group_sizes)]
    res_ref, gate_ref, o_ref = refs[n_a + len(group_sizes):]
    y = None
    k = 0
    for gi, n in enumerate(group_sizes):
        a = a_refs[k][0]
        for r in a_refs[k + 1:k + n]:
            a = a + r[0]
        k += n
        part = _mm(a.astype(BF16), w_refs[gi][...])
        y = part if y is None else y + part
    o_ref[0] = res_ref[0] + gate_ref[0] * y


def _out_project(groups, w, res, gate, tm):
    bx, tx, d = res.shape
    kg = groups[0][0].shape[-1]
    r = gate.shape[1]
    row = pl.BlockSpec((1, tm, d), lambda b, i: (b, i, 0))
    a_spec = pl.BlockSpec((1, tm, kg), lambda b, i: (b, i, 0))
    in_specs, args = [], []
    for grp in groups:
        for a in grp:
            in_specs.append(a_spec); args.append(a)
    for gi in range(len(groups)):
        in_specs.append(pl.BlockSpec((kg, d), lambda b, i, gi=gi: (gi, 0), pipeline_mode=pl.Buffered(1)))
        args.append(w)
    in_specs += [row, row if r == tx else pl.BlockSpec((1, 1, d), lambda b, i: (b, 0, 0))]
    args += [res, gate]
    return pl.pallas_call(
        functools.partial(_outproj_body, group_sizes=tuple(len(g) for g in groups)),
        grid=(bx, tx // tm),
        in_specs=in_specs,
        out_specs=row,
        out_shape=jax.ShapeDtypeStruct((bx, tx, d), F32),
        compiler_params=_params("arbitrary", "arbitrary"),
        name="out_project",
    )(*args)


FFN_SLICES = 2


def _ffn_body(x_ref, g_ref, sc_ref, sh_ref, gate_ref, wg_ref, wu_ref, wd_ref, o_ref):
    x = x_ref[0]
    h = _modulated_norm(x, g_ref[...], sc_ref[0], sh_ref[0]).astype(BF16)
    width = D_FF // FFN_SLICES
    y = None
    for c in range(FFN_SLICES):
        cols = slice(c * width, (c + 1) * width)
        gt = _mm(h, wg_ref[:, cols])
        up = _mm(h, wu_ref[:, cols])
        act = ((gt * jax.nn.sigmoid(gt)) * up).astype(BF16)
        part = _mm(act, wd_ref[cols, :])
        y = part if y is None else y + part
    o_ref[0] = x + gate_ref[0] * y


def _ffn(x, gain, sc, sh, gate, w_up, w_down, tm):
    bx, tx, d = x.shape
    r = sc.shape[1]
    mod_spec = (pl.BlockSpec((1, tm, d), lambda b, i: (b, i, 0)) if r == tx
                else pl.BlockSpec((1, 1, d), lambda b, i: (b, 0, 0)))
    x_spec = pl.BlockSpec((1, tm, d), lambda b, i: (b, i, 0))
    half = lambda j: pl.BlockSpec((d, D_FF), lambda b, i: (0, j), pipeline_mode=pl.Buffered(1))
    return pl.pallas_call(
        _ffn_body,
        grid=(bx, tx // tm),
        in_specs=[x_spec, _resident((1, d)), mod_spec, mod_spec, mod_spec, half(0), half(1), _resident((D_FF, d))],
        out_specs=x_spec,
        out_shape=jax.ShapeDtypeStruct(x.shape, F32),
        compiler_params=_params("arbitrary", "arbitrary"),
        name="swiglu",
    )(x, gain.reshape(1, d), sc, sh, gate, w_up, w_up, w_down)


def _chunk_rows_body(a_ref, o_ref, *, n_chunks):
    for s in range(CMP_STRIDE):
        rows = a_ref[0, pl.ds(s, n_chunks, stride=CMP_STRIDE), :]
        for g in range(GROUPS_PER_TILE):
            o_ref[0, g, :, s * HEAD_DIM:(s + 1) * HEAD_DIM] = rows[:, g * HEAD_DIM:(g + 1) * HEAD_DIM]


def _chunk_rows(a, acb, tm):
    bx, t, _ = a.shape
    n_chunks = tm // CMP_STRIDE
    tiles = NSA_KV_WIDTH // LANE
    return pl.pallas_call(
        functools.partial(_chunk_rows_body, n_chunks=n_chunks),
        grid=(bx, t // tm, tiles),
        in_specs=[pl.BlockSpec((1, tm, LANE), lambda b, i, c: (b, i, acb * tiles + c))],
        out_specs=pl.BlockSpec((1, GROUPS_PER_TILE, n_chunks, CMP_STRIDE * HEAD_DIM), lambda b, i, c: (b, c, i, 0)),
        out_shape=jax.ShapeDtypeStruct((bx, NSA_GROUPS, t // CMP_STRIDE, CMP_STRIDE * HEAD_DIM), F32),
        compiler_params=_params("arbitrary", "arbitrary", "arbitrary"),
        name="chunk_rows",
    )(a)


def _compress_mlp(a, pos_ref, w1_ref, w2_ref):
    half = CMP_STRIDE * HEAD_DIM
    w1 = w1_ref[...].astype(BF16)
    first = _mm(a, w1[:half])
    second = _mm(a, w1[half:])
    bias = _mm(pos_ref[...].astype(BF16), w1)
    hidden = (first + pltpu.roll(second, a.shape[0] - 1, 0)) + bias
    return _mm(jax.nn.gelu(hidden).astype(BF16), w2_ref[...].astype(BF16))


def _compress_body(a_ref, pos_ref, w1_ref, w2_ref, o_ref):
    o_ref[0, 0] = _compress_mlp(a_ref[0, 0].astype(BF16), pos_ref, w1_ref, w2_ref)


def _compress(chunks, pos_emb, w1, w2):
    bx, g, n_chunks, half = chunks.shape
    hidden = w1.shape[1]
    return pl.pallas_call(
        _compress_body,
        grid=(bx, g),
        in_specs=[pl.BlockSpec((1, 1, n_chunks, half), lambda b, i: (b, i, 0, 0)),
                  pl.BlockSpec((1, 2 * half), lambda b, i: (0, 0)),
                  pl.BlockSpec((2 * half, hidden), lambda b, i: (0, 0)),
                  pl.BlockSpec((hidden, HEAD_DIM), lambda b, i: (0, 0))],
        out_specs=pl.BlockSpec((1, 1, n_chunks, HEAD_DIM), lambda b, i: (b, i, 0, 0)),
        out_shape=jax.ShapeDtypeStruct((bx, g, n_chunks, HEAD_DIM), F32),
        compiler_params=_params("arbitrary", "arbitrary"),
        name="compress",
    )(chunks, pos_emb.reshape(1, 2 * half), w1, w2)


def _compress_pages_body(pt_ref, *rest, n_steps, pps):
    del pt_ref
    page_refs = rest[:pps]
    pos_ref, w1_ref, w2_ref, o_ref, tok_scr, chunk_scr = rest[pps:]
    p = pl.program_id(1)
    cpp = PAGE // CMP_STRIDE
    pairs = NSA_GROUPS // GROUPS_PER_TILE
    for i in range(pps):
        row0 = pl.multiple_of((p * pps + i) * cpp, cpp)
        for c in range(pairs):
            tok = tok_scr.at[i * pairs + c]
            tok[...] = page_refs[i][0, c * GROUPS_PER_TILE:(c + 1) * GROUPS_PER_TILE].reshape(LANE, PAGE).T
            for s in range(CMP_STRIDE):
                rows = tok[pl.ds(s, cpp, stride=CMP_STRIDE), :]
                for g in range(GROUPS_PER_TILE):
                    chunk_scr[c * GROUPS_PER_TILE + g, pl.ds(row0, cpp), s * HEAD_DIM:(s + 1) * HEAD_DIM] = (
                        rows[:, g * HEAD_DIM:(g + 1) * HEAD_DIM])

    @pl.when(p == n_steps - 1)
    def _():
        for g in range(NSA_GROUPS):
            o_ref[0, :, g * HEAD_DIM:(g + 1) * HEAD_DIM] = _compress_mlp(
                chunk_scr[g].astype(BF16), pos_ref, w1_ref, w2_ref)


def _compress_pages(cache_t, page_table, pos_emb, w1, w2):
    bx, n_pages = page_table.shape
    pps = _pages_per_step(n_pages)
    cpp = PAGE // CMP_STRIDE
    n_chunks = n_pages * cpp
    half = CMP_STRIDE * HEAD_DIM
    hidden = w1.shape[1]
    const = lambda b, p, pt: (0, 0)
    return pl.pallas_call(
        functools.partial(_compress_pages_body, n_steps=n_pages // pps, pps=pps),
        grid_spec=pltpu.PrefetchScalarGridSpec(
            num_scalar_prefetch=1, grid=(bx, n_pages // pps),
            in_specs=[pl.BlockSpec((1, NSA_GROUPS, HEAD_DIM, PAGE), _page_map(n_pages, pps, i, 4)) for i in range(pps)]
            + [pl.BlockSpec((1, 2 * half), const), pl.BlockSpec((2 * half, hidden), const),
               pl.BlockSpec((hidden, HEAD_DIM), const)],
            out_specs=pl.BlockSpec((1, n_chunks, NSA_KV_WIDTH), lambda b, p, pt: (b, 0, 0)),
            scratch_shapes=[pltpu.VMEM((pps * NSA_GROUPS // GROUPS_PER_TILE, PAGE, LANE), F32),
                            pltpu.VMEM((NSA_GROUPS, n_chunks, half), F32)]),
        out_shape=jax.ShapeDtypeStruct((bx, n_chunks, NSA_KV_WIDTH), F32),
        compiler_params=_params("arbitrary", "arbitrary"),
        name="compress_pages",
    )(page_table.reshape(-1), *([cache_t] * pps), pos_emb.reshape(1, 2 * half), w1, w2)


def _bf16_terms(x):
    hi = x.astype(BF16)
    r1 = x - hi.astype(F32)
    mid = r1.astype(BF16)
    lo = (r1 - mid.astype(F32)).astype(BF16)
    return hi, mid, lo


def _cmp_select_body(q_ref, kc_ref, vc_ref, gate_ref, o_ref, sel_ref, qbd_scr, *, tq, n_cmp, n_sel, pos0, scale):
    ncp = kc_ref.shape[1]
    nselp = sel_ref.shape[-1]
    gt = NSA_GROUPS * tq
    rows = NSA_REP * gt
    qbd_scr[...] = jnp.zeros(qbd_scr.shape, F32)
    for h in range(NSA_HEADS):
        g, r = divmod(h, NSA_REP)
        qbd_scr[r * gt + g * tq:r * gt + (g + 1) * tq, g * HEAD_DIM:(g + 1) * HEAD_DIM] = (
            q_ref[0, :, h * HEAD_DIM:(h + 1) * HEAD_DIM] * scale)
    row_pos = pos0 + lax.broadcasted_iota(jnp.int32, (rows, 1), 0) % tq
    c_idx = lax.broadcasted_iota(jnp.int32, (1, ncp), 1)
    c_valid = (c_idx * CMP_STRIDE + (CMP_BLOCK - 1) <= row_pos) & (c_idx < n_cmp)
    s = jnp.where(c_valid, _nt(qbd_scr[...].astype(BF16), kc_ref[0].astype(BF16)), NEG_INF)
    m = jnp.max(s, axis=-1, keepdims=True)
    p = jnp.where(c_valid, jnp.exp(s - m), 0.0)
    l = jnp.sum(p, axis=-1, keepdims=True)
    pc = jnp.where(l > 0.0, p / jnp.where(l > 0.0, l, 1.0), 0.0)
    o = _mm(pc.astype(BF16), vc_ref[0].astype(BF16))
    for h in range(NSA_HEADS):
        g, r = divmod(h, NSA_REP)
        c = h * NSA_BRANCHES
        o_ref[0, :, h * HEAD_DIM:(h + 1) * HEAD_DIM] = (
            o[r * gt + g * tq:r * gt + (g + 1) * tq, g * HEAD_DIM:(g + 1) * HEAD_DIM] * gate_ref[0, :, c:c + 1])
    pc_sum = pc[0:gt]
    for r in range(1, NSA_REP):
        pc_sum = pc_sum + pc[r * gt:(r + 1) * gt]
    cj = lax.broadcasted_iota(jnp.int32, (ncp, nselp), 0) * CMP_STRIDE
    sj = lax.broadcasted_iota(jnp.int32, (ncp, nselp), 1) * SEL_BLOCK
    overlap = ((cj < sj + SEL_BLOCK) & (cj + (CMP_BLOCK - 1) >= sj)).astype(BF16)
    terms = _mm(jnp.concatenate(_bf16_terms(pc_sum), axis=0), overlap)
    imp = (terms[0:gt] + terms[gt:2 * gt]) + terms[2 * gt:3 * gt]
    gpos = pos0 + lax.broadcasted_iota(jnp.int32, (gt, 1), 0) % tq
    j_idx = lax.broadcasted_iota(jnp.int32, (1, nselp), 1)
    forced = (j_idx == 0) | (j_idx == gpos // SEL_BLOCK)
    valid = j_idx * SEL_BLOCK <= gpos
    score = jnp.where(valid, jnp.where(forced, BIG, imp), -BIG)
    rank = jnp.zeros(score.shape, jnp.int32)
    for i in range(n_sel):
        si = score[:, i:i + 1]
        ahead = (si > score) | ((si == score) & (i < j_idx))
        rank = rank + ahead.astype(jnp.int32)
    sel_ref[0] = ((rank < SEL_TOPK) & valid).astype(F32)


def _cmp_select(qn, kcmp, vcmp, gates, n_cmp, n_sel, nselp, pos0):
    bx, t, wq = qn.shape
    ncp = kcmp.shape[1]
    assert (NSA_GROUPS * t) % (2 * SUBLANE) == 0
    whole = lambda b: (b, 0, 0)
    cmp_spec = pl.BlockSpec((1, ncp, NSA_KV_WIDTH), whole)
    return pl.pallas_call(
        functools.partial(_cmp_select_body, tq=t, n_cmp=n_cmp, n_sel=n_sel, pos0=pos0, scale=HEAD_DIM ** -0.5),
        grid=(bx,),
        in_specs=[pl.BlockSpec((1, t, wq), whole), cmp_spec, cmp_spec, pl.BlockSpec((1, t, LANE), whole)],
        out_specs=[pl.BlockSpec((1, t, wq), whole), pl.BlockSpec((1, NSA_GROUPS * t, nselp), whole)],
        out_shape=[jax.ShapeDtypeStruct((bx, t, wq), F32),
                   jax.ShapeDtypeStruct((bx, NSA_GROUPS * t, nselp), F32)],
        scratch_shapes=[pltpu.VMEM((NSA_HEADS * t, NSA_KV_WIDTH), F32)],
        compiler_params=_params("arbitrary"),
        name="cmp_select",
    )(qn, kcmp, vcmp, gates)


def _cmp_select_t_body(q_ref, kc_ref, vc_ref, gate_ref, o_ref, sel_ref, *, tq, n_cmp, n_sel, pos0, scale):
    g, qi = pl.program_id(1), pl.program_id(2)
    ncp = kc_ref.shape[2]
    nselp = sel_ref.shape[2]
    qpos = pos0 + qi * tq + lax.broadcasted_iota(jnp.int32, (1, tq), 1)
    c_idx = lax.broadcasted_iota(jnp.int32, (ncp, 1), 0)
    c_valid = (c_idx * CMP_STRIDE + (CMP_BLOCK - 1) <= qpos) & (c_idx < n_cmp)
    kc = kc_ref[0, 0].astype(BF16)
    vc_t = vc_ref[0, 0].T.astype(BF16)
    lane = lax.broadcasted_iota(jnp.int32, (1, LANE), 1)
    q_t = jnp.concatenate([(q_ref[0, :, p * LANE:(p + 1) * LANE] * scale).T
                           for p in range(NSA_REP // GROUPS_PER_TILE)], axis=0).astype(BF16)
    q_wide = jnp.concatenate([q_t[r * HEAD_DIM:(r + 1) * HEAD_DIM] for r in range(NSA_REP)], axis=1)
    valid_w = jnp.concatenate([c_valid] * NSA_REP, axis=1)
    s = jnp.where(valid_w, _mm(kc, q_wide), NEG_INF)
    m = jnp.max(s, axis=0, keepdims=True)
    p = jnp.where(valid_w, jnp.exp(s - m), 0.0)
    l = jnp.sum(p, axis=0, keepdims=True)
    pc = jnp.where(l > 0.0, p / jnp.where(l > 0.0, l, 1.0), 0.0)
    o_t = _mm(vc_t, pc.astype(BF16))
    pc_sum = pc[:, 0:tq]
    for r in range(1, NSA_REP):
        pc_sum = pc_sum + pc[:, r * tq:(r + 1) * tq]
    low = _half_mask((tq, LANE), 0)
    for pair in range(NSA_REP // GROUPS_PER_TILE):
        gate_cols = []
        for half in range(GROUPS_PER_TILE):
            col = (g * NSA_REP + pair * GROUPS_PER_TILE + half) * NSA_BRANCHES
            gate_cols.append(jnp.sum(jnp.where(lane == col, gate_ref[0], 0.0), axis=-1, keepdims=True))
        r0 = pair * GROUPS_PER_TILE
        o_pair = jnp.concatenate([o_t[:, r0 * tq:(r0 + 1) * tq], o_t[:, (r0 + 1) * tq:(r0 + 2) * tq]], axis=0).T
        o_ref[0, :, pair * LANE:(pair + 1) * LANE] = o_pair * jnp.where(low, gate_cols[0], gate_cols[1])
    sj = lax.broadcasted_iota(jnp.int32, (nselp, ncp), 0) * SEL_BLOCK
    cj = lax.broadcasted_iota(jnp.int32, (nselp, ncp), 1) * CMP_STRIDE
    overlap_t = ((cj < sj + SEL_BLOCK) & (cj + (CMP_BLOCK - 1) >= sj)).astype(F32)
    imp = jnp.dot(overlap_t, pc_sum, preferred_element_type=F32, precision=lax.Precision.HIGHEST)
    j_idx = lax.broadcasted_iota(jnp.int32, (nselp, 1), 0)
    forced = (j_idx == 0) | (j_idx == qpos // SEL_BLOCK)
    valid = j_idx * SEL_BLOCK <= qpos
    score = jnp.where(valid, jnp.where(forced, BIG, imp), -BIG)
    rank = jnp.zeros((nselp, tq), jnp.int32)
    for i in range(n_sel):
        si = score[i:i + 1, :]
        ahead = (si > score) | ((si == score) & (i < j_idx))
        rank = rank + ahead.astype(jnp.int32)
    sel_ref[0, 0] = ((rank < SEL_TOPK) & valid).astype(F32)


def _cmp_select_t(qn, kcmp, vcmp, gates, n_cmp, n_sel, pos0, tq):
    bx, t, _ = qn.shape
    ncp = kcmp.shape[2]
    gw = NSA_REP * HEAD_DIM
    nselp = -(-n_sel // SUBLANE) * SUBLANE
    return pl.pallas_call(
        functools.partial(_cmp_select_t_body, tq=tq, n_cmp=n_cmp, n_sel=n_sel, pos0=pos0, scale=HEAD_DIM ** -0.5),
        grid=(bx, NSA_GROUPS, t // tq),
        in_specs=[pl.BlockSpec((1, tq, gw), lambda b, g, i: (b, i, g)),
                  pl.BlockSpec((1, 1, ncp, HEAD_DIM), lambda b, g, i: (b, g, 0, 0)),
                  pl.BlockSpec((1, 1, ncp, HEAD_DIM), lambda b, g, i: (b, g, 0, 0)),
                  pl.BlockSpec((1, tq, LANE), lambda b, g, i: (b, i, 0))],
        out_specs=[pl.BlockSpec((1, tq, gw), lambda b, g, i: (b, i, g)),
                   pl.BlockSpec((1, 1, nselp, tq), lambda b, g, i: (b, g, 0, i))],
        out_shape=[jax.ShapeDtypeStruct((bx, t, NSA_HEADS * HEAD_DIM), F32),
                   jax.ShapeDtypeStruct((bx, NSA_GROUPS, nselp, t), F32)],
        compiler_params=_params("arbitrary", "arbitrary", "arbitrary"),
        name="cmp_select_t",
    )(qn, kcmp, vcmp, gates)


def _pages_per_step(n_pages, want=PAGES_PER_STEP):
    pps = min(want, n_pages)
    assert n_pages % pps == 0
    return pps


def _page_map(n_pages, pps, i, rank):
    return lambda b, p, pt: (pt[b * n_pages + p * pps + i],) + (0,) * (rank - 1)


def _softmax_step(s, vis, m_scr, l_scr, acc_scr, pv_fn):
    m_prev = m_scr[...]
    m_new = jnp.maximum(m_prev, jnp.max(s, axis=-1, keepdims=True))
    alpha = jnp.exp(m_prev - m_new)
    p = jnp.exp(s - m_new)
    if vis is not None:
        p = jnp.where(vis, p, 0.0)
    l_scr[...] = alpha * l_scr[...] + jnp.sum(p, axis=-1, keepdims=True)
    acc_scr[...] = alpha * acc_scr[...] + pv_fn(p)
    m_scr[...] = m_new


def _new_rows_step(qbd, k_new, v_new, row_t, bias_cols, m_scr, l_scr, acc_scr, tnew):
    cols = []
    for j in range(tnew):
        sj = jnp.sum(qbd * k_new[j:j + 1, :], axis=-1, keepdims=True)
        if bias_cols is not None:
            sj = sj - bias_cols[j]
        cols.append(jnp.where(row_t >= j, sj, NEG_INF))
    m_prev = m_scr[...]
    m_new = m_prev
    for sj in cols:
        m_new = jnp.maximum(m_new, sj)
    alpha = jnp.exp(m_prev - m_new)
    l = alpha * l_scr[...]
    acc = alpha * acc_scr[...]
    for j, sj in enumerate(cols):
        pj = jnp.where(row_t >= j, jnp.exp(sj - m_new), 0.0)
        l = l + pj
        acc = acc + pj * v_new[j:j + 1, :]
    return acc / l


def _init_softmax(m_scr, l_scr, acc_scr):
    m_scr[...] = jnp.full(m_scr.shape, NEG_INF, F32)
    l_scr[...] = jnp.zeros(l_scr.shape, F32)
    acc_scr[...] = jnp.zeros(acc_scr.shape, F32)


def _fox_sample_body(pt_ref, q_ref, kn_ref, vn_ref, fn_ref, bf_ref, *rest, n_steps, pps, tnew, scale):
    del pt_ref
    kt_refs, vt_refs, lf_refs = rest[:pps], rest[pps:2 * pps], rest[2 * pps:3 * pps]
    o_ref, lfo_ref, qbd_scr, m_scr, l_scr, acc_scr, carry_scr = rest[3 * pps:]
    p = pl.program_id(1)
    nh, w = FOX_HEADS, FOX_WIDTH
    rows = tnew * nh
    head_of_lane = lax.broadcasted_iota(jnp.int32, (nh, w), 1) // HEAD_DIM
    hmask = (head_of_lane == lax.broadcasted_iota(jnp.int32, (nh, w), 0)).astype(F32)

    @pl.when(p == 0)
    def _():
        for t in range(tnew):
            qbd_scr[t * nh:(t + 1) * nh, :] = q_ref[0, t:t + 1, :] * scale * hmask
        _init_softmax(m_scr, l_scr, acc_scr)
        carry_scr[...] = jnp.zeros(carry_scr.shape, F32)

    tri = (lax.broadcasted_iota(jnp.int32, (PAGE, PAGE), 0) <= lax.broadcasted_iota(jnp.int32, (PAGE, PAGE), 1)
           ).astype(F32)
    carry = carry_scr[...]
    cums = []
    for i in range(pps):
        within = jnp.dot(lf_refs[i][0], tri, preferred_element_type=F32, precision=lax.Precision.HIGHEST)
        cums.append(carry + within)
        carry = carry + within[:, PAGE - 1:PAGE]
    carry_scr[...] = carry
    bias = jnp.concatenate([jnp.concatenate(cums, axis=1)] * tnew, axis=0)
    kt = jnp.concatenate([r[0].reshape(w, PAGE).astype(BF16) for r in kt_refs], axis=1)
    vt = jnp.concatenate([r[0].reshape(w, PAGE).astype(BF16) for r in vt_refs], axis=1)
    s = _mm(qbd_scr[...].astype(BF16), kt) - bias
    _softmax_step(s, None, m_scr, l_scr, acc_scr, lambda pr: _nt(pr.astype(BF16), vt))

    @pl.when(p == n_steps - 1)
    def _():
        lf_new = jax.nn.log_sigmoid(fn_ref[0] + bf_ref[...])
        lfo_ref[0] = lf_new
        run = carry_scr[...]
        bias_cols = []
        for j in range(tnew):
            run = run + lf_new[:, j:j + 1]
            bias_cols.append(jnp.concatenate([run] * tnew, axis=0))
        row_t = lax.broadcasted_iota(jnp.int32, (rows, 1), 0) // nh
        o = _new_rows_step(qbd_scr[...], kn_ref[0], vn_ref[0], row_t, bias_cols, m_scr, l_scr, acc_scr, tnew)
        o = o * jnp.concatenate([hmask] * tnew, axis=0)
        for t in range(tnew):
            o_ref[0, t:t + 1, :] = jnp.sum(o[t * nh:(t + 1) * nh], axis=0, keepdims=True)


def _fox_sample(q, k_new, v_new, f_new, b_fgate, cache_kt, cache_vt, cache_lft, page_table):
    bx, tnew, w = q.shape
    n_pages = page_table.shape[1]
    pps = _pages_per_step(n_pages, ATTN_PAGES_PER_STEP)
    nh = FOX_HEADS
    rows = tnew * nh
    new_spec = pl.BlockSpec((1, tnew, w), lambda b, p, pt: (b, 0, 0))
    kv_specs = [pl.BlockSpec((1, nh, HEAD_DIM, PAGE), _page_map(n_pages, pps, i, 4)) for i in range(pps)]
    lf_specs = [pl.BlockSpec((1, nh, PAGE), _page_map(n_pages, pps, i, 3)) for i in range(pps)]
    return pl.pallas_call(
        functools.partial(_fox_sample_body, n_steps=n_pages // pps, pps=pps, tnew=tnew, scale=HEAD_DIM ** -0.5),
        grid_spec=pltpu.PrefetchScalarGridSpec(
            num_scalar_prefetch=1, grid=(bx, n_pages // pps),
            in_specs=[new_spec, new_spec, new_spec,
                      pl.BlockSpec((1, nh, tnew), lambda b, p, pt: (b, 0, 0)),
                      pl.BlockSpec((nh, 1), lambda b, p, pt: (0, 0))] + kv_specs + kv_specs + lf_specs,
            out_specs=[new_spec, pl.BlockSpec((1, nh, tnew), lambda b, p, pt: (b, 0, 0))],
            scratch_shapes=[pltpu.VMEM((rows, w), F32), pltpu.VMEM((rows, 1), F32), pltpu.VMEM((rows, 1), F32),
                            pltpu.VMEM((rows, w), F32), pltpu.VMEM((nh, 1), F32)]),
        out_shape=[jax.ShapeDtypeStruct((bx, tnew, w), F32), jax.ShapeDtypeStruct((bx, nh, tnew), F32)],
        compiler_params=_params("arbitrary", "arbitrary"),
        name="fox_sample",
    )(page_table.reshape(-1), q, k_new, v_new, f_new, b_fgate.reshape(nh, 1),
      *([cache_kt] * pps), *([cache_vt] * pps), *([cache_lft] * pps))


def _fill_group_queries(qbd_scr, q_ref, tnew, scale):
    qbd_scr[...] = jnp.zeros(qbd_scr.shape, F32)
    for h in range(NSA_HEADS):
        g = h // NSA_REP
        qbd_scr[h * tnew:(h + 1) * tnew, g * HEAD_DIM:(g + 1) * HEAD_DIM] = (
            q_ref[0, :, h * HEAD_DIM:(h + 1) * HEAD_DIM] * scale)


def _write_group_heads(o_ref, o, gate_ref, branch, tnew):
    for h in range(NSA_HEADS):
        g = h // NSA_REP
        c = h * NSA_BRANCHES + branch
        o_ref[0, :, h * HEAD_DIM:(h + 1) * HEAD_DIM] = (
            o[h * tnew:(h + 1) * tnew, g * HEAD_DIM:(g + 1) * HEAD_DIM] * gate_ref[0, :, c:c + 1])


def _sel_sample_body(pt_ref, q_ref, kn_ref, vn_ref, selrows_ref, gate_ref, *rest, n_steps, pps, tnew, scale):
    del pt_ref
    kt_refs, vt_refs = rest[:pps], rest[pps:2 * pps]
    o_ref, qbd_scr, m_scr, l_scr, acc_scr = rest[2 * pps:]
    p = pl.program_id(1)
    rows = NSA_HEADS * tnew
    nselp = selrows_ref.shape[-1]
    keys = pps * PAGE

    @pl.when(p == 0)
    def _():
        _fill_group_queries(qbd_scr, q_ref, tnew, scale)
        _init_softmax(m_scr, l_scr, acc_scr)

    blk_of_lane = (p * keys + lax.broadcasted_iota(jnp.int32, (nselp, keys), 1)) // SEL_BLOCK
    expand = (lax.broadcasted_iota(jnp.int32, (nselp, keys), 0) == blk_of_lane).astype(BF16)
    vis = _mm(selrows_ref[0].astype(BF16), expand) > 0.5
    kt = jnp.concatenate([r[0].reshape(NSA_KV_WIDTH, PAGE).astype(BF16) for r in kt_refs], axis=1)
    vt = jnp.concatenate([r[0].reshape(NSA_KV_WIDTH, PAGE).astype(BF16) for r in vt_refs], axis=1)
    s = jnp.where(vis, _mm(qbd_scr[...].astype(BF16), kt), NEG_INF)
    _softmax_step(s, vis, m_scr, l_scr, acc_scr, lambda pr: _nt(pr.astype(BF16), vt))

    @pl.when(p == n_steps - 1)
    def _():
        row_t = lax.broadcasted_iota(jnp.int32, (rows, 1), 0) % tnew
        o = _new_rows_step(qbd_scr[...], kn_ref[0], vn_ref[0], row_t, None, m_scr, l_scr, acc_scr, tnew)
        _write_group_heads(o_ref, o, gate_ref, 1, tnew)


def _sel_sample(q, k_new, v_new, selrows, gates, cache_kt, cache_vt, page_table):
    bx, tnew, wq = q.shape
    n_pages = page_table.shape[1]
    rows = NSA_HEADS * tnew
    wk = NSA_KV_WIDTH
    pps = _pages_per_step(n_pages, ATTN_PAGES_PER_STEP)
    fixed = lambda b, p, pt: (b, 0, 0)
    kv_specs = [pl.BlockSpec((1, NSA_GROUPS, HEAD_DIM, PAGE), _page_map(n_pages, pps, i, 4)) for i in range(pps)]
    return pl.pallas_call(
        functools.partial(_sel_sample_body, n_steps=n_pages // pps, pps=pps, tnew=tnew, scale=HEAD_DIM ** -0.5),
        grid_spec=pltpu.PrefetchScalarGridSpec(
            num_scalar_prefetch=1, grid=(bx, n_pages // pps),
            in_specs=[pl.BlockSpec((1, tnew, wq), fixed), pl.BlockSpec((1, tnew, wk), fixed),
                      pl.BlockSpec((1, tnew, wk), fixed), pl.BlockSpec((1, rows, selrows.shape[-1]), fixed),
                      pl.BlockSpec((1, tnew, LANE), fixed)] + kv_specs + kv_specs,
            out_specs=pl.BlockSpec((1, tnew, wq), fixed),
            scratch_shapes=[pltpu.VMEM((rows, wk), F32), pltpu.VMEM((rows, 1), F32), pltpu.VMEM((rows, 1), F32),
                            pltpu.VMEM((rows, wk), F32)]),
        out_shape=jax.ShapeDtypeStruct((bx, tnew, wq), F32),
        compiler_params=_params("arbitrary", "arbitrary"),
        name="sel_sample",
    )(page_table.reshape(-1), q, k_new, v_new, selrows, gates, *([cache_kt] * pps), *([cache_vt] * pps))


def _win_sample_body(q_ref, kn_ref, vn_ref, gate_ref, kt_ref, vt_ref, o_ref, qbd_scr, m_scr, l_scr, acc_scr,
                     *, tnew, wbuf, scale):
    rows = NSA_HEADS * tnew
    _fill_group_queries(qbd_scr, q_ref, tnew, scale)
    _init_softmax(m_scr, l_scr, acc_scr)
    row_t = lax.broadcasted_iota(jnp.int32, (rows, 1), 0) % tnew
    vis = lax.broadcasted_iota(jnp.int32, (rows, wbuf), 1) > row_t + (wbuf - WINDOW)
    kt = kt_ref[0].reshape(NSA_KV_WIDTH, wbuf).astype(BF16)
    vt = vt_ref[0].reshape(NSA_KV_WIDTH, wbuf).astype(BF16)
    s = jnp.where(vis, _mm(qbd_scr[...].astype(BF16), kt), NEG_INF)
    _softmax_step(s, vis, m_scr, l_scr, acc_scr, lambda pr: _nt(pr.astype(BF16), vt))
    o = _new_rows_step(qbd_scr[...], kn_ref[0], vn_ref[0], row_t, None, m_scr, l_scr, acc_scr, tnew)
    _write_group_heads(o_ref, o, gate_ref, 2, tnew)


def _win_sample(q, k_new, v_new, gates, buf_kt, buf_vt):
    bx, tnew, wq = q.shape
    wbuf = buf_kt.shape[-1]
    rows = NSA_HEADS * tnew
    wk = NSA_KV_WIDTH
    fixed = lambda b: (b, 0, 0)
    buf = pl.BlockSpec((1, NSA_GROUPS, HEAD_DIM, wbuf), lambda b: (b, 0, 0, 0))
    return pl.pallas_call(
        functools.partial(_win_sample_body, tnew=tnew, wbuf=wbuf, scale=HEAD_DIM ** -0.5),
        grid=(bx,),
        in_specs=[pl.BlockSpec((1, tnew, wq), fixed), pl.BlockSpec((1, tnew, wk), fixed),
                  pl.BlockSpec((1, tnew, wk), fixed), pl.BlockSpec((1, tnew, LANE), fixed), buf, buf],
        out_specs=pl.BlockSpec((1, tnew, wq), fixed),
        out_shape=jax.ShapeDtypeStruct((bx, tnew, wq), F32),
        scratch_shapes=[pltpu.VMEM((rows, wk), F32), pltpu.VMEM((rows, 1), F32), pltpu.VMEM((rows, 1), F32),
                        pltpu.VMEM((rows, wk), F32)],
        compiler_params=_params("arbitrary"),
        name="win_sample",
    )(q, k_new, v_new, gates, buf_kt, buf_vt)


def _tile_heads(v, n):
    return jnp.tile(v.astype(F32), n)


def _mix_ab(x, mods, pos0, past, p, tiles):
    y, f_t = _project(x, p["norm_mix"], mods["sc1"], mods["sh1"], p["w_in_main"], tiles["tm"], w_t=p["w_in_f_t"])
    gains = jnp.concatenate([_tile_heads(p["ab_q_norm"], FOX_HEADS), _tile_heads(p["ab_k_norm"], FOX_HEADS)])
    n_tiles = 2 * FOX_WIDTH // LANE
    (qk,) = _headnorm(y, 2 * FOX_WIDTH, gains.reshape(1, -1), [(c, 0, c) for c in range(n_tiles)],
                      [2 * FOX_WIDTH], tiles["tm_norm"])
    return y, f_t, qk


def _nsa_project(x, mods, pos_rows, p, tiles):
    n_gate = NSA_BRANCHES * NSA_HEADS
    y, gl = _project(x, p["norm_mix"], mods["sc1"], mods["sh1"], p["w_in_main"], tiles["tm"], w_side=p["w_in_gate"])
    kn3 = p["nsa_k_norm"]
    qw = NSA_HEADS * HEAD_DIM
    gains = jnp.concatenate([
        _tile_heads(p["nsa_q_norm"], NSA_HEADS),
        _tile_heads(kn3[0], NSA_GROUPS), jnp.ones((NSA_KV_WIDTH,), F32),
        _tile_heads(kn3[1], NSA_GROUPS), jnp.ones((NSA_KV_WIDTH,), F32),
        _tile_heads(kn3[2], NSA_GROUPS)])
    width = gains.shape[0]
    qt = qw // LANE
    kt = NSA_KV_WIDTH // LANE
    tile_map = [(c, 0, c) for c in range(qt)]
    for i in range(NSA_BRANCHES):
        tile_map += [(qt + 2 * i * kt + c, 1, i * kt + c) for c in range(kt)]
    b_gate = jnp.pad(p["nsa_b_gate"], (0, LANE - n_gate)).reshape(1, LANE)
    qn, kn, gates = _headnorm(y, width, gains.reshape(1, -1), tile_map, [qw, NSA_BRANCHES * NSA_KV_WIDTH],
                              tiles["tm_norm"], rope_tabs=_rope_tables(pos_rows), gate_logits=gl, gate_bias=b_gate)
    return y, qn, kn, gates


def _layer_params(params, layer):
    e = layer // 2
    p = {"norm_mix": params["norm_mix"][layer], "norm_ffn": params["norm_ffn"][layer],
         "w_up": params["w_up"][layer].astype(BF16), "w_down": params["w_down"][layer].astype(BF16)}
    prefix = "ab_" if layer % 2 == 0 else "nsa_"
    for k, v in params.items():
        if k.startswith(prefix):
            p[k] = v[e]
    if layer % 2 == 0:
        w_in = p["ab_w_in"]
        split_f = 3 * FOX_WIDTH
        p["w_in_main"] = jnp.concatenate([w_in[:, :split_f], w_in[:, split_f + FOX_HEADS:]], axis=1).astype(BF16)
        p["w_in_f_t"] = w_in[:, split_f:split_f + FOX_HEADS].T.astype(BF16)
        p["w_out"] = p["ab_w_out"].astype(BF16)
    else:
        w_in = p["nsa_w_in"]
        n_main = NSA_HEADS * HEAD_DIM + 6 * NSA_KV_WIDTH
        n_gate = NSA_BRANCHES * NSA_HEADS
        p["w_in_main"] = w_in[:, :n_main].astype(BF16)
        p["w_in_gate"] = jnp.pad(w_in[:, n_main:], ((0, 0), (0, LANE - n_gate))).astype(BF16)
        p["w_out"] = p["nsa_w_out"].astype(BF16)
    return p


def _heads(a, n):
    return a.reshape(a.shape[0], a.shape[1], n, HEAD_DIM)


def _prompt_trunk(x, mod, layers):
    bx, t, d = x.shape
    tiles = {"tm": 512, "tm_norm": 256}
    states = {}
    for layer, p in enumerate(layers):
        sh1, sc1, g1, sh2, sc2, g2 = [m[:, None, :] for m in jnp.split(mod[layer], 6, axis=-1)]
        mods = {"sc1": sc1, "sh1": sh1}
        if layer % 2 == 0:
            y, f_t, qk = _mix_ab(x, mods, 0, None, p, tiles)
            lf_t, cum = _logf_cumsum(f_t, p["ab_b_fgate"])
            o_fox = _fox_flash(qk, y, 2, cum, 512)
            o_pool = _pool_mixer(y, 3, jnp.zeros((bx, POOL_HALO, POOL_WIDTH), F32), p["ab_pool_map"],
                                 p["ab_pool_scale"], 0, 512)
            x = _out_project([[o_fox], [o_pool]], p["w_out"], x, g1, tiles["tm"])
            states["fox_k"] = _heads(qk[:, :, FOX_WIDTH:], FOX_HEADS)
            states["fox_v"] = _heads(y[:, :, 2 * FOX_WIDTH:3 * FOX_WIDTH], FOX_HEADS)
            states["fox_logf"] = lf_t.transpose(0, 2, 1)
            states["pool"] = y[:, t - (POOL_HALO - 1):, 3 * FOX_WIDTH:]
        else:
            y, qn, kn, gates = _nsa_project(x, mods, jnp.arange(t), p, tiles)
            qw, kw_ = NSA_HEADS * HEAD_DIM, NSA_KV_WIDTH
            n_chunk = t // CMP_STRIDE
            kcmp = _compress(_chunk_rows(kn, 0, t), p["nsa_cmp_pos_k"], p["nsa_cmp_w1_k"], p["nsa_cmp_w2_k"])
            vcmp = _compress(_chunk_rows(y, (qw + kw_) // kw_, t), p["nsa_cmp_pos_v"], p["nsa_cmp_w1_v"],
                             p["nsa_cmp_w2_v"])
            n_sel = -(-t // SEL_BLOCK)
            o_cmp, sel_t = _cmp_select_t(qn, kcmp, vcmp, gates, n_chunk - 1, n_sel, 0, 256)
            o_sel = _nsa_flash(qn, kn, 1, y, (qw + 3 * kw_) // kw_, gates, 1, 256, "causal", sel_t=sel_t)
            o_win = _nsa_flash(qn, kn, 2, y, (qw + 5 * kw_) // kw_, gates, 2, 256, "window")
            x = _out_project([[o_cmp, o_sel, o_win]], p["w_out"], x, g1, tiles["tm"])
            buf = min(WINDOW, t)
            states["nsa_kc"] = _heads(kn[:, :, :kw_], NSA_GROUPS)
            states["nsa_vc"] = _heads(y[:, :, qw + kw_:qw + 2 * kw_], NSA_GROUPS)
            states["nsa_ks"] = _heads(kn[:, :, kw_:2 * kw_], NSA_GROUPS)
            states["nsa_vs"] = _heads(y[:, :, qw + 3 * kw_:qw + 4 * kw_], NSA_GROUPS)
            states["nsa_kw"] = _heads(kn[:, t - buf:, 2 * kw_:], NSA_GROUPS)
            states["nsa_vw"] = _heads(y[:, t - buf:, qw + 5 * kw_:qw + 6 * kw_], NSA_GROUPS)
        x = _ffn(x, p["norm_ffn"], sc2, sh2, g2, p["w_up"], p["w_down"], tiles["tm"])
    return x, states


def _sample_trunk(x, mod, layers, past, page_table):
    bx, tnew, d = x.shape
    rows = bx * tnew
    n_pages = page_table.shape[1]
    pos0 = n_pages * PAGE
    assert tnew < CMP_STRIDE and pos0 % CMP_STRIDE == 0 and pos0 >= WINDOW
    tiles = {"tm": rows, "tm_norm": rows}
    xf = x.reshape(1, rows, d)
    per_batch = lambda a: a.reshape(bx, tnew, a.shape[-1])
    states = {}
    for layer, p in enumerate(layers):
        e = layer // 2
        sh1, sc1, g1, sh2, sc2, g2 = [jnp.repeat(m, tnew, axis=0)[None] for m in jnp.split(mod[layer], 6, axis=-1)]
        mods = {"sc1": sc1, "sh1": sh1}
        if layer % 2 == 0:
            y, f_t, qk = _mix_ab(xf, mods, pos0, None, p, tiles)
            q_s, k_s = per_batch(qk[0, :, :FOX_WIDTH]), per_batch(qk[0, :, FOX_WIDTH:])
            v_s = per_batch(y[0, :, 2 * FOX_WIDTH:3 * FOX_WIDTH])
            u_s = per_batch(y[0, :, 3 * FOX_WIDTH:])
            f_new = f_t[0].reshape(FOX_HEADS, bx, tnew).transpose(1, 0, 2)
            cache_kt = past["cache_fox_k"][e].transpose(0, 2, 3, 1)
            cache_vt = past["cache_fox_v"][e].transpose(0, 2, 3, 1)
            cache_lft = past["cache_fox_logf"][e].transpose(0, 2, 1)
            o_fox, lf_new = _fox_sample(q_s, k_s, v_s, f_new, p["ab_b_fgate"], cache_kt, cache_vt, cache_lft,
                                        page_table)
            pool_prev = past["state_pool"][e]
            prefix = jnp.pad(pool_prev, ((0, 0), (1, 0), (0, 0)))
            o_pool = _pool_mixer(u_s, 0, prefix, p["ab_pool_map"], p["ab_pool_scale"], pos0, tnew)
            xf = _out_project([[o_fox.reshape(1, rows, -1)], [o_pool.reshape(1, rows, -1)]], p["w_out"], xf, g1, rows)
            states["fox_k"] = _heads(k_s, FOX_HEADS)
            states["fox_v"] = _heads(v_s, FOX_HEADS)
            states["fox_logf"] = lf_new.transpose(0, 2, 1)
            states["pool"] = jnp.concatenate([pool_prev, u_s], axis=1)[:, -(POOL_HALO - 1):]
        else:
            pos_rows = pos0 + jnp.arange(rows) % tnew
            y, qn, kn, gates = _nsa_project(xf, mods, pos_rows, p, tiles)
            qw, kw_ = NSA_HEADS * HEAD_DIM, NSA_KV_WIDTH
            q_s, gates_s = per_batch(qn[0]), per_batch(gates[0])
            kn_s, y_s = per_batch(kn[0]), per_batch(y[0])
            kc_s, ks_s, kwn_s = kn_s[..., :kw_], kn_s[..., kw_:2 * kw_], kn_s[..., 2 * kw_:]
            vc_s, vs_s, vwn_s = (y_s[..., qw + kw_:qw + 2 * kw_], y_s[..., qw + 3 * kw_:qw + 4 * kw_],
                                 y_s[..., qw + 5 * kw_:qw + 6 * kw_])
            d_major = lambda a: a.transpose(0, 2, 3, 1)
            kcmp = _compress_pages(d_major(past["cache_nsa_kc"][e]), page_table, p["nsa_cmp_pos_k"],
                                   p["nsa_cmp_w1_k"], p["nsa_cmp_w2_k"])
            vcmp = _compress_pages(d_major(past["cache_nsa_vc"][e]), page_table, p["nsa_cmp_pos_v"],
                                   p["nsa_cmp_w1_v"], p["nsa_cmp_w2_v"])
            total = pos0 + tnew
            n_cmp = total // CMP_STRIDE - 1
            n_sel = -(-total // SEL_BLOCK)
            nselp = -(-n_sel // (2 * LANE)) * (2 * LANE)
            o_cmp, sel = _cmp_select(q_s, kcmp, vcmp, gates_s, n_cmp, n_sel, nselp, pos0)
            selrows = jnp.repeat(sel.reshape(bx, NSA_GROUPS, tnew, nselp), NSA_REP, axis=1).reshape(
                bx, NSA_HEADS * tnew, nselp)
            o_sel = _sel_sample(q_s, ks_s, vs_s, selrows, gates_s, d_major(past["cache_nsa_ks"][e]),
                                d_major(past["cache_nsa_vs"][e]), page_table)
            kw_prev, vw_prev = past["state_nsa_kw"][e], past["state_nsa_vw"][e]
            o_win = _win_sample(q_s, kwn_s, vwn_s, gates_s, d_major(kw_prev), d_major(vw_prev))
            flat = lambda a: a.reshape(1, rows, -1)
            xf = _out_project([[flat(o_cmp), flat(o_sel), flat(o_win)]], p["w_out"], xf, g1, rows)
            buf = kw_prev.shape[1]
            states["nsa_kc"] = _heads(kc_s, NSA_GROUPS)
            states["nsa_vc"] = _heads(vc_s, NSA_GROUPS)
            states["nsa_ks"] = _heads(ks_s, NSA_GROUPS)
            states["nsa_vs"] = _heads(vs_s, NSA_GROUPS)
            states["nsa_kw"] = jnp.concatenate([kw_prev, _heads(kwn_s, NSA_GROUPS)], axis=1)[:, -buf:]
            states["nsa_vw"] = jnp.concatenate([vw_prev, _heads(vwn_s, NSA_GROUPS)], axis=1)[:, -buf:]
        xf = _ffn(xf, p["norm_ffn"], sc2, sh2, g2, p["w_up"], p["w_down"], rows)
    return xf.reshape(bx, tnew, d), states


_STATE_NAMES = ("fox_k", "fox_v", "fox_logf", "pool", "nsa_kc", "nsa_vc", "nsa_ks", "nsa_vs", "nsa_kw", "nsa_vw")


def kernel(x_prompt, x_sample, cache_fox_k, cache_fox_v, cache_fox_logf, state_pool, cache_nsa_kc, cache_nsa_vc,
           cache_nsa_ks, cache_nsa_vs, state_nsa_kw, state_nsa_vw, page_table, c_prompt, c_sample, w_mod, b_mod,
           norm_mix, norm_ffn, w_up, w_down, ab_w_in, ab_b_fgate, ab_q_norm, ab_k_norm, ab_pool_map, ab_pool_scale,
           ab_w_out, nsa_w_in, nsa_b_gate, nsa_q_norm, nsa_k_norm, nsa_cmp_pos_k, nsa_cmp_w1_k, nsa_cmp_w2_k,
           nsa_cmp_pos_v, nsa_cmp_w1_v, nsa_cmp_w2_v, nsa_w_out):
    assert w_mod.shape[0] == 2, "one forgetting/pooling layer followed by one sparse-attention layer"
    params = {
        "norm_mix": norm_mix, "norm_ffn": norm_ffn, "w_up": w_up, "w_down": w_down,
        "ab_w_in": ab_w_in, "ab_b_fgate": ab_b_fgate, "ab_q_norm": ab_q_norm, "ab_k_norm": ab_k_norm,
        "ab_pool_map": ab_pool_map, "ab_pool_scale": ab_pool_scale, "ab_w_out": ab_w_out,
        "nsa_w_in": nsa_w_in, "nsa_b_gate": nsa_b_gate, "nsa_q_norm": nsa_q_norm, "nsa_k_norm": nsa_k_norm,
        "nsa_cmp_pos_k": nsa_cmp_pos_k, "nsa_cmp_w1_k": nsa_cmp_w1_k, "nsa_cmp_w2_k": nsa_cmp_w2_k,
        "nsa_cmp_pos_v": nsa_cmp_pos_v, "nsa_cmp_w1_v": nsa_cmp_w1_v, "nsa_cmp_w2_v": nsa_cmp_w2_v,
        "nsa_w_out": nsa_w_out,
    }
    past = {
        "cache_fox_k": cache_fox_k, "cache_fox_v": cache_fox_v, "cache_fox_logf": cache_fox_logf,
        "state_pool": state_pool, "cache_nsa_kc": cache_nsa_kc, "cache_nsa_vc": cache_nsa_vc,
        "cache_nsa_ks": cache_nsa_ks, "cache_nsa_vs": cache_nsa_vs,
        "state_nsa_kw": state_nsa_kw, "state_nsa_vw": state_nsa_vw,
    }
    n_prompt = c_prompt.shape[0]
    mod = _modulation(jnp.concatenate([c_prompt, c_sample], axis=0), w_mod, b_mod)
    layers = [_layer_params(params, layer) for layer in range(w_mod.shape[0])]
    y_prompt, sp = _prompt_trunk(x_prompt, mod[:, :n_prompt], layers)
    y_sample, ss = _sample_trunk(x_sample, mod[:, n_prompt:], layers, past, page_table)
    return (y_prompt, y_sample, *[sp[n][None] for n in _STATE_NAMES], *[ss[n][None] for n in _STATE_NAMES])
```

```python
import functools

import jax
import jax.numpy as jnp
from jax import lax
from jax.experimental import pallas as pl
from jax.experimental.pallas import tpu as pltpu

F32 = jnp.float32
BF16 = jnp.bfloat16

D_MODEL = 1024
HEAD_DIM = 64
EPS = 1e-6
ROPE_THETA = 10000.0
NEG_INF = -1e30
MASKED = -2e30
BIG = 1e9
LOG2E = 1.4426950408889634
PAGE = 128
FOX_HEADS = 8
FOX_WIDTH = FOX_HEADS * HEAD_DIM
POOL_WINDOWS = (2, 4, 8, 16)
POOL_WIDTH = 512
POOL_GROUP_CH = 128
POOL_HALO = 16
NSA_HEADS = 16
NSA_GROUPS = 4
NSA_REP = NSA_HEADS // NSA_GROUPS
NSA_KV_WIDTH = NSA_GROUPS * HEAD_DIM
NSA_BRANCHES = 3
CMP_BLOCK = 32
CMP_STRIDE = 16
SEL_BLOCK = 64
SEL_TOPK = 16
WINDOW = 512
D_FF = 2816
LANE = 128
SUBLANE = 8
GROUPS_PER_TILE = LANE // HEAD_DIM
PAGES_PER_STEP = 16
ATTN_PAGES_PER_STEP = 16
VMEM_LIMIT = 48 * 1024 * 1024


def _params(*sem):
    return pltpu.CompilerParams(dimension_semantics=sem, vmem_limit_bytes=VMEM_LIMIT)


def _nt(a, b):
    return lax.dot_general(a, b, (((1,), (1,)), ((), ())), preferred_element_type=F32)


def _mm(a, b):
    return jnp.dot(a, b, preferred_element_type=F32)


def _mod_body(c_ref, w_ref, b_ref, o_ref):
    c = c_ref[...]
    a = (c * jax.nn.sigmoid(c)).astype(BF16)
    o_ref[0] = _mm(a, w_ref[0].astype(BF16)) + b_ref[0]


def _modulation(c_all, w_mod, b_mod):
    n_layers, d, n = w_mod.shape
    rows = c_all.shape[0]
    tn = 1536
    return pl.pallas_call(
        _mod_body,
        grid=(n_layers, n // tn),
        in_specs=[
            pl.BlockSpec((rows, d), lambda l, j: (0, 0)),
            pl.BlockSpec((1, d, tn), lambda l, j: (l, 0, j)),
            pl.BlockSpec((1, 1, tn), lambda l, j: (l, 0, j)),
        ],
        out_specs=pl.BlockSpec((1, rows, tn), lambda l, j: (l, 0, j)),
        out_shape=jax.ShapeDtypeStruct((n_layers, rows, n), F32),
        compiler_params=_params("arbitrary", "arbitrary"),
        name="modulation",
    )(c_all, w_mod, b_mod.reshape(n_layers, 1, n))


def _modulated_norm(x, gain, sc, sh):
    xn = x * lax.rsqrt(jnp.mean(x * x, axis=-1, keepdims=True) + EPS) * gain
    return xn * (1.0 + sc) + sh


def _proj_body(*refs, has_side, has_t):
    x_ref, g_ref, sc_ref, sh_ref, w_ref = refs[:5]
    k = 5
    side_ref = t_ref = None
    if has_side:
        side_ref = refs[k]; k += 1
    if has_t:
        t_ref = refs[k]; k += 1
    y_ref = refs[k]; k += 1
    hb = _modulated_norm(x_ref[0], g_ref[...], sc_ref[0], sh_ref[0]).astype(BF16)
    y_ref[0] = _mm(hb, w_ref[...])
    if has_side:
        refs[k][0] = _mm(hb, side_ref[...]); k += 1
    if has_t:
        refs[k][0] = _nt(t_ref[...], hb)


def _resident(shape):
    return pl.BlockSpec(shape, lambda *_: (0,) * len(shape), pipeline_mode=pl.Buffered(1))


def _project(x, gain, sc, sh, w, tm, w_side=None, w_t=None):
    bx, tx, d = x.shape
    n_cols = w.shape[1]
    r = sc.shape[1]
    mod_spec = (pl.BlockSpec((1, tm, d), lambda b, i: (b, i, 0)) if r == tx
                else pl.BlockSpec((1, 1, d), lambda b, i: (b, 0, 0)))
    in_specs = [pl.BlockSpec((1, tm, d), lambda b, i: (b, i, 0)), _resident((1, d)), mod_spec, mod_spec,
                _resident((d, n_cols))]
    args = [x, gain.reshape(1, d), sc, sh, w]
    out_specs = [pl.BlockSpec((1, tm, n_cols), lambda b, i: (b, i, 0))]
    out_shape = [jax.ShapeDtypeStruct((bx, tx, n_cols), F32)]
    if w_side is not None:
        ns = w_side.shape[1]
        in_specs.append(_resident((d, ns)))
        args.append(w_side)
        out_specs.append(pl.BlockSpec((1, tm, ns), lambda b, i: (b, i, 0)))
        out_shape.append(jax.ShapeDtypeStruct((bx, tx, ns), F32))
    if w_t is not None:
        nt = w_t.shape[0]
        in_specs.append(_resident((nt, d)))
        args.append(w_t)
        out_specs.append(pl.BlockSpec((1, nt, tm), lambda b, i: (b, 0, i)))
        out_shape.append(jax.ShapeDtypeStruct((bx, nt, tx), F32))
    return pl.pallas_call(
        functools.partial(_proj_body, has_side=w_side is not None, has_t=w_t is not None),
        grid=(bx, tx // tm),
        in_specs=in_specs, out_specs=out_specs, out_shape=out_shape,
        compiler_params=_params("arbitrary", "arbitrary"),
        name="project",
    )(*args)


def _headnorm_body(*refs, tiles, rope, has_gate, n_out):
    y_ref, gain_ref = refs[:2]
    k = 2
    cos_ref = sin_ref = gl_ref = bg_ref = None
    if rope:
        cos_ref, sin_ref = refs[k], refs[k + 1]; k += 2
    if has_gate:
        gl_ref, bg_ref = refs[k], refs[k + 1]; k += 2
    outs = refs[k:k + n_out]
    gate_out = refs[k + n_out] if has_gate else None
    lane = lax.broadcasted_iota(jnp.int32, (1, LANE), 1)
    low_head = lane < HEAD_DIM
    first_half = (lane % HEAD_DIM) < (HEAD_DIM // 2)
    for src, oi, dst in tiles:
        y = y_ref[0, :, src * LANE:(src + 1) * LANE]
        y2 = y * y
        s_lo = jnp.sum(jnp.where(low_head, y2, 0.0), axis=-1, keepdims=True)
        s_hi = jnp.sum(jnp.where(low_head, 0.0, y2), axis=-1, keepdims=True)
        ms = jnp.where(low_head, s_lo, s_hi) * (1.0 / HEAD_DIM)
        yn = y * lax.rsqrt(ms + EPS) * gain_ref[:, src * LANE:(src + 1) * LANE]
        if rope:
            partner = jnp.where(first_half, pltpu.roll(yn, LANE - HEAD_DIM // 2, 1), pltpu.roll(yn, HEAD_DIM // 2, 1))
            yn = yn * cos_ref[...] + partner * sin_ref[...]
        outs[oi][0, :, dst * LANE:(dst + 1) * LANE] = yn
    if has_gate:
        gate_out[0] = jax.nn.sigmoid(gl_ref[0] + bg_ref[...])


def _headnorm(y, width, gains, tiles, out_widths, tm, rope_tabs=None, gate_logits=None, gate_bias=None):
    bx, tx, _ = y.shape
    rope = rope_tabs is not None
    has_gate = gate_logits is not None
    in_specs = [pl.BlockSpec((1, tm, width), lambda b, i: (b, i, 0)),
                pl.BlockSpec((1, width), lambda b, i: (0, 0))]
    args = [y, gains]
    if rope:
        in_specs += [pl.BlockSpec((tm, LANE), lambda b, i: (i, 0))] * 2
        args += list(rope_tabs)
    if has_gate:
        in_specs += [pl.BlockSpec((1, tm, LANE), lambda b, i: (b, i, 0)), pl.BlockSpec((1, LANE), lambda b, i: (0, 0))]
        args += [gate_logits, gate_bias]
    out_specs = [pl.BlockSpec((1, tm, w), lambda b, i: (b, i, 0)) for w in out_widths]
    out_shape = [jax.ShapeDtypeStruct((bx, tx, w), F32) for w in out_widths]
    if has_gate:
        out_specs.append(pl.BlockSpec((1, tm, LANE), lambda b, i: (b, i, 0)))
        out_shape.append(jax.ShapeDtypeStruct((bx, tx, LANE), F32))
    return pl.pallas_call(
        functools.partial(_headnorm_body, tiles=tuple(tiles), rope=rope, has_gate=has_gate, n_out=len(out_widths)),
        grid=(bx, tx // tm),
        in_specs=in_specs, out_specs=out_specs, out_shape=out_shape,
        compiler_params=_params("arbitrary", "arbitrary"),
        name="headnorm",
    )(*args)


def _rope_tables(pos):
    half = HEAD_DIM // 2
    inv_freq = ROPE_THETA ** (-jnp.arange(half, dtype=F32) / half)
    ang = pos.astype(F32)[:, None] * inv_freq[None, :]
    cos, sin = jnp.cos(ang), jnp.sin(ang)
    reps = LANE // HEAD_DIM
    return (jnp.tile(jnp.concatenate([cos, cos], axis=1), (1, reps)),
            jnp.tile(jnp.concatenate([-sin, sin], axis=1), (1, reps)))


def _lane_cumsum(x):
    n = x.shape[-1]
    lane = lax.broadcasted_iota(jnp.int32, x.shape, x.ndim - 1)
    s = 1
    while s < n:
        x = x + jnp.where(lane >= s, pltpu.roll(x, s, x.ndim - 1), 0.0)
        s *= 2
    return x


def _logf_body(f_ref, b_ref, lf_ref, cum_ref):
    lf = jax.nn.log_sigmoid(f_ref[0] + b_ref[...])
    lf_ref[0] = lf
    cum_ref[0] = _lane_cumsum(lf)


def _logf_cumsum(f_t, b_fgate):
    bx, h, t = f_t.shape
    spec = pl.BlockSpec((1, h, t), lambda b: (b, 0, 0))
    return pl.pallas_call(
        _logf_body, grid=(bx,),
        in_specs=[spec, pl.BlockSpec((h, 1), lambda b: (0, 0))],
        out_specs=[spec, spec],
        out_shape=[jax.ShapeDtypeStruct(f_t.shape, F32)] * 2,
        compiler_params=_params("arbitrary"),
        name="logf_cumsum",
    )(f_t, b_fgate.reshape(h, 1))


def _half_mask(shape, half):
    lane = lax.broadcasted_iota(jnp.int32, shape, len(shape) - 1)
    return (lane % LANE) // HEAD_DIM == half


def _fox_flash_body(q_ref, k_ref, v_ref, cum_ref, o_ref, qm_scr, m_scr, l_scr, acc_scr, *, tq, nk, scale):
    qi, ki = pl.program_id(1), pl.program_id(2)
    n_pairs = FOX_HEADS // 2
    reps = tq // LANE

    @pl.when(ki == 0)
    def _():
        for h in range(FOX_HEADS):
            c = h // 2
            qpair = q_ref[0, :, c * LANE:(c + 1) * LANE] * scale
            qm_scr[h] = jnp.where(_half_mask(qpair.shape, h % 2), qpair, 0.0).astype(BF16)
        m_scr[...] = jnp.full(m_scr.shape, NEG_INF, F32)
        l_scr[...] = jnp.zeros(l_scr.shape, F32)
        acc_scr[...] = jnp.zeros(acc_scr.shape, F32)

    def update(diagonal):
        if diagonal:
            vis = (lax.broadcasted_iota(jnp.int32, (tq, tq), 1) <= lax.broadcasted_iota(jnp.int32, (tq, tq), 0))
        for c in range(n_pairs):
            k_pair = k_ref[0, :, c * LANE:(c + 1) * LANE].astype(BF16)
            v_pair = v_ref[0, :, c * LANE:(c + 1) * LANE]
            pv = []
            alphas = []
            for half in range(2):
                h = 2 * c + half
                s = _nt(qm_scr[h], k_pair) - cum_ref[0, h:h + 1, :] * LOG2E
                if diagonal:
                    s = jnp.where(vis, s, MASKED)
                m_prev = m_scr[h]
                m_new = jnp.maximum(m_prev, jnp.max(s, axis=1, keepdims=True))
                alpha = jnp.exp2(m_prev - m_new)
                p = jnp.exp2(s - jnp.concatenate([m_new] * reps, axis=1))
                l_scr[h] = alpha * l_scr[h] + jnp.sum(p, axis=1, keepdims=True)
                m_scr[h] = m_new
                v_half = jnp.where(_half_mask(v_pair.shape, half), v_pair, 0.0).astype(BF16)
                pv.append(_mm(p.astype(BF16), v_half))
                alphas.append(alpha)
            alpha_pair = jnp.where(_half_mask(alphas[0].shape, 0), alphas[0], alphas[1])
            acc_scr[c] = alpha_pair * acc_scr[c] + (pv[0] + pv[1])

    pl.when(ki < qi)(functools.partial(update, False))
    pl.when(ki == qi)(functools.partial(update, True))

    @pl.when(ki == nk - 1)
    def _():
        for c in range(n_pairs):
            l_pair = jnp.where(_half_mask((tq, LANE), 0), l_scr[2 * c], l_scr[2 * c + 1])
            o_ref[0, :, c * LANE:(c + 1) * LANE] = acc_scr[c] / l_pair


def _fox_flash(qk, v_arr, vcb, cum, tq):
    bx, t, _ = qk.shape
    w = FOX_WIDTH
    nq = t // tq
    kmap = lambda qi, ki: jnp.minimum(ki, qi)
    return pl.pallas_call(
        functools.partial(_fox_flash_body, tq=tq, nk=nq, scale=HEAD_DIM ** -0.5 * LOG2E),
        grid=(bx, nq, nq),
        in_specs=[pl.BlockSpec((1, tq, w), lambda b, qi, ki: (b, qi, 0)),
                  pl.BlockSpec((1, tq, w), lambda b, qi, ki: (b, kmap(qi, ki), 1)),
                  pl.BlockSpec((1, tq, w), lambda b, qi, ki: (b, kmap(qi, ki), vcb)),
                  pl.BlockSpec((1, FOX_HEADS, tq), lambda b, qi, ki: (b, 0, kmap(qi, ki)))],
        out_specs=pl.BlockSpec((1, tq, w), lambda b, qi, ki: (b, qi, 0)),
        out_shape=jax.ShapeDtypeStruct((bx, t, w), F32),
        scratch_shapes=[pltpu.VMEM((FOX_HEADS, tq, LANE), BF16), pltpu.VMEM((FOX_HEADS, tq, LANE), F32),
                        pltpu.VMEM((FOX_HEADS, tq, LANE), F32), pltpu.VMEM((FOX_HEADS // 2, tq, LANE), F32)],
        compiler_params=_params("arbitrary", "arbitrary", "arbitrary"),
        name="fox_flash",
    )(qk, qk, v_arr, cum)


def _nsa_flash_body(*refs, tq, nk, mode, has_sel, gate_branch, scale):
    q_ref, k_ref, v_ref = refs[:3]
    i = 3
    sel_ref = None
    if has_sel:
        sel_ref = refs[i]; i += 1
    gate_ref, o_ref, qt_scr, m_scr, l_scr, acc_scr = refs[i:i + 6]
    qi, ki = pl.program_id(1), pl.program_id(2)
    kt = ki if mode == "causal" else qi - (nk - 1) + ki
    active = (ki <= qi) if mode == "causal" else (kt >= 0)

    @pl.when(ki == 0)
    def _():
        zeros = jnp.zeros((HEAD_DIM, tq), F32)
        for pair in range(NSA_HEADS // 2):
            q_pair_t = (q_ref[0, :, pair * LANE:(pair + 1) * LANE] * scale).T
            for half in range(2):
                g, r = divmod(2 * pair + half, NSA_REP)
                q_t = q_pair_t[half * HEAD_DIM:(half + 1) * HEAD_DIM]
                parts = [q_t, zeros] if g % GROUPS_PER_TILE == 0 else [zeros, q_t]
                qt_scr[g, 0:LANE, r * tq:(r + 1) * tq] = jnp.concatenate(parts, axis=0).astype(BF16)
        if has_sel:
            nsel = sel_ref.shape[2]
            for g in range(NSA_GROUPS):
                bias = jnp.where(sel_ref[0, g] > 0.5, 0.0, MASKED)
                bias = jnp.concatenate([bias, jnp.zeros((LANE - nsel, tq), F32)], axis=0).astype(BF16)
                qt_scr[g, LANE:2 * LANE, :] = jnp.concatenate([bias] * NSA_REP, axis=1)
        m_scr[...] = jnp.full(m_scr.shape, NEG_INF, F32)
        l_scr[...] = jnp.zeros(l_scr.shape, F32)
        acc_scr[...] = jnp.zeros(acc_scr.shape, F32)

    def update(mask_kind):
        key = lax.broadcasted_iota(jnp.int32, (tq, tq), 0)
        query = lax.broadcasted_iota(jnp.int32, (tq, tq), 1)
        if mask_kind == "causal":
            vis = key <= query
        elif mask_kind == "tail":
            vis = key + (WINDOW - (nk - 1) * tq) > query
        if has_sel:
            kblk = (kt * tq + lax.broadcasted_iota(jnp.int32, (tq, LANE), 0)) // SEL_BLOCK
            expand = (kblk == lax.broadcasted_iota(jnp.int32, (tq, LANE), 1)).astype(BF16)
        logits, v_ts = [], []
        for c in range(NSA_GROUPS // GROUPS_PER_TILE):
            k_pair = k_ref[0, :, c * LANE:(c + 1) * LANE].astype(BF16)
            v_pair_t = v_ref[0, :, c * LANE:(c + 1) * LANE].T
            if has_sel:
                k_pair = jnp.concatenate([k_pair, expand], axis=1)
            for gg in range(GROUPS_PER_TILE):
                g = c * GROUPS_PER_TILE + gg
                v_ts.append(v_pair_t[gg * HEAD_DIM:(gg + 1) * HEAD_DIM].astype(BF16))
                s = _mm(k_pair, qt_scr[g])
                if mask_kind is not None:
                    s = jnp.where(jnp.concatenate([vis] * NSA_REP, axis=1), s, MASKED)
                logits.append(s)
        for g in range(NSA_GROUPS):
            s = logits[g]
            m_prev = m_scr[g]
            m_new = jnp.maximum(m_prev, jnp.max(s, axis=0, keepdims=True))
            alpha = jnp.exp2(m_prev - m_new)
            p = jnp.exp2(s - m_new[0:1])
            l_scr[g] = alpha * l_scr[g] + jnp.sum(p, axis=0, keepdims=True)
            acc_scr[g] = alpha[0:1] * acc_scr[g] + _mm(v_ts[g], p.astype(BF16))
            m_scr[g] = m_new

    last = ki == nk - 1
    if mode == "causal":
        pl.when(ki < qi)(functools.partial(update, None))
        pl.when(ki == qi)(functools.partial(update, "causal"))
    else:
        assert nk >= 2 and WINDOW % tq == 0
        pl.when(active & (ki == 0))(functools.partial(update, "tail"))
        if nk > 2:
            pl.when(active & (ki > 0) & jnp.logical_not(last))(functools.partial(update, None))
        pl.when(last)(functools.partial(update, "causal"))

    @pl.when(ki == nk - 1)
    def _():
        low = _half_mask((tq, LANE), 0)
        for pair in range(NSA_HEADS // 2):
            parts, gate_cols = [], []
            for half in range(2):
                h = 2 * pair + half
                g, r = divmod(h, NSA_REP)
                lanes = slice(r * tq, (r + 1) * tq)
                parts.append(acc_scr[g, :, lanes] / l_scr[g, 0:1, lanes])
                c = h * NSA_BRANCHES + gate_branch
                gate_cols.append(gate_ref[0, :, c:c + 1])
            o_pair = jnp.concatenate(parts, axis=0).T
            o_ref[0, :, pair * LANE:(pair + 1) * LANE] = o_pair * jnp.where(low, gate_cols[0], gate_cols[1])


def _nsa_flash(qn, kn, kcb, y, vcb, gates, gate_branch, tq, mode, sel_t=None):
    bx, t, wq = qn.shape
    wk = NSA_KV_WIDTH
    nq = t // tq
    nk = nq if mode == "causal" else WINDOW // tq + 1
    if mode == "causal":
        kmap = lambda qi, ki: jnp.minimum(ki, qi)
    else:
        kmap = lambda qi, ki: jnp.maximum(qi - (nk - 1) + ki, 0)
    in_specs = [pl.BlockSpec((1, tq, wq), lambda b, qi, ki: (b, qi, 0)),
                pl.BlockSpec((1, tq, wk), lambda b, qi, ki: (b, kmap(qi, ki), kcb)),
                pl.BlockSpec((1, tq, wk), lambda b, qi, ki: (b, kmap(qi, ki), vcb))]
    args = [qn, kn, y]
    if sel_t is not None:
        in_specs.append(pl.BlockSpec((1, NSA_GROUPS, sel_t.shape[2], tq), lambda b, qi, ki: (b, 0, 0, qi)))
        args.append(sel_t)
    in_specs.append(pl.BlockSpec((1, tq, LANE), lambda b, qi, ki: (b, qi, 0)))
    args.append(gates)
    return pl.pallas_call(
        functools.partial(_nsa_flash_body, tq=tq, nk=nk, mode=mode, has_sel=sel_t is not None,
                          gate_branch=gate_branch, scale=HEAD_DIM ** -0.5 * LOG2E),
        grid=(bx, nq, nk),
        in_specs=in_specs,
        out_specs=pl.BlockSpec((1, tq, wq), lambda b, qi, ki: (b, qi, 0)),
        out_shape=jax.ShapeDtypeStruct((bx, t, wq), F32),
        scratch_shapes=[pltpu.VMEM((NSA_GROUPS, (2 if sel_t is not None else 1) * LANE, NSA_REP * tq), BF16),
                        pltpu.VMEM((NSA_GROUPS, SUBLANE, NSA_REP * tq), F32),
                        pltpu.VMEM((NSA_GROUPS, SUBLANE, NSA_REP * tq), F32),
                        pltpu.VMEM((NSA_GROUPS, HEAD_DIM, NSA_REP * tq), F32)],
        compiler_params=_params("arbitrary", "arbitrary", "arbitrary"),
        name="nsa_flash_" + mode,
    )(*args)


def _pool_body(u_ref, pre_ref, map_ref, scale_ref, o_ref, ext_scr, *, tm, pos0):
    j = pl.program_id(1)

    @pl.when(j == 0)
    def _():
        ext_scr[0:POOL_HALO] = pre_ref[0]

    @pl.when(j > 0)
    def _():
        ext_scr[0:POOL_HALO] = ext_scr[tm:tm + POOL_HALO]

    ext_scr[POOL_HALO:POOL_HALO + tm] = u_ref[0]
    qpos = pos0 + j * tm + lax.broadcasted_iota(jnp.int32, (tm, 1), 0)
    for g, w in enumerate(POOL_WINDOWS):
        lo, hi = g * POOL_GROUP_CH, (g + 1) * POOL_GROUP_CH
        u_new = ext_scr[POOL_HALO:POOL_HALO + tm, lo:hi]
        tot = u_new
        for d in range(1, w):
            tot = tot + ext_scr[POOL_HALO - d:POOL_HALO - d + tm, lo:hi]
        count = jnp.minimum(w, qpos + 1).astype(F32)
        diff = tot / count - u_new
        y = _mm(diff.astype(BF16), map_ref[g].astype(BF16))
        o_ref[0, :, lo:hi] = y * scale_ref[:, lo:hi]


def _pool_mixer(u_arr, ucb, prefix, w_map, scale, pos0, tm):
    bx, t, _ = u_arr.shape
    c = POOL_WIDTH
    return pl.pallas_call(
        functools.partial(_pool_body, tm=tm, pos0=pos0),
        grid=(bx, t // tm),
        in_specs=[pl.BlockSpec((1, tm, c), lambda b, j: (b, j, ucb)),
                  pl.BlockSpec((1, POOL_HALO, c), lambda b, j: (b, 0, 0)),
                  pl.BlockSpec((len(POOL_WINDOWS), POOL_GROUP_CH, POOL_GROUP_CH), lambda b, j: (0, 0, 0)),
                  pl.BlockSpec((1, c), lambda b, j: (0, 0))],
        out_specs=pl.BlockSpec((1, tm, c), lambda b, j: (b, j, 0)),
        out_shape=jax.ShapeDtypeStruct((bx, t, c), F32),
        scratch_shapes=[pltpu.VMEM((POOL_HALO + tm, c), F32)],
        compiler_params=_params("arbitrary", "arbitrary"),
        name="pool_mixer",
    )(u_arr, prefix, w_map, scale.reshape(1, c))


def _outproj_body(*refs, group_sizes):
    n_a = sum(group_sizes)
    a_refs = refs[:n_a]
    w_refs = refs[n_a:n_a + len(group_sizes)]
    res_ref, gate_ref, o_ref = refs[n_a + len(group_sizes):]
    y = None
    k = 0
    for gi, n in enumerate(group_sizes):
        a = a_refs[k][0]
        for r in a_refs[k + 1:k + n]:
            a = a + r[0]
        k += n
        part = _mm(a.astype(BF16), w_refs[gi][...])
        y = part if y is None else y + part
    o_ref[0] = res_ref[0] + gate_ref[0] * y


def _out_project(groups, w, res, gate, tm):
    bx, tx, d = res.shape
    kg = groups[0][0].shape[-1]
    r = gate.shape[1]
    row = pl.BlockSpec((1, tm, d), lambda b, i: (b, i, 0))
    a_spec = pl.BlockSpec((1, tm, kg), lambda b, i: (b, i, 0))
    in_specs, args = [], []
    for grp in groups:
        for a in grp:
            in_specs.append(a_spec); args.append(a)
    for gi in range(len(groups)):
        in_specs.append(pl.BlockSpec((kg, d), lambda b, i, gi=gi: (gi, 0), pipeline_mode=pl.Buffered(1)))
        args.append(w)
    in_specs += [row, row if r == tx else pl.BlockSpec((1, 1, d), lambda b, i: (b, 0, 0))]
    args += [res, gate]
    return pl.pallas_call(
        functools.partial(_outproj_body, group_sizes=tuple(len(g) for g in groups)),
        grid=(bx, tx // tm),
        in_specs=in_specs,
        out_specs=row,
        out_shape=jax.ShapeDtypeStruct((bx, tx, d), F32),
        compiler_params=_params("arbitrary", "arbitrary"),
        name="out_project",
    )(*args)


FFN_SLICES = 2


def _ffn_body(x_ref, g_ref, sc_ref, sh_ref, gate_ref, wg_ref, wu_ref, wd_ref, o_ref):
    x = x_ref[0]
    h = _modulated_norm(x, g_ref[...], sc_ref[0], sh_ref[0]).astype(BF16)
    width = D_FF // FFN_SLICES
    y = None
    for c in range(FFN_SLICES):
        cols = slice(c * width, (c + 1) * width)
        gt = _mm(h, wg_ref[:, cols])
        up = _mm(h, wu_ref[:, cols])
        act = ((gt * jax.nn.sigmoid(gt)) * up).astype(BF16)
        part = _mm(act, wd_ref[cols, :])
        y = part if y is None else y + part
    o_ref[0] = x + gate_ref[0] * y


def _ffn(x, gain, sc, sh, gate, w_up, w_down, tm):
    bx, tx, d = x.shape
    r = sc.shape[1]
    mod_spec = (pl.BlockSpec((1, tm, d), lambda b, i: (b, i, 0)) if r == tx
                else pl.BlockSpec((1, 1, d), lambda b, i: (b, 0, 0)))
    x_spec = pl.BlockSpec((1, tm, d), lambda b, i: (b, i, 0))
    half = lambda j: pl.BlockSpec((d, D_FF), lambda b, i: (0, j), pipeline_mode=pl.Buffered(1))
    return pl.pallas_call(
        _ffn_body,
        grid=(bx, tx // tm),
        in_specs=[x_spec, _resident((1, d)), mod_spec, mod_spec, mod_spec, half(0), half(1), _resident((D_FF, d))],
        out_specs=x_spec,
        out_shape=jax.ShapeDtypeStruct(x.shape, F32),
        compiler_params=_params("arbitrary", "arbitrary"),
        name="swiglu",
    )(x, gain.reshape(1, d), sc, sh, gate, w_up, w_up, w_down)


def _chunk_rows_body(a_ref, o_ref, *, n_chunks):
    for s in range(CMP_STRIDE):
        rows = a_ref[0, pl.ds(s, n_chunks, stride=CMP_STRIDE), :]
        for g in range(GROUPS_PER_TILE):
            o_ref[0, g, :, s * HEAD_DIM:(s + 1) * HEAD_DIM] = rows[:, g * HEAD_DIM:(g + 1) * HEAD_DIM]


def _chunk_rows(a, acb, tm):
    bx, t, _ = a.shape
    n_chunks = tm // CMP_STRIDE
    tiles = NSA_KV_WIDTH // LANE
    return pl.pallas_call(
        functools.partial(_chunk_rows_body, n_chunks=n_chunks),
        grid=(bx, t // tm, tiles),
        in_specs=[pl.BlockSpec((1, tm, LANE), lambda b, i, c: (b, i, acb * tiles + c))],
        out_specs=pl.BlockSpec((1, GROUPS_PER_TILE, n_chunks, CMP_STRIDE * HEAD_DIM), lambda b, i, c: (b, c, i, 0)),
        out_shape=jax.ShapeDtypeStruct((bx, NSA_GROUPS, t // CMP_STRIDE, CMP_STRIDE * HEAD_DIM), F32),
        compiler_params=_params("arbitrary", "arbitrary", "arbitrary"),
        name="chunk_rows",
    )(a)


def _compress_mlp(a, pos_ref, w1_ref, w2_ref):
    half = CMP_STRIDE * HEAD_DIM
    w1 = w1_ref[...].astype(BF16)
    first = _mm(a, w1[:half])
    second = _mm(a, w1[half:])
    bias = _mm(pos_ref[...].astype(BF16), w1)
    hidden = (first + pltpu.roll(second, a.shape[0] - 1, 0)) + bias
    return _mm(jax.nn.gelu(hidden).astype(BF16), w2_ref[...].astype(BF16))


def _compress_body(a_ref, pos_ref, w1_ref, w2_ref, o_ref):
    o_ref[0, 0] = _compress_mlp(a_ref[0, 0].astype(BF16), pos_ref, w1_ref, w2_ref)


def _compress(chunks, pos_emb, w1, w2):
    bx, g, n_chunks, half = chunks.shape
    hidden = w1.shape[1]
    return pl.pallas_call(
        _compress_body,
        grid=(bx, g),
        in_specs=[pl.BlockSpec((1, 1, n_chunks, half), lambda b, i: (b, i, 0, 0)),
                  pl.BlockSpec((1, 2 * half), lambda b, i: (0, 0)),
                  pl.BlockSpec((2 * half, hidden), lambda b, i: (0, 0)),
                  pl.BlockSpec((hidden, HEAD_DIM), lambda b, i: (0, 0))],
        out_specs=pl.BlockSpec((1, 1, n_chunks, HEAD_DIM), lambda b, i: (b, i, 0, 0)),
        out_shape=jax.ShapeDtypeStruct((bx, g, n_chunks, HEAD_DIM), F32),
        compiler_params=_params("arbitrary", "arbitrary"),
        name="compress",
    )(chunks, pos_emb.reshape(1, 2 * half), w1, w2)


def _compress_pages_body(pt_ref, *rest, n_steps, pps):
    del pt_ref
    page_refs = rest[:pps]
    pos_ref, w1_ref, w2_ref, o_ref, tok_scr, chunk_scr = rest[pps:]
    p = pl.program_id(1)
    cpp = PAGE // CMP_STRIDE
    pairs = NSA_GROUPS // GROUPS_PER_TILE
    for i in range(pps):
        row0 = pl.multiple_of((p * pps + i) * cpp, cpp)
        for c in range(pairs):
            tok = tok_scr.at[i * pairs + c]
            tok[...] = page_refs[i][0, c * GROUPS_PER_TILE:(c + 1) * GROUPS_PER_TILE].reshape(LANE, PAGE).T
            for s in range(CMP_STRIDE):
                rows = tok[pl.ds(s, cpp, stride=CMP_STRIDE), :]
                for g in range(GROUPS_PER_TILE):
                    chunk_scr[c * GROUPS_PER_TILE + g, pl.ds(row0, cpp), s * HEAD_DIM:(s + 1) * HEAD_DIM] = (
                        rows[:, g * HEAD_DIM:(g + 1) * HEAD_DIM])

    @pl.when(p == n_steps - 1)
    def _():
        for g in range(NSA_GROUPS):
            o_ref[0, :, g * HEAD_DIM:(g + 1) * HEAD_DIM] = _compress_mlp(
                chunk_scr[g].astype(BF16), pos_ref, w1_ref, w2_ref)


def _compress_pages(cache_t, page_table, pos_emb, w1, w2):
    bx, n_pages = page_table.shape
    pps = _pages_per_step(n_pages)
    cpp = PAGE // CMP_STRIDE
    n_chunks = n_pages * cpp
    half = CMP_STRIDE * HEAD_DIM
    hidden = w1.shape[1]
    const = lambda b, p, pt: (0, 0)
    return pl.pallas_call(
        functools.partial(_compress_pages_body, n_steps=n_pages // pps, pps=pps),
        grid_spec=pltpu.PrefetchScalarGridSpec(
            num_scalar_prefetch=1, grid=(bx, n_pages // pps),
            in_specs=[pl.BlockSpec((1, NSA_GROUPS, HEAD_DIM, PAGE), _page_map(n_pages, pps, i, 4)) for i in range(pps)]
            + [pl.BlockSpec((1, 2 * half), const), pl.BlockSpec((2 * half, hidden), const),
               pl.BlockSpec((hidden, HEAD_DIM), const)],
            out_specs=pl.BlockSpec((1, n_chunks, NSA_KV_WIDTH), lambda b, p, pt: (b, 0, 0)),
            scratch_shapes=[pltpu.VMEM((pps * NSA_GROUPS // GROUPS_PER_TILE, PAGE, LANE), F32),
                            pltpu.VMEM((NSA_GROUPS, n_chunks, half), F32)]),
        out_shape=jax.ShapeDtypeStruct((bx, n_chunks, NSA_KV_WIDTH), F32),
        compiler_params=_params("arbitrary", "arbitrary"),
        name="compress_pages",
    )(page_table.reshape(-1), *([cache_t] * pps), pos_emb.reshape(1, 2 * half), w1, w2)


def _bf16_terms(x):
    hi = x.astype(BF16)
    r1 = x - hi.astype(F32)
    mid = r1.astype(BF16)
    lo = (r1 - mid.astype(F32)).astype(BF16)
    return hi, mid, lo


def _cmp_select_body(q_ref, kc_ref, vc_ref, gate_ref, o_ref, sel_ref, qbd_scr, *, tq, n_cmp, n_sel, pos0, scale):
    ncp = kc_ref.shape[1]
    nselp = sel_ref.shape[-1]
    gt = NSA_GROUPS * tq
    rows = NSA_REP * gt
    qbd_scr[...] = jnp.zeros(qbd_scr.shape, F32)
    for h in range(NSA_HEADS):
        g, r = divmod(h, NSA_REP)
        qbd_scr[r * gt + g * tq:r * gt + (g + 1) * tq, g * HEAD_DIM:(g + 1) * HEAD_DIM] = (
            q_ref[0, :, h * HEAD_DIM:(h + 1) * HEAD_DIM] * scale)
    row_pos = pos0 + lax.broadcasted_iota(jnp.int32, (rows, 1), 0) % tq
    c_idx = lax.broadcasted_iota(jnp.int32, (1, ncp), 1)
    c_valid = (c_idx * CMP_STRIDE + (CMP_BLOCK - 1) <= row_pos) & (c_idx < n_cmp)
    s = jnp.where(c_valid, _nt(qbd_scr[...].astype(BF16), kc_ref[0].astype(BF16)), NEG_INF)
    m = jnp.max(s, axis=-1, keepdims=True)
    p = jnp.where(c_valid, jnp.exp(s - m), 0.0)
    l = jnp.sum(p, axis=-1, keepdims=True)
    pc = jnp.where(l > 0.0, p / jnp.where(l > 0.0, l, 1.0), 0.0)
    o = _mm(pc.astype(BF16), vc_ref[0].astype(BF16))
    for h in range(NSA_HEADS):
        g, r = divmod(h, NSA_REP)
        c = h * NSA_BRANCHES
        o_ref[0, :, h * HEAD_DIM:(h + 1) * HEAD_DIM] = (
            o[r * gt + g * tq:r * gt + (g + 1) * tq, g * HEAD_DIM:(g + 1) * HEAD_DIM] * gate_ref[0, :, c:c + 1])
    pc_sum = pc[0:gt]
    for r in range(1, NSA_REP):
        pc_sum = pc_sum + pc[r * gt:(r + 1) * gt]
    cj = lax.broadcasted_iota(jnp.int32, (ncp, nselp), 0) * CMP_STRIDE
    sj = lax.broadcasted_iota(jnp.int32, (ncp, nselp), 1) * SEL_BLOCK
    overlap = ((cj < sj + SEL_BLOCK) & (cj + (CMP_BLOCK - 1) >= sj)).astype(BF16)
    terms = _mm(jnp.concatenate(_bf16_terms(pc_sum), axis=0), overlap)
    imp = (terms[0:gt] + terms[gt:2 * gt]) + terms[2 * gt:3 * gt]
    gpos = pos0 + lax.broadcasted_iota(jnp.int32, (gt, 1), 0) % tq
    j_idx = lax.broadcasted_iota(jnp.int32, (1, nselp), 1)
    forced = (j_idx == 0) | (j_idx == gpos // SEL_BLOCK)
    valid = j_idx * SEL_BLOCK <= gpos
    score = jnp.where(valid, jnp.where(forced, BIG, imp), -BIG)
    rank = jnp.zeros(score.shape, jnp.int32)
    for i in range(n_sel):
        si = score[:, i:i + 1]
        ahead = (si > score) | ((si == score) & (i < j_idx))
        rank = rank + ahead.astype(jnp.int32)
    sel_ref[0] = ((rank < SEL_TOPK) & valid).astype(F32)


def _cmp_select(qn, kcmp, vcmp, gates, n_cmp, n_sel, nselp, pos0):
    bx, t, wq = qn.shape
    ncp = kcmp.shape[1]
    assert (NSA_GROUPS * t) % (2 * SUBLANE) == 0
    whole = lambda b: (b, 0, 0)
    cmp_spec = pl.BlockSpec((1, ncp, NSA_KV_WIDTH), whole)
    return pl.pallas_call(
        functools.partial(_cmp_select_body, tq=t, n_cmp=n_cmp, n_sel=n_sel, pos0=pos0, scale=HEAD_DIM ** -0.5),
        grid=(bx,),
        in_specs=[pl.BlockSpec((1, t, wq), whole), cmp_spec, cmp_spec, pl.BlockSpec((1, t, LANE), whole)],
        out_specs=[pl.BlockSpec((1, t, wq), whole), pl.BlockSpec((1, NSA_GROUPS * t, nselp), whole)],
        out_shape=[jax.ShapeDtypeStruct((bx, t, wq), F32),
                   jax.ShapeDtypeStruct((bx, NSA_GROUPS * t, nselp), F32)],
        scratch_shapes=[pltpu.VMEM((NSA_HEADS * t, NSA_KV_WIDTH), F32)],
        compiler_params=_params("arbitrary"),
        name="cmp_select",
    )(qn, kcmp, vcmp, gates)


def _cmp_select_t_body(q_ref, kc_ref, vc_ref, gate_ref, o_ref, sel_ref, *, tq, n_cmp, n_sel, pos0, scale):
    qi = pl.program_id(1)
    ncp = kc_ref.shape[2]
    nselp = sel_ref.shape[2]
    qpos = pos0 + qi * tq + lax.broadcasted_iota(jnp.int32, (1, tq), 1)
    c_idx = lax.broadcasted_iota(jnp.int32, (ncp, 1), 0)
    c_valid = (c_idx * CMP_STRIDE + (CMP_BLOCK - 1) <= qpos) & (c_idx < n_cmp)
    valid_w = jnp.concatenate([c_valid] * NSA_REP, axis=1)
    sj = lax.broadcasted_iota(jnp.int32, (nselp, ncp), 0) * SEL_BLOCK
    cj = lax.broadcasted_iota(jnp.int32, (nselp, ncp), 1) * CMP_STRIDE
    overlap_t = ((cj < sj + SEL_BLOCK) & (cj + (CMP_BLOCK - 1) >= sj)).astype(F32)
    j_idx = lax.broadcasted_iota(jnp.int32, (nselp, 1), 0)
    forced = (j_idx == 0) | (j_idx == qpos // SEL_BLOCK)
    valid = j_idx * SEL_BLOCK <= qpos
    low = _half_mask((tq, LANE), 0)
    pairs = NSA_REP // GROUPS_PER_TILE
    for g in range(NSA_GROUPS):
        kc = kc_ref[0, g].astype(BF16)
        vc_t = vc_ref[0, g].T.astype(BF16)
        q_t = jnp.concatenate([(q_ref[0, :, (g * pairs + p) * LANE:(g * pairs + p + 1) * LANE] * scale).T
                               for p in range(pairs)], axis=0).astype(BF16)
        q_wide = jnp.concatenate([q_t[r * HEAD_DIM:(r + 1) * HEAD_DIM] for r in range(NSA_REP)], axis=1)
        s = jnp.where(valid_w, _mm(kc, q_wide), NEG_INF)
        m = jnp.max(s, axis=0, keepdims=True)
        p = jnp.where(valid_w, jnp.exp(s - m), 0.0)
        l = jnp.sum(p, axis=0, keepdims=True)
        pc = jnp.where(l > 0.0, p / jnp.where(l > 0.0, l, 1.0), 0.0)
        o_t = _mm(vc_t, pc.astype(BF16))
        pc_sum = pc[:, 0:tq]
        for r in range(1, NSA_REP):
            pc_sum = pc_sum + pc[:, r * tq:(r + 1) * tq]
        for pair in range(pairs):
            h0 = g * NSA_REP + pair * GROUPS_PER_TILE
            c0, c1 = h0 * NSA_BRANCHES, (h0 + 1) * NSA_BRANCHES
            r0 = pair * GROUPS_PER_TILE
            o_pair = jnp.concatenate([o_t[:, r0 * tq:(r0 + 1) * tq], o_t[:, (r0 + 1) * tq:(r0 + 2) * tq]],
                                     axis=0).T
            o_ref[0, :, (g * pairs + pair) * LANE:(g * pairs + pair + 1) * LANE] = o_pair * jnp.where(
                low, gate_ref[0, :, c0:c0 + 1], gate_ref[0, :, c1:c1 + 1])
        imp = jnp.dot(overlap_t, pc_sum, preferred_element_type=F32, precision=lax.Precision.HIGHEST)
        score = jnp.where(valid, jnp.where(forced, BIG, imp), -BIG)
        rank = jnp.zeros((nselp, tq), jnp.int32)
        for i in range(n_sel):
            si = score[i:i + 1, :]
            ahead = (si > score) | ((si == score) & (i < j_idx))
            rank = rank + ahead.astype(jnp.int32)
        sel_ref[0, g] = ((rank < SEL_TOPK) & valid).astype(F32)


def _cmp_select_t(qn, kcmp, vcmp, gates, n_cmp, n_sel, pos0, tq):
    bx, t, wq = qn.shape
    ncp = kcmp.shape[2]
    nselp = -(-n_sel // SUBLANE) * SUBLANE
    cmp_spec = pl.BlockSpec((1, NSA_GROUPS, ncp, HEAD_DIM), lambda b, i: (b, 0, 0, 0))
    return pl.pallas_call(
        functools.partial(_cmp_select_t_body, tq=tq, n_cmp=n_cmp, n_sel=n_sel, pos0=pos0, scale=HEAD_DIM ** -0.5),
        grid=(bx, t // tq),
        in_specs=[pl.BlockSpec((1, tq, wq), lambda b, i: (b, i, 0)), cmp_spec, cmp_spec,
                  pl.BlockSpec((1, tq, LANE), lambda b, i: (b, i, 0))],
        out_specs=[pl.BlockSpec((1, tq, wq), lambda b, i: (b, i, 0)),
                   pl.BlockSpec((1, NSA_GROUPS, nselp, tq), lambda b, i: (b, 0, 0, i))],
        out_shape=[jax.ShapeDtypeStruct((bx, t, wq), F32),
                   jax.ShapeDtypeStruct((bx, NSA_GROUPS, nselp, t), F32)],
        compiler_params=_params("arbitrary", "arbitrary"),
        name="cmp_select_t",
    )(qn, kcmp, vcmp, gates)


def _pages_per_step(n_pages, want=PAGES_PER_STEP):
    pps = min(want, n_pages)
    assert n_pages % pps == 0
    return pps


def _page_map(n_pages, pps, i, rank):
    return lambda b, p, pt: (pt[b * n_pages + p * pps + i],) + (0,) * (rank - 1)


def _softmax_step(s, vis, m_scr, l_scr, acc_scr, pv_fn):
    m_prev = m_scr[...]
    m_new = jnp.maximum(m_prev, jnp.max(s, axis=-1, keepdims=True))
    alpha = jnp.exp(m_prev - m_new)
    p = jnp.exp(s - m_new)
    if vis is not None:
        p = jnp.where(vis, p, 0.0)
    l_scr[...] = alpha * l_scr[...] + jnp.sum(p, axis=-1, keepdims=True)
    acc_scr[...] = alpha * acc_scr[...] + pv_fn(p)
    m_scr[...] = m_new


def _new_rows_step(qbd, k_new, v_new, row_t, bias_cols, m_scr, l_scr, acc_scr, tnew):
    cols = []
    for j in range(tnew):
        sj = jnp.sum(qbd * k_new[j:j + 1, :], axis=-1, keepdims=True)
        if bias_cols is not None:
            sj = sj - bias_cols[j]
        cols.append(jnp.where(row_t >= j, sj, NEG_INF))
    m_prev = m_scr[...]
    m_new = m_prev
    for sj in cols:
        m_new = jnp.maximum(m_new, sj)
    alpha = jnp.exp(m_prev - m_new)
    l = alpha * l_scr[...]
    acc = alpha * acc_scr[...]
    for j, sj in enumerate(cols):
        pj = jnp.where(row_t >= j, jnp.exp(sj - m_new), 0.0)
        l = l + pj
        acc = acc + pj * v_new[j:j + 1, :]
    return acc / l


def _init_softmax(m_scr, l_scr, acc_scr):
    m_scr[...] = jnp.full(m_scr.shape, NEG_INF, F32)
    l_scr[...] = jnp.zeros(l_scr.shape, F32)
    acc_scr[...] = jnp.zeros(acc_scr.shape, F32)


def _fox_sample_body(pt_ref, q_ref, kn_ref, vn_ref, fn_ref, bf_ref, *rest, n_steps, pps, tnew, scale):
    del pt_ref
    kt_refs, vt_refs, lf_refs = rest[:pps], rest[pps:2 * pps], rest[2 * pps:3 * pps]
    o_ref, lfo_ref, qbd_scr, m_scr, l_scr, acc_scr, carry_scr = rest[3 * pps:]
    p = pl.program_id(1)
    nh, w = FOX_HEADS, FOX_WIDTH
    rows = tnew * nh
    head_of_lane = lax.broadcasted_iota(jnp.int32, (nh, w), 1) // HEAD_DIM
    hmask = (head_of_lane == lax.broadcasted_iota(jnp.int32, (nh, w), 0)).astype(F32)

    @pl.when(p == 0)
    def _():
        for t in range(tnew):
            qbd_scr[t * nh:(t + 1) * nh, :] = q_ref[0, t:t + 1, :] * scale * hmask
        _init_softmax(m_scr, l_scr, acc_scr)
        carry_scr[...] = jnp.zeros(carry_scr.shape, F32)

    tri = (lax.broadcasted_iota(jnp.int32, (PAGE, PAGE), 0) <= lax.broadcasted_iota(jnp.int32, (PAGE, PAGE), 1)
           ).astype(BF16)
    lf_all = jnp.concatenate([r[0] for r in lf_refs], axis=0)
    n_lf = pps * nh
    terms = _mm(jnp.concatenate(_bf16_terms(lf_all), axis=0), tri)
    within_all = (terms[0:n_lf] + terms[n_lf:2 * n_lf]) + terms[2 * n_lf:3 * n_lf]
    carry = carry_scr[...]
    cums = []
    for i in range(pps):
        within = within_all[i * nh:(i + 1) * nh]
        cums.append(carry + within)
        carry = carry + within[:, PAGE - 1:PAGE]
    carry_scr[...] = carry
    bias = jnp.concatenate([jnp.concatenate(cums, axis=1)] * tnew, axis=0)
    kt = jnp.concatenate([r[0].reshape(w, PAGE).astype(BF16) for r in kt_refs], axis=1)
    vt = jnp.concatenate([r[0].reshape(w, PAGE).astype(BF16) for r in vt_refs], axis=1)
    s = _mm(qbd_scr[...].astype(BF16), kt) - bias
    _softmax_step(s, None, m_scr, l_scr, acc_scr, lambda pr: _nt(pr.astype(BF16), vt))

    @pl.when(p == n_steps - 1)
    def _():
        lf_new = jax.nn.log_sigmoid(fn_ref[0] + bf_ref[...])
        lfo_ref[0] = lf_new
        run = carry_scr[...]
        bias_cols = []
        for j in range(tnew):
            run = run + lf_new[:, j:j + 1]
            bias_cols.append(jnp.concatenate([run] * tnew, axis=0))
        row_t = lax.broadcasted_iota(jnp.int32, (rows, 1), 0) // nh
        o = _new_rows_step(qbd_scr[...], kn_ref[0], vn_ref[0], row_t, bias_cols, m_scr, l_scr, acc_scr, tnew)
        o = o * jnp.concatenate([hmask] * tnew, axis=0)
        for t in range(tnew):
            o_ref[0, t:t + 1, :] = jnp.sum(o[t * nh:(t + 1) * nh], axis=0, keepdims=True)


def _fox_sample(q, k_new, v_new, f_new, b_fgate, cache_kt, cache_vt, cache_lft, page_table):
    bx, tnew, w = q.shape
    n_pages = page_table.shape[1]
    pps = _pages_per_step(n_pages, ATTN_PAGES_PER_STEP)
    nh = FOX_HEADS
    rows = tnew * nh
    new_spec = pl.BlockSpec((1, tnew, w), lambda b, p, pt: (b, 0, 0))
    kv_specs = [pl.BlockSpec((1, nh, HEAD_DIM, PAGE), _page_map(n_pages, pps, i, 4)) for i in range(pps)]
    lf_specs = [pl.BlockSpec((1, nh, PAGE), _page_map(n_pages, pps, i, 3)) for i in range(pps)]
    return pl.pallas_call(
        functools.partial(_fox_sample_body, n_steps=n_pages // pps, pps=pps, tnew=tnew, scale=HEAD_DIM ** -0.5),
        grid_spec=pltpu.PrefetchScalarGridSpec(
            num_scalar_prefetch=1, grid=(bx, n_pages // pps),
            in_specs=[new_spec, new_spec, new_spec,
                      pl.BlockSpec((1, nh, tnew), lambda b, p, pt: (b, 0, 0)),
                      pl.BlockSpec((nh, 1), lambda b, p, pt: (0, 0))] + kv_specs + kv_specs + lf_specs,
            out_specs=[new_spec, pl.BlockSpec((1, nh, tnew), lambda b, p, pt: (b, 0, 0))],
            scratch_shapes=[pltpu.VMEM((rows, w), F32), pltpu.VMEM((rows, 1), F32), pltpu.VMEM((rows, 1), F32),
                            pltpu.VMEM((rows, w), F32), pltpu.VMEM((nh, 1), F32)]),
        out_shape=[jax.ShapeDtypeStruct((bx, tnew, w), F32), jax.ShapeDtypeStruct((bx, nh, tnew), F32)],
        compiler_params=_params("arbitrary", "arbitrary"),
        name="fox_sample",
    )(page_table.reshape(-1), q, k_new, v_new, f_new, b_fgate.reshape(nh, 1),
      *([cache_kt] * pps), *([cache_vt] * pps), *([cache_lft] * pps))


def _fill_group_queries(qbd_scr, q_ref, tnew, scale):
    qbd_scr[...] = jnp.zeros(qbd_scr.shape, F32)
    for h in range(NSA_HEADS):
        g = h // NSA_REP
        qbd_scr[h * tnew:(h + 1) * tnew, g * HEAD_DIM:(g + 1) * HEAD_DIM] = (
            q_ref[0, :, h * HEAD_DIM:(h + 1) * HEAD_DIM] * scale)


def _write_group_heads(o_ref, o, gate_ref, branch, tnew):
    for h in range(NSA_HEADS):
        g = h // NSA_REP
        c = h * NSA_BRANCHES + branch
        o_ref[0, :, h * HEAD_DIM:(h + 1) * HEAD_DIM] = (
            o[h * tnew:(h + 1) * tnew, g * HEAD_DIM:(g + 1) * HEAD_DIM] * gate_ref[0, :, c:c + 1])


def _sel_sample_body(pt_ref, q_ref, kn_ref, vn_ref, selrows_ref, gate_ref, *rest, n_steps, pps, tnew, scale):
    del pt_ref
    kt_refs, vt_refs = rest[:pps], rest[pps:2 * pps]
    o_ref, qbd_scr, m_scr, l_scr, acc_scr = rest[2 * pps:]
    p = pl.program_id(1)
    rows = NSA_HEADS * tnew
    nselp = selrows_ref.shape[-1]
    keys = pps * PAGE

    @pl.when(p == 0)
    def _():
        _fill_group_queries(qbd_scr, q_ref, tnew, scale)
        _init_softmax(m_scr, l_scr, acc_scr)

    blk_of_lane = (p * keys + lax.broadcasted_iota(jnp.int32, (nselp, keys), 1)) // SEL_BLOCK
    expand = (lax.broadcasted_iota(jnp.int32, (nselp, keys), 0) == blk_of_lane).astype(BF16)
    vis = _mm(selrows_ref[0].astype(BF16), expand) > 0.5
    kt = jnp.concatenate([r[0].reshape(NSA_KV_WIDTH, PAGE).astype(BF16) for r in kt_refs], axis=1)
    vt = jnp.concatenate([r[0].reshape(NSA_KV_WIDTH, PAGE).astype(BF16) for r in vt_refs], axis=1)
    s = jnp.where(vis, _mm(qbd_scr[...].astype(BF16), kt), NEG_INF)
    _softmax_step(s, vis, m_scr, l_scr, acc_scr, lambda pr: _nt(pr.astype(BF16), vt))

    @pl.when(p == n_steps - 1)
    def _():
        row_t = lax.broadcasted_iota(jnp.int32, (rows, 1), 0) % tnew
        o = _new_rows_step(qbd_scr[...], kn_ref[0], vn_ref[0], row_t, None, m_scr, l_scr, acc_scr, tnew)
        _write_group_heads(o_ref, o, gate_ref, 1, tnew)


def _sel_sample(q, k_new, v_new, selrows, gates, cache_kt, cache_vt, page_table):
    bx, tnew, wq = q.shape
    n_pages = page_table.shape[1]
    rows = NSA_HEADS * tnew
    wk = NSA_KV_WIDTH
    pps = _pages_per_step(n_pages, ATTN_PAGES_PER_STEP)
    fixed = lambda b, p, pt: (b, 0, 0)
    kv_specs = [pl.BlockSpec((1, NSA_GROUPS, HEAD_DIM, PAGE), _page_map(n_pages, pps, i, 4)) for i in range(pps)]
    return pl.pallas_call(
        functools.partial(_sel_sample_body, n_steps=n_pages // pps, pps=pps, tnew=tnew, scale=HEAD_DIM ** -0.5),
        grid_spec=pltpu.PrefetchScalarGridSpec(
            num_scalar_prefetch=1, grid=(bx, n_pages // pps),
            in_specs=[pl.BlockSpec((1, tnew, wq), fixed), pl.BlockSpec((1, tnew, wk), fixed),
                      pl.BlockSpec((1, tnew, wk), fixed), pl.BlockSpec((1, rows, selrows.shape[-1]), fixed),
                      pl.BlockSpec((1, tnew, LANE), fixed)] + kv_specs + kv_specs,
            out_specs=pl.BlockSpec((1, tnew, wq), fixed),
            scratch_shapes=[pltpu.VMEM((rows, wk), F32), pltpu.VMEM((rows, 1), F32), pltpu.VMEM((rows, 1), F32),
                            pltpu.VMEM((rows, wk), F32)]),
        out_shape=jax.ShapeDtypeStruct((bx, tnew, wq), F32),
        compiler_params=_params("arbitrary", "arbitrary"),
        name="sel_sample",
    )(page_table.reshape(-1), q, k_new, v_new, selrows, gates, *([cache_kt] * pps), *([cache_vt] * pps))


def _win_sample_body(q_ref, kn_ref, vn_ref, gate_ref, kt_ref, vt_ref, o_ref, qbd_scr, m_scr, l_scr, acc_scr,
                     *, tnew, wbuf, scale):
    rows = NSA_HEADS * tnew
    _fill_group_queries(qbd_scr, q_ref, tnew, scale)
    _init_softmax(m_scr, l_scr, acc_scr)
    row_t = lax.broadcasted_iota(jnp.int32, (rows, 1), 0) % tnew
    vis = lax.broadcasted_iota(jnp.int32, (rows, wbuf), 1) > row_t + (wbuf - WINDOW)
    kt = kt_ref[0].reshape(NSA_KV_WIDTH, wbuf).astype(BF16)
    vt = vt_ref[0].reshape(NSA_KV_WIDTH, wbuf).astype(BF16)
    s = jnp.where(vis, _mm(qbd_scr[...].astype(BF16), kt), NEG_INF)
    _softmax_step(s, vis, m_scr, l_scr, acc_scr, lambda pr: _nt(pr.astype(BF16), vt))
    o = _new_rows_step(qbd_scr[...], kn_ref[0], vn_ref[0], row_t, None, m_scr, l_scr, acc_scr, tnew)
    _write_group_heads(o_ref, o, gate_ref, 2, tnew)


def _win_sample(q, k_new, v_new, gates, buf_kt, buf_vt):
    bx, tnew, wq = q.shape
    wbuf = buf_kt.shape[-1]
    rows = NSA_HEADS * tnew
    wk = NSA_KV_WIDTH
    fixed = lambda b: (b, 0, 0)
    buf = pl.BlockSpec((1, NSA_GROUPS, HEAD_DIM, wbuf), lambda b: (b, 0, 0, 0))
    return pl.pallas_call(
        functools.partial(_win_sample_body, tnew=tnew, wbuf=wbuf, scale=HEAD_DIM ** -0.5),
        grid=(bx,),
        in_specs=[pl.BlockSpec((1, tnew, wq), fixed), pl.BlockSpec((1, tnew, wk), fixed),
                  pl.BlockSpec((1, tnew, wk), fixed), pl.BlockSpec((1, tnew, LANE), fixed), buf, buf],
        out_specs=pl.BlockSpec((1, tnew, wq), fixed),
        out_shape=jax.ShapeDtypeStruct((bx, tnew, wq), F32),
        scratch_shapes=[pltpu.VMEM((rows, wk), F32), pltpu.VMEM((rows, 1), F32), pltpu.VMEM((rows, 1), F32),
                        pltpu.VMEM((rows, wk), F32)],
        compiler_params=_params("arbitrary"),
        name="win_sample",
    )(q, k_new, v_new, gates, buf_kt, buf_vt)


def _tile_heads(v, n):
    return jnp.tile(v.astype(F32), n)


def _mix_ab(x, mods, pos0, past, p, tiles):
    y, f_t = _project(x, p["norm_mix"], mods["sc1"], mods["sh1"], p["w_in_main"], tiles["tm"], w_t=p["w_in_f_t"])
    gains = jnp.concatenate([_tile_heads(p["ab_q_norm"], FOX_HEADS), _tile_heads(p["ab_k_norm"], FOX_HEADS)])
    n_tiles = 2 * FOX_WIDTH // LANE
    (qk,) = _headnorm(y, 2 * FOX_WIDTH, gains.reshape(1, -1), [(c, 0, c) for c in range(n_tiles)],
                      [2 * FOX_WIDTH], tiles["tm_norm"])
    return y, f_t, qk


def _nsa_project(x, mods, pos_rows, p, tiles):
    n_gate = NSA_BRANCHES * NSA_HEADS
    y, gl = _project(x, p["norm_mix"], mods["sc1"], mods["sh1"], p["w_in_main"], tiles["tm"], w_side=p["w_in_gate"])
    kn3 = p["nsa_k_norm"]
    qw = NSA_HEADS * HEAD_DIM
    gains = jnp.concatenate([
        _tile_heads(p["nsa_q_norm"], NSA_HEADS),
        _tile_heads(kn3[0], NSA_GROUPS), jnp.ones((NSA_KV_WIDTH,), F32),
        _tile_heads(kn3[1], NSA_GROUPS), jnp.ones((NSA_KV_WIDTH,), F32),
        _tile_heads(kn3[2], NSA_GROUPS)])
    width = gains.shape[0]
    qt = qw // LANE
    kt = NSA_KV_WIDTH // LANE
    tile_map = [(c, 0, c) for c in range(qt)]
    for i in range(NSA_BRANCHES):
        tile_map += [(qt + 2 * i * kt + c, 1, i * kt + c) for c in range(kt)]
    b_gate = jnp.pad(p["nsa_b_gate"], (0, LANE - n_gate)).reshape(1, LANE)
    qn, kn, gates = _headnorm(y, width, gains.reshape(1, -1), tile_map, [qw, NSA_BRANCHES * NSA_KV_WIDTH],
                              tiles["tm_norm"], rope_tabs=_rope_tables(pos_rows), gate_logits=gl, gate_bias=b_gate)
    return y, qn, kn, gates


def _layer_params(params, layer):
    e = layer // 2
    p = {"norm_mix": params["norm_mix"][layer], "norm_ffn": params["norm_ffn"][layer],
         "w_up": params["w_up"][layer].astype(BF16), "w_down": params["w_down"][layer].astype(BF16)}
    prefix = "ab_" if layer % 2 == 0 else "nsa_"
    for k, v in params.items():
        if k.startswith(prefix):
            p[k] = v[e]
    if layer % 2 == 0:
        w_in = p["ab_w_in"]
        split_f = 3 * FOX_WIDTH
        p["w_in_main"] = jnp.concatenate([w_in[:, :split_f], w_in[:, split_f + FOX_HEADS:]], axis=1).astype(BF16)
        p["w_in_f_t"] = w_in[:, split_f:split_f + FOX_HEADS].T.astype(BF16)
        p["w_out"] = p["ab_w_out"].astype(BF16)
    else:
        w_in = p["nsa_w_in"]
        n_main = NSA_HEADS * HEAD_DIM + 6 * NSA_KV_WIDTH
        n_gate = NSA_BRANCHES * NSA_HEADS
        p["w_in_main"] = w_in[:, :n_main].astype(BF16)
        p["w_in_gate"] = jnp.pad(w_in[:, n_main:], ((0, 0), (0, LANE - n_gate))).astype(BF16)
        p["w_out"] = p["nsa_w_out"].astype(BF16)
    return p


def _heads(a, n):
    return a.reshape(a.shape[0], a.shape[1], n, HEAD_DIM)


def _prompt_trunk(x, mod, layers):
    bx, t, d = x.shape
    tiles = {"tm": 512, "tm_norm": 256}
    states = {}
    for layer, p in enumerate(layers):
        sh1, sc1, g1, sh2, sc2, g2 = [m[:, None, :] for m in jnp.split(mod[layer], 6, axis=-1)]
        mods = {"sc1": sc1, "sh1": sh1}
        if layer % 2 == 0:
            y, f_t, qk = _mix_ab(x, mods, 0, None, p, tiles)
            lf_t, cum = _logf_cumsum(f_t, p["ab_b_fgate"])
            o_fox = _fox_flash(qk, y, 2, cum, 512)
            o_pool = _pool_mixer(y, 3, jnp.zeros((bx, POOL_HALO, POOL_WIDTH), F32), p["ab_pool_map"],
                                 p["ab_pool_scale"], 0, 512)
            x = _out_project([[o_fox], [o_pool]], p["w_out"], x, g1, tiles["tm"])
            states["fox_k"] = _heads(qk[:, :, FOX_WIDTH:], FOX_HEADS)
            states["fox_v"] = _heads(y[:, :, 2 * FOX_WIDTH:3 * FOX_WIDTH], FOX_HEADS)
            states["fox_logf"] = lf_t.transpose(0, 2, 1)
            states["pool"] = y[:, t - (POOL_HALO - 1):, 3 * FOX_WIDTH:]
        else:
            y, qn, kn, gates = _nsa_project(x, mods, jnp.arange(t), p, tiles)
            qw, kw_ = NSA_HEADS * HEAD_DIM, NSA_KV_WIDTH
            n_chunk = t // CMP_STRIDE
            kcmp = _compress(_chunk_rows(kn, 0, t), p["nsa_cmp_pos_k"], p["nsa_cmp_w1_k"], p["nsa_cmp_w2_k"])
            vcmp = _compress(_chunk_rows(y, (qw + kw_) // kw_, t), p["nsa_cmp_pos_v"], p["nsa_cmp_w1_v"],
                             p["nsa_cmp_w2_v"])
            n_sel = -(-t // SEL_BLOCK)
            o_cmp, sel_t = _cmp_select_t(qn, kcmp, vcmp, gates, n_chunk - 1, n_sel, 0, 256)
            o_sel = _nsa_flash(qn, kn, 1, y, (qw + 3 * kw_) // kw_, gates, 1, 256, "causal", sel_t=sel_t)
            o_win = _nsa_flash(qn, kn, 2, y, (qw + 5 * kw_) // kw_, gates, 2, 256, "window")
            x = _out_project([[o_cmp, o_sel, o_win]], p["w_out"], x, g1, tiles["tm"])
            buf = min(WINDOW, t)
            states["nsa_kc"] = _heads(kn[:, :, :kw_], NSA_GROUPS)
            states["nsa_vc"] = _heads(y[:, :, qw + kw_:qw + 2 * kw_], NSA_GROUPS)
            states["nsa_ks"] = _heads(kn[:, :, kw_:2 * kw_], NSA_GROUPS)
            states["nsa_vs"] = _heads(y[:, :, qw + 3 * kw_:qw + 4 * kw_], NSA_GROUPS)
            states["nsa_kw"] = _heads(kn[:, t - buf:, 2 * kw_:], NSA_GROUPS)
            states["nsa_vw"] = _heads(y[:, t - buf:, qw + 5 * kw_:qw + 6 * kw_], NSA_GROUPS)
        x = _ffn(x, p["norm_ffn"], sc2, sh2, g2, p["w_up"], p["w_down"], tiles["tm"])
    return x, states


def _sample_trunk(x, mod, layers, past, page_table):
    bx, tnew, d = x.shape
    rows = bx * tnew
    n_pages = page_table.shape[1]
    pos0 = n_pages * PAGE
    assert tnew < CMP_STRIDE and pos0 % CMP_STRIDE == 0 and pos0 >= WINDOW
    tiles = {"tm": rows, "tm_norm": rows}
    xf = x.reshape(1, rows, d)
    per_batch = lambda a: a.reshape(bx, tnew, a.shape[-1])
    states = {}
    for layer, p in enumerate(layers):
        e = layer // 2
        sh1, sc1, g1, sh2, sc2, g2 = [jnp.repeat(m, tnew, axis=0)[None] for m in jnp.split(mod[layer], 6, axis=-1)]
        mods = {"sc1": sc1, "sh1": sh1}
        if layer % 2 == 0:
            y, f_t, qk = _mix_ab(xf, mods, pos0, None, p, tiles)
            q_s, k_s = per_batch(qk[0, :, :FOX_WIDTH]), per_batch(qk[0, :, FOX_WIDTH:])
            v_s = per_batch(y[0, :, 2 * FOX_WIDTH:3 * FOX_WIDTH])
            u_s = per_batch(y[0, :, 3 * FOX_WIDTH:])
            f_new = f_t[0].reshape(FOX_HEADS, bx, tnew).transpose(1, 0, 2)
            cache_kt = past["cache_fox_k"][e].transpose(0, 2, 3, 1)
            cache_vt = past["cache_fox_v"][e].transpose(0, 2, 3, 1)
            cache_lft = past["cache_fox_logf"][e].transpose(0, 2, 1)
            o_fox, lf_new = _fox_sample(q_s, k_s, v_s, f_new, p["ab_b_fgate"], cache_kt, cache_vt, cache_lft,
                                        page_table)
            pool_prev = past["state_pool"][e]
            prefix = jnp.pad(pool_prev, ((0, 0), (1, 0), (0, 0)))
            o_pool = _pool_mixer(u_s, 0, prefix, p["ab_pool_map"], p["ab_pool_scale"], pos0, tnew)
            xf = _out_project([[o_fox.reshape(1, rows, -1)], [o_pool.reshape(1, rows, -1)]], p["w_out"], xf, g1, rows)
            states["fox_k"] = _heads(k_s, FOX_HEADS)
            states["fox_v"] = _heads(v_s, FOX_HEADS)
            states["fox_logf"] = lf_new.transpose(0, 2, 1)
            states["pool"] = jnp.concatenate([pool_prev, u_s], axis=1)[:, -(POOL_HALO - 1):]
        else:
            pos_rows = pos0 + jnp.arange(rows) % tnew
            y, qn, kn, gates = _nsa_project(xf, mods, pos_rows, p, tiles)
            qw, kw_ = NSA_HEADS * HEAD_DIM, NSA_KV_WIDTH
            q_s, gates_s = per_batch(qn[0]), per_batch(gates[0])
            kn_s, y_s = per_batch(kn[0]), per_batch(y[0])
            kc_s, ks_s, kwn_s = kn_s[..., :kw_], kn_s[..., kw_:2 * kw_], kn_s[..., 2 * kw_:]
            vc_s, vs_s, vwn_s = (y_s[..., qw + kw_:qw + 2 * kw_], y_s[..., qw + 3 * kw_:qw + 4 * kw_],
                                 y_s[..., qw + 5 * kw_:qw + 6 * kw_])
            d_major = lambda a: a.transpose(0, 2, 3, 1)
            kcmp = _compress_pages(d_major(past["cache_nsa_kc"][e]), page_table, p["nsa_cmp_pos_k"],
                                   p["nsa_cmp_w1_k"], p["nsa_cmp_w2_k"])
            vcmp = _compress_pages(d_major(past["cache_nsa_vc"][e]), page_table, p["nsa_cmp_pos_v"],
                                   p["nsa_cmp_w1_v"], p["nsa_cmp_w2_v"])
            total = pos0 + tnew
            n_cmp = total // CMP_STRIDE - 1
            n_sel = -(-total // SEL_BLOCK)
            nselp = -(-n_sel // (2 * LANE)) * (2 * LANE)
            o_cmp, sel = _cmp_select(q_s, kcmp, vcmp, gates_s, n_cmp, n_sel, nselp, pos0)
            selrows = jnp.repeat(sel.reshape(bx, NSA_GROUPS, tnew, nselp), NSA_REP, axis=1).reshape(
                bx, NSA_HEADS * tnew, nselp)
            o_sel = _sel_sample(q_s, ks_s, vs_s, selrows, gates_s, d_major(past["cache_nsa_ks"][e]),
                                d_major(past["cache_nsa_vs"][e]), page_table)
            kw_prev, vw_prev = past["state_nsa_kw"][e], past["state_nsa_vw"][e]
            o_win = _win_sample(q_s, kwn_s, vwn_s, gates_s, d_major(kw_prev), d_major(vw_prev))
            flat = lambda a: a.reshape(1, rows, -1)
            xf = _out_project([[flat(o_cmp), flat(o_sel), flat(o_win)]], p["w_out"], xf, g1, rows)
            buf = kw_prev.shape[1]
            states["nsa_kc"] = _heads(kc_s, NSA_GROUPS)
            states["nsa_vc"] = _heads(vc_s, NSA_GROUPS)
            states["nsa_ks"] = _heads(ks_s, NSA_GROUPS)
            states["nsa_vs"] = _heads(vs_s, NSA_GROUPS)
            states["nsa_kw"] = jnp.concatenate([kw_prev, _heads(kwn_s, NSA_GROUPS)], axis=1)[:, -buf:]
            states["nsa_vw"] = jnp.concatenate([vw_prev, _heads(vwn_s, NSA_GROUPS)], axis=1)[:, -buf:]
        xf = _ffn(xf, p["norm_ffn"], sc2, sh2, g2, p["w_up"], p["w_down"], rows)
    return xf.reshape(bx, tnew, d), states


_STATE_NAMES = ("fox_k", "fox_v", "fox_logf", "pool", "nsa_kc", "nsa_vc", "nsa_ks", "nsa_vs", "nsa_kw", "nsa_vw")


def kernel(x_prompt, x_sample, cache_fox_k, cache_fox_v, cache_fox_logf, state_pool, cache_nsa_kc, cache_nsa_vc,
           cache_nsa_ks, cache_nsa_vs, state_nsa_kw, state_nsa_vw, page_table, c_prompt, c_sample, w_mod, b_mod,
           norm_mix, norm_ffn, w_up, w_down, ab_w_in, ab_b_fgate, ab_q_norm, ab_k_norm, ab_pool_map, ab_pool_scale,
           ab_w_out, nsa_w_in, nsa_b_gate, nsa_q_norm, nsa_k_norm, nsa_cmp_pos_k, nsa_cmp_w1_k, nsa_cmp_w2_k,
           nsa_cmp_pos_v, nsa_cmp_w1_v, nsa_cmp_w2_v, nsa_w_out):
    assert w_mod.shape[0] == 2, "one forgetting/pooling layer followed by one sparse-attention layer"
    params = {
        "norm_mix": norm_mix, "norm_ffn": norm_ffn, "w_up": w_up, "w_down": w_down,
        "ab_w_in": ab_w_in, "ab_b_fgate": ab_b_fgate, "ab_q_norm": ab_q_norm, "ab_k_norm": ab_k_norm,
        "ab_pool_map": ab_pool_map, "ab_pool_scale": ab_pool_scale, "ab_w_out": ab_w_out,
        "nsa_w_in": nsa_w_in, "nsa_b_gate": nsa_b_gate, "nsa_q_norm": nsa_q_norm, "nsa_k_norm": nsa_k_norm,
        "nsa_cmp_pos_k": nsa_cmp_pos_k, "nsa_cmp_w1_k": nsa_cmp_w1_k, "nsa_cmp_w2_k": nsa_cmp_w2_k,
        "nsa_cmp_pos_v": nsa_cmp_pos_v, "nsa_cmp_w1_v": nsa_cmp_w1_v, "nsa_cmp_w2_v": nsa_cmp_w2_v,
        "nsa_w_out": nsa_w_out,
    }
    past = {
        "cache_fox_k": cache_fox_k, "cache_fox_v": cache_fox_v, "cache_fox_logf": cache_fox_logf,
        "state_pool": state_pool, "cache_nsa_kc": cache_nsa_kc, "cache_nsa_vc": cache_nsa_vc,
        "cache_nsa_ks": cache_nsa_ks, "cache_nsa_vs": cache_nsa_vs,
        "state_nsa_kw": state_nsa_kw, "state_nsa_vw": state_nsa_vw,
    }
    n_prompt = c_prompt.shape[0]
    mod = _modulation(jnp.concatenate([c_prompt, c_sample], axis=0), w_mod, b_mod)
    layers = [_layer_params(params, layer) for layer in range(w_mod.shape[0])]
    y_prompt, sp = _prompt_trunk(x_prompt, mod[:, :n_prompt], layers)
    y_sample, ss = _sample_trunk(x_sample, mod[:, n_prompt:], layers, past, page_table)
    return (y_prompt, y_sample, *[sp[n][None] for n in _STATE_NAMES], *[ss[n][None] for n in _STATE_NAMES])
```

```python
import functools

import jax
import jax.numpy as jnp
from jax import lax
from jax.experimental import pallas as pl
from jax.experimental.pallas import tpu as pltpu

F32 = jnp.float32
BF16 = jnp.bfloat16

D_MODEL = 1024
HEAD_DIM = 64
EPS = 1e-6
ROPE_THETA = 10000.0
NEG_INF = -1e30
MASKED = -2e30
BIG = 1e9
LOG2E = 1.4426950408889634
PAGE = 128
FOX_HEADS = 8
FOX_WIDTH = FOX_HEADS * HEAD_DIM
POOL_WINDOWS = (2, 4, 8, 16)
POOL_WIDTH = 512
POOL_GROUP_CH = 128
POOL_HALO = 16
NSA_HEADS = 16
NSA_GROUPS = 4
NSA_REP = NSA_HEADS // NSA_GROUPS
NSA_KV_WIDTH = NSA_GROUPS * HEAD_DIM
NSA_BRANCHES = 3
CMP_BLOCK = 32
CMP_STRIDE = 16
SEL_BLOCK = 64
SEL_TOPK = 16
WINDOW = 512
D_FF = 2816
LANE = 128
SUBLANE = 8
GROUPS_PER_TILE = LANE // HEAD_DIM
PAGES_PER_STEP = 32
ATTN_PAGES_PER_STEP = 16
VMEM_LIMIT = 48 * 1024 * 1024


def _params(*sem):
    return pltpu.CompilerParams(dimension_semantics=sem, vmem_limit_bytes=VMEM_LIMIT)


def _nt(a, b):
    return lax.dot_general(a, b, (((1,), (1,)), ((), ())), preferred_element_type=F32)


def _mm(a, b):
    return jnp.dot(a, b, preferred_element_type=F32)


def _mod_body(c_ref, w_ref, b_ref, o_ref):
    c = c_ref[...]
    a = (c * jax.nn.sigmoid(c)).astype(BF16)
    o_ref[0] = _mm(a, w_ref[0].astype(BF16)) + b_ref[0]


def _modulation(c_all, w_mod, b_mod):
    n_layers, d, n = w_mod.shape
    rows = c_all.shape[0]
    tn = 1536
    return pl.pallas_call(
        _mod_body,
        grid=(n_layers, n // tn),
        in_specs=[
            pl.BlockSpec((rows, d), lambda l, j: (0, 0)),
            pl.BlockSpec((1, d, tn), lambda l, j: (l, 0, j)),
            pl.BlockSpec((1, 1, tn), lambda l, j: (l, 0, j)),
        ],
        out_specs=pl.BlockSpec((1, rows, tn), lambda l, j: (l, 0, j)),
        out_shape=jax.ShapeDtypeStruct((n_layers, rows, n), F32),
        compiler_params=_params("arbitrary", "arbitrary"),
        name="modulation",
    )(c_all, w_mod, b_mod.reshape(n_layers, 1, n))


def _modulated_norm(x, gain, sc, sh):
    xn = x * lax.rsqrt(jnp.mean(x * x, axis=-1, keepdims=True) + EPS) * gain
    return xn * (1.0 + sc) + sh


def _proj_body(*refs, has_side, has_t):
    x_ref, g_ref, sc_ref, sh_ref, w_ref = refs[:5]
    k = 5
    side_ref = t_ref = None
    if has_side:
        side_ref = refs[k]; k += 1
    if has_t:
        t_ref = refs[k]; k += 1
    y_ref = refs[k]; k += 1
    hb = _modulated_norm(x_ref[0], g_ref[...], sc_ref[0], sh_ref[0]).astype(BF16)
    y_ref[0] = _mm(hb, w_ref[...])
    if has_side:
        refs[k][0] = _mm(hb, side_ref[...]); k += 1
    if has_t:
        refs[k][0] = _nt(t_ref[...], hb)


def _resident(shape):
    return pl.BlockSpec(shape, lambda *_: (0,) * len(shape), pipeline_mode=pl.Buffered(1))


def _project(x, gain, sc, sh, w, tm, w_side=None, w_t=None):
    bx, tx, d = x.shape
    n_cols = w.shape[1]
    r = sc.shape[1]
    mod_spec = (pl.BlockSpec((1, tm, d), lambda b, i: (b, i, 0)) if r == tx
                else pl.BlockSpec((1, 1, d), lambda b, i: (b, 0, 0)))
    in_specs = [pl.BlockSpec((1, tm, d), lambda b, i: (b, i, 0)), _resident((1, d)), mod_spec, mod_spec,
                _resident((d, n_cols))]
    args = [x, gain.reshape(1, d), sc, sh, w]
    out_specs = [pl.BlockSpec((1, tm, n_cols), lambda b, i: (b, i, 0))]
    out_shape = [jax.ShapeDtypeStruct((bx, tx, n_cols), F32)]
    if w_side is not None:
        ns = w_side.shape[1]
        in_specs.append(_resident((d, ns)))
        args.append(w_side)
        out_specs.append(pl.BlockSpec((1, tm, ns), lambda b, i: (b, i, 0)))
        out_shape.append(jax.ShapeDtypeStruct((bx, tx, ns), F32))
    if w_t is not None:
        nt = w_t.shape[0]
        in_specs.append(_resident((nt, d)))
        args.append(w_t)
        out_specs.append(pl.BlockSpec((1, nt, tm), lambda b, i: (b, 0, i)))
        out_shape.append(jax.ShapeDtypeStruct((bx, nt, tx), F32))
    return pl.pallas_call(
        functools.partial(_proj_body, has_side=w_side is not None, has_t=w_t is not None),
        grid=(bx, tx // tm),
        in_specs=in_specs, out_specs=out_specs, out_shape=out_shape,
        compiler_params=_params("arbitrary", "arbitrary"),
        name="project",
    )(*args)


def _headnorm_body(*refs, tiles, rope, has_gate, n_out):
    y_ref, gain_ref = refs[:2]
    k = 2
    cos_ref = sin_ref = gl_ref = bg_ref = None
    if rope:
        cos_ref, sin_ref = refs[k], refs[k + 1]; k += 2
    if has_gate:
        gl_ref, bg_ref = refs[k], refs[k + 1]; k += 2
    outs = refs[k:k + n_out]
    gate_out = refs[k + n_out] if has_gate else None
    lane = lax.broadcasted_iota(jnp.int32, (1, LANE), 1)
    low_head = lane < HEAD_DIM
    first_half = (lane % HEAD_DIM) < (HEAD_DIM // 2)
    for src, oi, dst in tiles:
        y = y_ref[0, :, src * LANE:(src + 1) * LANE]
        y2 = y * y
        s_lo = jnp.sum(jnp.where(low_head, y2, 0.0), axis=-1, keepdims=True)
        s_hi = jnp.sum(jnp.where(low_head, 0.0, y2), axis=-1, keepdims=True)
        ms = jnp.where(low_head, s_lo, s_hi) * (1.0 / HEAD_DIM)
        yn = y * lax.rsqrt(ms + EPS) * gain_ref[:, src * LANE:(src + 1) * LANE]
        if rope:
            partner = jnp.where(first_half, pltpu.roll(yn, LANE - HEAD_DIM // 2, 1), pltpu.roll(yn, HEAD_DIM // 2, 1))
            yn = yn * cos_ref[...] + partner * sin_ref[...]
        outs[oi][0, :, dst * LANE:(dst + 1) * LANE] = yn
    if has_gate:
        gate_out[0] = jax.nn.sigmoid(gl_ref[0] + bg_ref[...])


def _headnorm(y, width, gains, tiles, out_widths, tm, rope_tabs=None, gate_logits=None, gate_bias=None):
    bx, tx, _ = y.shape
    rope = rope_tabs is not None
    has_gate = gate_logits is not None
    in_specs = [pl.BlockSpec((1, tm, width), lambda b, i: (b, i, 0)),
                pl.BlockSpec((1, width), lambda b, i: (0, 0))]
    args = [y, gains]
    if rope:
        in_specs += [pl.BlockSpec((tm, LANE), lambda b, i: (i, 0))] * 2
        args += list(rope_tabs)
    if has_gate:
        in_specs += [pl.BlockSpec((1, tm, LANE), lambda b, i: (b, i, 0)), pl.BlockSpec((1, LANE), lambda b, i: (0, 0))]
        args += [gate_logits, gate_bias]
    out_specs = [pl.BlockSpec((1, tm, w), lambda b, i: (b, i, 0)) for w in out_widths]
    out_shape = [jax.ShapeDtypeStruct((bx, tx, w), F32) for w in out_widths]
    if has_gate:
        out_specs.append(pl.BlockSpec((1, tm, LANE), lambda b, i: (b, i, 0)))
        out_shape.append(jax.ShapeDtypeStruct((bx, tx, LANE), F32))
    return pl.pallas_call(
        functools.partial(_headnorm_body, tiles=tuple(tiles), rope=rope, has_gate=has_gate, n_out=len(out_widths)),
        grid=(bx, tx // tm),
        in_specs=in_specs, out_specs=out_specs, out_shape=out_shape,
        compiler_params=_params("arbitrary", "arbitrary"),
        name="headnorm",
    )(*args)


def _rope_tables(pos):
    half = HEAD_DIM // 2
    inv_freq = ROPE_THETA ** (-jnp.arange(half, dtype=F32) / half)
    ang = pos.astype(F32)[:, None] * inv_freq[None, :]
    cos, sin = jnp.cos(ang), jnp.sin(ang)
    reps = LANE // HEAD_DIM
    return (jnp.tile(jnp.concatenate([cos, cos], axis=1), (1, reps)),
            jnp.tile(jnp.concatenate([-sin, sin], axis=1), (1, reps)))


def _lane_cumsum(x):
    n = x.shape[-1]
    lane = lax.broadcasted_iota(jnp.int32, x.shape, x.ndim - 1)
    s = 1
    while s < n:
        x = x + jnp.where(lane >= s, pltpu.roll(x, s, x.ndim - 1), 0.0)
        s *= 2
    return x


def _logf_body(f_ref, b_ref, lf_ref, cum_ref):
    lf = jax.nn.log_sigmoid(f_ref[0] + b_ref[...])
    lf_ref[0] = lf
    cum_ref[0] = _lane_cumsum(lf)


def _logf_cumsum(f_t, b_fgate):
    bx, h, t = f_t.shape
    spec = pl.BlockSpec((1, h, t), lambda b: (b, 0, 0))
    return pl.pallas_call(
        _logf_body, grid=(bx,),
        in_specs=[spec, pl.BlockSpec((h, 1), lambda b: (0, 0))],
        out_specs=[spec, spec],
        out_shape=[jax.ShapeDtypeStruct(f_t.shape, F32)] * 2,
        compiler_params=_params("arbitrary"),
        name="logf_cumsum",
    )(f_t, b_fgate.reshape(h, 1))


def _half_mask(shape, half):
    lane = lax.broadcasted_iota(jnp.int32, shape, len(shape) - 1)
    return (lane % LANE) // HEAD_DIM == half


def _fox_flash_body(q_ref, k_ref, v_ref, cum_ref, o_ref, qm_scr, m_scr, l_scr, acc_scr, *, tq, nk, scale):
    qi, ki = pl.program_id(1), pl.program_id(2)
    n_pairs = FOX_HEADS // 2
    reps = tq // LANE

    @pl.when(ki == 0)
    def _():
        for h in range(FOX_HEADS):
            c = h // 2
            qpair = q_ref[0, :, c * LANE:(c + 1) * LANE] * scale
            qm_scr[h] = jnp.where(_half_mask(qpair.shape, h % 2), qpair, 0.0).astype(BF16)
        m_scr[...] = jnp.full(m_scr.shape, NEG_INF, F32)
        l_scr[...] = jnp.zeros(l_scr.shape, F32)
        acc_scr[...] = jnp.zeros(acc_scr.shape, F32)

    def update(diagonal):
        if diagonal:
            vis = (lax.broadcasted_iota(jnp.int32, (tq, tq), 1) <= lax.broadcasted_iota(jnp.int32, (tq, tq), 0))
        for c in range(n_pairs):
            k_pair = k_ref[0, :, c * LANE:(c + 1) * LANE].astype(BF16)
            v_pair = v_ref[0, :, c * LANE:(c + 1) * LANE]
            pv = []
            alphas = []
            for half in range(2):
                h = 2 * c + half
                s = _nt(qm_scr[h], k_pair) - cum_ref[0, h:h + 1, :] * LOG2E
                if diagonal:
                    s = jnp.where(vis, s, MASKED)
                m_prev = m_scr[h]
                m_new = jnp.maximum(m_prev, jnp.max(s, axis=1, keepdims=True))
                alpha = jnp.exp2(m_prev - m_new)
                p = jnp.exp2(s - jnp.concatenate([m_new] * reps, axis=1))
                l_scr[h] = alpha * l_scr[h] + jnp.sum(p, axis=1, keepdims=True)
                m_scr[h] = m_new
                v_half = jnp.where(_half_mask(v_pair.shape, half), v_pair, 0.0).astype(BF16)
                pv.append(_mm(p.astype(BF16), v_half))
                alphas.append(alpha)
            alpha_pair = jnp.where(_half_mask(alphas[0].shape, 0), alphas[0], alphas[1])
            acc_scr[c] = alpha_pair * acc_scr[c] + (pv[0] + pv[1])

    pl.when(ki < qi)(functools.partial(update, False))
    pl.when(ki == qi)(functools.partial(update, True))

    @pl.when(ki == nk - 1)
    def _():
        for c in range(n_pairs):
            l_pair = jnp.where(_half_mask((tq, LANE), 0), l_scr[2 * c], l_scr[2 * c + 1])
            o_ref[0, :, c * LANE:(c + 1) * LANE] = acc_scr[c] / l_pair


def _fox_flash(qk, v_arr, vcb, cum, tq):
    bx, t, _ = qk.shape
    w = FOX_WIDTH
    nq = t // tq
    kmap = lambda qi, ki: jnp.minimum(ki, qi)
    return pl.pallas_call(
        functools.partial(_fox_flash_body, tq=tq, nk=nq, scale=HEAD_DIM ** -0.5 * LOG2E),
        grid=(bx, nq, nq),
        in_specs=[pl.BlockSpec((1, tq, w), lambda b, qi, ki: (b, qi, 0)),
                  pl.BlockSpec((1, tq, w), lambda b, qi, ki: (b, kmap(qi, ki), 1)),
                  pl.BlockSpec((1, tq, w), lambda b, qi, ki: (b, kmap(qi, ki), vcb)),
                  pl.BlockSpec((1, FOX_HEADS, tq), lambda b, qi, ki: (b, 0, kmap(qi, ki)))],
        out_specs=pl.BlockSpec((1, tq, w), lambda b, qi, ki: (b, qi, 0)),
        out_shape=jax.ShapeDtypeStruct((bx, t, w), F32),
        scratch_shapes=[pltpu.VMEM((FOX_HEADS, tq, LANE), BF16), pltpu.VMEM((FOX_HEADS, tq, LANE), F32),
                        pltpu.VMEM((FOX_HEADS, tq, LANE), F32), pltpu.VMEM((FOX_HEADS // 2, tq, LANE), F32)],
        compiler_params=_params("arbitrary", "arbitrary", "arbitrary"),
        name="fox_flash",
    )(qk, qk, v_arr, cum)


def _nsa_flash_body(*refs, tq, nk, mode, has_sel, gate_branch, scale):
    q_ref, k_ref, v_ref = refs[:3]
    i = 3
    sel_ref = None
    if has_sel:
        sel_ref = refs[i]; i += 1
    gate_ref, o_ref, qt_scr, m_scr, l_scr, acc_scr = refs[i:i + 6]
    qi, ki = pl.program_id(1), pl.program_id(2)
    kt = ki if mode == "causal" else qi - (nk - 1) + ki
    active = (ki <= qi) if mode == "causal" else (kt >= 0)

    @pl.when(ki == 0)
    def _():
        zeros = jnp.zeros((HEAD_DIM, tq), F32)
        for pair in range(NSA_HEADS // 2):
            q_pair_t = (q_ref[0, :, pair * LANE:(pair + 1) * LANE] * scale).T
            for half in range(2):
                g, r = divmod(2 * pair + half, NSA_REP)
                q_t = q_pair_t[half * HEAD_DIM:(half + 1) * HEAD_DIM]
                parts = [q_t, zeros] if g % GROUPS_PER_TILE == 0 else [zeros, q_t]
                qt_scr[g, 0:LANE, r * tq:(r + 1) * tq] = jnp.concatenate(parts, axis=0).astype(BF16)
        if has_sel:
            nsel = sel_ref.shape[2]
            for g in range(NSA_GROUPS):
                bias = jnp.where(sel_ref[0, g] > 0.5, 0.0, MASKED)
                bias = jnp.concatenate([bias, jnp.zeros((LANE - nsel, tq), F32)], axis=0).astype(BF16)
                qt_scr[g, LANE:2 * LANE, :] = jnp.concatenate([bias] * NSA_REP, axis=1)
        m_scr[...] = jnp.full(m_scr.shape, NEG_INF, F32)
        l_scr[...] = jnp.zeros(l_scr.shape, F32)
        acc_scr[...] = jnp.zeros(acc_scr.shape, F32)

    def update(mask_kind):
        key = lax.broadcasted_iota(jnp.int32, (tq, tq), 0)
        query = lax.broadcasted_iota(jnp.int32, (tq, tq), 1)
        if mask_kind == "causal":
            vis = key <= query
        elif mask_kind == "tail":
            vis = key + (WINDOW - (nk - 1) * tq) > query
        if has_sel:
            kblk = (kt * tq + lax.broadcasted_iota(jnp.int32, (tq, LANE), 0)) // SEL_BLOCK
            expand = (kblk == lax.broadcasted_iota(jnp.int32, (tq, LANE), 1)).astype(BF16)
        logits, v_ts = [], []
        for c in range(NSA_GROUPS // GROUPS_PER_TILE):
            k_pair = k_ref[0, :, c * LANE:(c + 1) * LANE].astype(BF16)
            v_pair_t = v_ref[0, :, c * LANE:(c + 1) * LANE].T
            if has_sel:
                k_pair = jnp.concatenate([k_pair, expand], axis=1)
            for gg in range(GROUPS_PER_TILE):
                g = c * GROUPS_PER_TILE + gg
                v_ts.append(v_pair_t[gg * HEAD_DIM:(gg + 1) * HEAD_DIM].astype(BF16))
                s = _mm(k_pair, qt_scr[g])
                if mask_kind is not None:
                    s = jnp.where(jnp.concatenate([vis] * NSA_REP, axis=1), s, MASKED)
                logits.append(s)
        for g in range(NSA_GROUPS):
            s = logits[g]
            m_prev = m_scr[g]
            m_new = jnp.maximum(m_prev, jnp.max(s, axis=0, keepdims=True))
            alpha = jnp.exp2(m_prev - m_new)
            p = jnp.exp2(s - m_new[0:1])
            l_scr[g] = alpha * l_scr[g] + jnp.sum(p, axis=0, keepdims=True)
            acc_scr[g] = alpha[0:1] * acc_scr[g] + _mm(v_ts[g], p.astype(BF16))
            m_scr[g] = m_new

    last = ki == nk - 1
    if mode == "causal":
        pl.when(ki < qi)(functools.partial(update, None))
        pl.when(ki == qi)(functools.partial(update, "causal"))
    else:
        assert nk >= 2 and WINDOW % tq == 0
        pl.when(active & (ki == 0))(functools.partial(update, "tail"))
        if nk > 2:
            pl.when(active & (ki > 0) & jnp.logical_not(last))(functools.partial(update, None))
        pl.when(last)(functools.partial(update, "causal"))

    @pl.when(ki == nk - 1)
    def _():
        low = _half_mask((tq, LANE), 0)
        for pair in range(NSA_HEADS // 2):
            parts, gate_cols = [], []
            for half in range(2):
                h = 2 * pair + half
                g, r = divmod(h, NSA_REP)
                lanes = slice(r * tq, (r + 1) * tq)
                parts.append(acc_scr[g, :, lanes] / l_scr[g, 0:1, lanes])
                c = h * NSA_BRANCHES + gate_branch
                gate_cols.append(gate_ref[0, :, c:c + 1])
            o_pair = jnp.concatenate(parts, axis=0).T
            o_ref[0, :, pair * LANE:(pair + 1) * LANE] = o_pair * jnp.where(low, gate_cols[0], gate_cols[1])


def _nsa_flash(qn, kn, kcb, y, vcb, gates, gate_branch, tq, mode, sel_t=None):
    bx, t, wq = qn.shape
    wk = NSA_KV_WIDTH
    nq = t // tq
    nk = nq if mode == "causal" else WINDOW // tq + 1
    if mode == "causal":
        kmap = lambda qi, ki: jnp.minimum(ki, qi)
    else:
        kmap = lambda qi, ki: jnp.maximum(qi - (nk - 1) + ki, 0)
    in_specs = [pl.BlockSpec((1, tq, wq), lambda b, qi, ki: (b, qi, 0)),
                pl.BlockSpec((1, tq, wk), lambda b, qi, ki: (b, kmap(qi, ki), kcb)),
                pl.BlockSpec((1, tq, wk), lambda b, qi, ki: (b, kmap(qi, ki), vcb))]
    args = [qn, kn, y]
    if sel_t is not None:
        in_specs.append(pl.BlockSpec((1, NSA_GROUPS, sel_t.shape[2], tq), lambda b, qi, ki: (b, 0, 0, qi)))
        args.append(sel_t)
    in_specs.append(pl.BlockSpec((1, tq, LANE), lambda b, qi, ki: (b, qi, 0)))
    args.append(gates)
    return pl.pallas_call(
        functools.partial(_nsa_flash_body, tq=tq, nk=nk, mode=mode, has_sel=sel_t is not None,
                          gate_branch=gate_branch, scale=HEAD_DIM ** -0.5 * LOG2E),
        grid=(bx, nq, nk),
        in_specs=in_specs,
        out_specs=pl.BlockSpec((1, tq, wq), lambda b, qi, ki: (b, qi, 0)),
        out_shape=jax.ShapeDtypeStruct((bx, t, wq), F32),
        scratch_shapes=[pltpu.VMEM((NSA_GROUPS, (2 if sel_t is not None else 1) * LANE, NSA_REP * tq), BF16),
                        pltpu.VMEM((NSA_GROUPS, SUBLANE, NSA_REP * tq), F32),
                        pltpu.VMEM((NSA_GROUPS, SUBLANE, NSA_REP * tq), F32),
                        pltpu.VMEM((NSA_GROUPS, HEAD_DIM, NSA_REP * tq), F32)],
        compiler_params=_params("arbitrary", "arbitrary", "arbitrary"),
        name="nsa_flash_" + mode,
    )(*args)


def _pool_body(u_ref, pre_ref, map_ref, scale_ref, o_ref, ext_scr, *, tm, pos0):
    j = pl.program_id(1)

    @pl.when(j == 0)
    def _():
        ext_scr[0:POOL_HALO] = pre_ref[0]

    @pl.when(j > 0)
    def _():
        ext_scr[0:POOL_HALO] = ext_scr[tm:tm + POOL_HALO]

    ext_scr[POOL_HALO:POOL_HALO + tm] = u_ref[0]
    qpos = pos0 + j * tm + lax.broadcasted_iota(jnp.int32, (tm, 1), 0)
    for g, w in enumerate(POOL_WINDOWS):
        lo, hi = g * POOL_GROUP_CH, (g + 1) * POOL_GROUP_CH
        u_new = ext_scr[POOL_HALO:POOL_HALO + tm, lo:hi]
        tot = u_new
        for d in range(1, w):
            tot = tot + ext_scr[POOL_HALO - d:POOL_HALO - d + tm, lo:hi]
        count = jnp.minimum(w, qpos + 1).astype(F32)
        diff = tot / count - u_new
        y = _mm(diff.astype(BF16), map_ref[g].astype(BF16))
        o_ref[0, :, lo:hi] = y * scale_ref[:, lo:hi]


def _pool_mixer(u_arr, ucb, prefix, w_map, scale, pos0, tm):
    bx, t, _ = u_arr.shape
    c = POOL_WIDTH
    return pl.pallas_call(
        functools.partial(_pool_body, tm=tm, pos0=pos0),
        grid=(bx, t // tm),
        in_specs=[pl.BlockSpec((1, tm, c), lambda b, j: (b, j, ucb)),
                  pl.BlockSpec((1, POOL_HALO, c), lambda b, j: (b, 0, 0)),
                  pl.BlockSpec((len(POOL_WINDOWS), POOL_GROUP_CH, POOL_GROUP_CH), lambda b, j: (0, 0, 0)),
                  pl.BlockSpec((1, c), lambda b, j: (0, 0))],
        out_specs=pl.BlockSpec((1, tm, c), lambda b, j: (b, j, 0)),
        out_shape=jax.ShapeDtypeStruct((bx, t, c), F32),
        scratch_shapes=[pltpu.VMEM((POOL_HALO + tm, c), F32)],
        compiler_params=_params("arbitrary", "arbitrary"),
        name="pool_mixer",
    )(u_arr, prefix, w_map, scale.reshape(1, c))


def _outproj_body(*refs, group_sizes):
    n_a = sum(group_sizes)
    a_refs = refs[:n_a]
    w_refs = refs[n_a:n_a + len(group_sizes)]
    res_ref, gate_ref, o_ref = refs[n_a + len(group_sizes):]
    y = None
    k = 0
    for gi, n in enumerate(group_sizes):
        a = a_refs[k][0]
        for r in a_refs[k + 1:k + n]:
            a = a + r[0]
        k += n
        part = _mm(a.astype(BF16), w_refs[gi][...])
        y = part if y is None else y + part
    o_ref[0] = res_ref[0] + gate_ref[0] * y


def _out_project(groups, w, res, gate, tm):
    bx, tx, d = res.shape
    kg = groups[0][0].shape[-1]
    r = gate.shape[1]
    row = pl.BlockSpec((1, tm, d), lambda b, i: (b, i, 0))
    a_spec = pl.BlockSpec((1, tm, kg), lambda b, i: (b, i, 0))
    in_specs, args = [], []
    for grp in groups:
        for a in grp:
            in_specs.append(a_spec); args.append(a)
    for gi in range(len(groups)):
        in_specs.append(pl.BlockSpec((kg, d), lambda b, i, gi=gi: (gi, 0), pipeline_mode=pl.Buffered(1)))
        args.append(w)
    in_specs += [row, row if r == tx else pl.BlockSpec((1, 1, d), lambda b, i: (b, 0, 0))]
    args += [res, gate]
    return pl.pallas_call(
        functools.partial(_outproj_body, group_sizes=tuple(len(g) for g in groups)),
        grid=(bx, tx // tm),
        in_specs=in_specs,
        out_specs=row,
        out_shape=jax.ShapeDtypeStruct((bx, tx, d), F32),
        compiler_params=_params("arbitrary", "arbitrary"),
        name="out_project",
    )(*args)


FFN_SLICES = 2


def _ffn_body(x_ref, g_ref, sc_ref, sh_ref, gate_ref, wg_ref, wu_ref, wd_ref, o_ref):
    x = x_ref[0]
    h = _modulated_norm(x, g_ref[...], sc_ref[0], sh_ref[0]).astype(BF16)
    width = D_FF // FFN_SLICES
    y = None
    for c in range(FFN_SLICES):
        cols = slice(c * width, (c + 1) * width)
        gt = _mm(h, wg_ref[:, cols])
        up = _mm(h, wu_ref[:, cols])
        act = ((gt * jax.nn.sigmoid(gt)) * up).astype(BF16)
        part = _mm(act, wd_ref[cols, :])
        y = part if y is None else y + part
    o_ref[0] = x + gate_ref[0] * y


def _ffn(x, gain, sc, sh, gate, w_up, w_down, tm):
    bx, tx, d = x.shape
    r = sc.shape[1]
    mod_spec = (pl.BlockSpec((1, tm, d), lambda b, i: (b, i, 0)) if r == tx
                else pl.BlockSpec((1, 1, d), lambda b, i: (b, 0, 0)))
    x_spec = pl.BlockSpec((1, tm, d), lambda b, i: (b, i, 0))
    half = lambda j: pl.BlockSpec((d, D_FF), lambda b, i: (0, j), pipeline_mode=pl.Buffered(1))
    return pl.pallas_call(
        _ffn_body,
        grid=(bx, tx // tm),
        in_specs=[x_spec, _resident((1, d)), mod_spec, mod_spec, mod_spec, half(0), half(1), _resident((D_FF, d))],
        out_specs=x_spec,
        out_shape=jax.ShapeDtypeStruct(x.shape, F32),
        compiler_params=_params("arbitrary", "arbitrary"),
        name="swiglu",
    )(x, gain.reshape(1, d), sc, sh, gate, w_up, w_up, w_down)


def _chunk_rows_body(a_ref, o_ref, *, n_chunks):
    for s in range(CMP_STRIDE):
        rows = a_ref[0, pl.ds(s, n_chunks, stride=CMP_STRIDE), :]
        for g in range(GROUPS_PER_TILE):
            o_ref[0, g, :, s * HEAD_DIM:(s + 1) * HEAD_DIM] = rows[:, g * HEAD_DIM:(g + 1) * HEAD_DIM]


def _chunk_rows(a, acb, tm):
    bx, t, _ = a.shape
    n_chunks = tm // CMP_STRIDE
    tiles = NSA_KV_WIDTH // LANE
    return pl.pallas_call(
        functools.partial(_chunk_rows_body, n_chunks=n_chunks),
        grid=(bx, t // tm, tiles),
        in_specs=[pl.BlockSpec((1, tm, LANE), lambda b, i, c: (b, i, acb * tiles + c))],
        out_specs=pl.BlockSpec((1, GROUPS_PER_TILE, n_chunks, CMP_STRIDE * HEAD_DIM), lambda b, i, c: (b, c, i, 0)),
        out_shape=jax.ShapeDtypeStruct((bx, NSA_GROUPS, t // CMP_STRIDE, CMP_STRIDE * HEAD_DIM), F32),
        compiler_params=_params("arbitrary", "arbitrary", "arbitrary"),
        name="chunk_rows",
    )(a)


def _compress_mlp(a, pos_ref, w1_ref, w2_ref):
    half = CMP_STRIDE * HEAD_DIM
    w1 = w1_ref[...].astype(BF16)
    first = _mm(a, w1[:half])
    second = _mm(a, w1[half:])
    bias = _mm(pos_ref[...].astype(BF16), w1)
    hidden = (first + pltpu.roll(second, a.shape[0] - 1, 0)) + bias
    return _mm(jax.nn.gelu(hidden).astype(BF16), w2_ref[...].astype(BF16))


def _compress_body(a_ref, pos_ref, w1_ref, w2_ref, o_ref):
    o_ref[0, 0] = _compress_mlp(a_ref[0, 0].astype(BF16), pos_ref, w1_ref, w2_ref)


def _compress(chunks, pos_emb, w1, w2):
    bx, g, n_chunks, half = chunks.shape
    hidden = w1.shape[1]
    return pl.pallas_call(
        _compress_body,
        grid=(bx, g),
        in_specs=[pl.BlockSpec((1, 1, n_chunks, half), lambda b, i: (b, i, 0, 0)),
                  pl.BlockSpec((1, 2 * half), lambda b, i: (0, 0)),
                  pl.BlockSpec((2 * half, hidden), lambda b, i: (0, 0)),
                  pl.BlockSpec((hidden, HEAD_DIM), lambda b, i: (0, 0))],
        out_specs=pl.BlockSpec((1, 1, n_chunks, HEAD_DIM), lambda b, i: (b, i, 0, 0)),
        out_shape=jax.ShapeDtypeStruct((bx, g, n_chunks, HEAD_DIM), F32),
        compiler_params=_params("arbitrary", "arbitrary"),
        name="compress",
    )(chunks, pos_emb.reshape(1, 2 * half), w1, w2)


def _compress_pages_body(pt_ref, *rest, n_steps, pps):
    del pt_ref
    page_refs = rest[:pps]
    pos_ref, w1_ref, w2_ref, o_ref, tok_scr, chunk_scr = rest[pps:]
    p = pl.program_id(1)
    cpp = PAGE // CMP_STRIDE
    pairs = NSA_GROUPS // GROUPS_PER_TILE
    for i in range(pps):
        row0 = pl.multiple_of((p * pps + i) * cpp, cpp)
        for c in range(pairs):
            tok = tok_scr.at[i * pairs + c]
            tok[...] = page_refs[i][0, c * GROUPS_PER_TILE:(c + 1) * GROUPS_PER_TILE].reshape(LANE, PAGE).T
            for s in range(CMP_STRIDE):
                rows = tok[pl.ds(s, cpp, stride=CMP_STRIDE), :]
                for g in range(GROUPS_PER_TILE):
                    chunk_scr[c * GROUPS_PER_TILE + g, pl.ds(row0, cpp), s * HEAD_DIM:(s + 1) * HEAD_DIM] = (
                        rows[:, g * HEAD_DIM:(g + 1) * HEAD_DIM])

    @pl.when(p == n_steps - 1)
    def _():
        for g in range(NSA_GROUPS):
            o_ref[0, :, g * HEAD_DIM:(g + 1) * HEAD_DIM] = _compress_mlp(
                chunk_scr[g].astype(BF16), pos_ref, w1_ref, w2_ref)


def _compress_pages(cache_t, page_table, pos_emb, w1, w2):
    bx, n_pages = page_table.shape
    pps = _pages_per_step(n_pages)
    cpp = PAGE // CMP_STRIDE
    n_chunks = n_pages * cpp
    half = CMP_STRIDE * HEAD_DIM
    hidden = w1.shape[1]
    const = lambda b, p, pt: (0, 0)
    return pl.pallas_call(
        functools.partial(_compress_pages_body, n_steps=n_pages // pps, pps=pps),
        grid_spec=pltpu.PrefetchScalarGridSpec(
            num_scalar_prefetch=1, grid=(bx, n_pages // pps),
            in_specs=[pl.BlockSpec((1, NSA_GROUPS, HEAD_DIM, PAGE), _page_map(n_pages, pps, i, 4)) for i in range(pps)]
            + [pl.BlockSpec((1, 2 * half), const), pl.BlockSpec((2 * half, hidden), const),
               pl.BlockSpec((hidden, HEAD_DIM), const)],
            out_specs=pl.BlockSpec((1, n_chunks, NSA_KV_WIDTH), lambda b, p, pt: (b, 0, 0)),
            scratch_shapes=[pltpu.VMEM((pps * NSA_GROUPS // GROUPS_PER_TILE, PAGE, LANE), F32),
                            pltpu.VMEM((NSA_GROUPS, n_chunks, half), F32)]),
        out_shape=jax.ShapeDtypeStruct((bx, n_chunks, NSA_KV_WIDTH), F32),
        compiler_params=_params("arbitrary", "arbitrary"),
        name="compress_pages",
    )(page_table.reshape(-1), *([cache_t] * pps), pos_emb.reshape(1, 2 * half), w1, w2)


def _bf16_terms(x):
    hi = x.astype(BF16)
    r1 = x - hi.astype(F32)
    mid = r1.astype(BF16)
    lo = (r1 - mid.astype(F32)).astype(BF16)
    return hi, mid, lo


def _cmp_select_body(q_ref, kc_ref, vc_ref, gate_ref, o_ref, sel_ref, qbd_scr, *, tq, n_cmp, n_sel, pos0, scale):
    ncp = kc_ref.shape[1]
    nselp = sel_ref.shape[-1]
    gt = NSA_GROUPS * tq
    rows = NSA_REP * gt
    qbd_scr[...] = jnp.zeros(qbd_scr.shape, F32)
    for h in range(NSA_HEADS):
        g, r = divmod(h, NSA_REP)
        qbd_scr[r * gt + g * tq:r * gt + (g + 1) * tq, g * HEAD_DIM:(g + 1) * HEAD_DIM] = (
            q_ref[0, :, h * HEAD_DIM:(h + 1) * HEAD_DIM] * scale)
    row_pos = pos0 + lax.broadcasted_iota(jnp.int32, (rows, 1), 0) % tq
    c_idx = lax.broadcasted_iota(jnp.int32, (1, ncp), 1)
    c_valid = (c_idx * CMP_STRIDE + (CMP_BLOCK - 1) <= row_pos) & (c_idx < n_cmp)
    s = jnp.where(c_valid, _nt(qbd_scr[...].astype(BF16), kc_ref[0].astype(BF16)), NEG_INF)
    m = jnp.max(s, axis=-1, keepdims=True)
    p = jnp.where(c_valid, jnp.exp(s - m), 0.0)
    l = jnp.sum(p, axis=-1, keepdims=True)
    pc = jnp.where(l > 0.0, p / jnp.where(l > 0.0, l, 1.0), 0.0)
    o = _mm(pc.astype(BF16), vc_ref[0].astype(BF16))
    for h in range(NSA_HEADS):
        g, r = divmod(h, NSA_REP)
        c = h * NSA_BRANCHES
        o_ref[0, :, h * HEAD_DIM:(h + 1) * HEAD_DIM] = (
            o[r * gt + g * tq:r * gt + (g + 1) * tq, g * HEAD_DIM:(g + 1) * HEAD_DIM] * gate_ref[0, :, c:c + 1])
    pc_sum = pc[0:gt]
    for r in range(1, NSA_REP):
        pc_sum = pc_sum + pc[r * gt:(r + 1) * gt]
    cj = lax.broadcasted_iota(jnp.int32, (ncp, nselp), 0) * CMP_STRIDE
    sj = lax.broadcasted_iota(jnp.int32, (ncp, nselp), 1) * SEL_BLOCK
    overlap = ((cj < sj + SEL_BLOCK) & (cj + (CMP_BLOCK - 1) >= sj)).astype(BF16)
    terms = _mm(jnp.concatenate(_bf16_terms(pc_sum), axis=0), overlap)
    imp = (terms[0:gt] + terms[gt:2 * gt]) + terms[2 * gt:3 * gt]
    gpos = pos0 + lax.broadcasted_iota(jnp.int32, (gt, 1), 0) % tq
    j_idx = lax.broadcasted_iota(jnp.int32, (1, nselp), 1)
    forced = (j_idx == 0) | (j_idx == gpos // SEL_BLOCK)
    valid = j_idx * SEL_BLOCK <= gpos
    score = jnp.where(valid, jnp.where(forced, BIG, imp), -BIG)
    rank = jnp.zeros(score.shape, jnp.int32)
    for i in range(n_sel):
        si = score[:, i:i + 1]
        ahead = (si > score) | ((si == score) & (i < j_idx))
        rank = rank + ahead.astype(jnp.int32)
    sel_ref[0] = ((rank < SEL_TOPK) & valid).astype(F32)


def _cmp_select(qn, kcmp, vcmp, gates, n_cmp, n_sel, nselp, pos0):
    bx, t, wq = qn.shape
    ncp = kcmp.shape[1]
    assert (NSA_GROUPS * t) % (2 * SUBLANE) == 0
    whole = lambda b: (b, 0, 0)
    cmp_spec = pl.BlockSpec((1, ncp, NSA_KV_WIDTH), whole)
    return pl.pallas_call(
        functools.partial(_cmp_select_body, tq=t, n_cmp=n_cmp, n_sel=n_sel, pos0=pos0, scale=HEAD_DIM ** -0.5),
        grid=(bx,),
        in_specs=[pl.BlockSpec((1, t, wq), whole), cmp_spec, cmp_spec, pl.BlockSpec((1, t, LANE), whole)],
        out_specs=[pl.BlockSpec((1, t, wq), whole), pl.BlockSpec((1, NSA_GROUPS * t, nselp), whole)],
        out_shape=[jax.ShapeDtypeStruct((bx, t, wq), F32),
                   jax.ShapeDtypeStruct((bx, NSA_GROUPS * t, nselp), F32)],
        scratch_shapes=[pltpu.VMEM((NSA_HEADS * t, NSA_KV_WIDTH), F32)],
        compiler_params=_params("arbitrary"),
        name="cmp_select",
    )(qn, kcmp, vcmp, gates)


def _cmp_select_t_body(q_ref, kc_ref, vc_ref, gate_ref, o_ref, sel_ref, *, tq, n_cmp, n_sel, pos0, scale):
    qi = pl.program_id(1)
    ncp = kc_ref.shape[2]
    nselp = sel_ref.shape[2]
    qpos = pos0 + qi * tq + lax.broadcasted_iota(jnp.int32, (1, tq), 1)
    c_idx = lax.broadcasted_iota(jnp.int32, (ncp, 1), 0)
    c_valid = (c_idx * CMP_STRIDE + (CMP_BLOCK - 1) <= qpos) & (c_idx < n_cmp)
    valid_w = jnp.concatenate([c_valid] * NSA_REP, axis=1)
    sj = lax.broadcasted_iota(jnp.int32, (nselp, ncp), 0) * SEL_BLOCK
    cj = lax.broadcasted_iota(jnp.int32, (nselp, ncp), 1) * CMP_STRIDE
    overlap_t = ((cj < sj + SEL_BLOCK) & (cj + (CMP_BLOCK - 1) >= sj)).astype(F32)
    j_idx = lax.broadcasted_iota(jnp.int32, (nselp, 1), 0)
    forced = (j_idx == 0) | (j_idx == qpos // SEL_BLOCK)
    valid = j_idx * SEL_BLOCK <= qpos
    low = _half_mask((tq, LANE), 0)
    pairs = NSA_REP // GROUPS_PER_TILE
    for g in range(NSA_GROUPS):
        kc = kc_ref[0, g].astype(BF16)
        vc_t = vc_ref[0, g].T.astype(BF16)
        q_t = jnp.concatenate([(q_ref[0, :, (g * pairs + p) * LANE:(g * pairs + p + 1) * LANE] * scale).T
                               for p in range(pairs)], axis=0).astype(BF16)
        q_wide = jnp.concatenate([q_t[r * HEAD_DIM:(r + 1) * HEAD_DIM] for r in range(NSA_REP)], axis=1)
        s = jnp.where(valid_w, _mm(kc, q_wide), NEG_INF)
        m = jnp.max(s, axis=0, keepdims=True)
        p = jnp.where(valid_w, jnp.exp(s - m), 0.0)
        l = jnp.sum(p, axis=0, keepdims=True)
        pc = jnp.where(l > 0.0, p / jnp.where(l > 0.0, l, 1.0), 0.0)
        o_t = _mm(vc_t, pc.astype(BF16))
        pc_sum = pc[:, 0:tq]
        for r in range(1, NSA_REP):
            pc_sum = pc_sum + pc[:, r * tq:(r + 1) * tq]
        for pair in range(pairs):
            h0 = g * NSA_REP + pair * GROUPS_PER_TILE
            c0, c1 = h0 * NSA_BRANCHES, (h0 + 1) * NSA_BRANCHES
            r0 = pair * GROUPS_PER_TILE
            o_pair = jnp.concatenate([o_t[:, r0 * tq:(r0 + 1) * tq], o_t[:, (r0 + 1) * tq:(r0 + 2) * tq]],
                                     axis=0).T
            o_ref[0, :, (g * pairs + pair) * LANE:(g * pairs + pair + 1) * LANE] = o_pair * jnp.where(
                low, gate_ref[0, :, c0:c0 + 1], gate_ref[0, :, c1:c1 + 1])
        imp = jnp.dot(overlap_t, pc_sum, preferred_element_type=F32, precision=lax.Precision.HIGHEST)
        score = jnp.where(valid, jnp.where(forced, BIG, imp), -BIG)
        rank = jnp.zeros((nselp, tq), jnp.int32)
        for i in range(n_sel):
            si = score[i:i + 1, :]
            ahead = (si > score) | ((si == score) & (i < j_idx))
            rank = rank + ahead.astype(jnp.int32)
        sel_ref[0, g] = ((rank < SEL_TOPK) & valid).astype(F32)


def _cmp_select_t(qn, kcmp, vcmp, gates, n_cmp, n_sel, pos0, tq):
    bx, t, wq = qn.shape
    ncp = kcmp.shape[2]
    nselp = -(-n_sel // SUBLANE) * SUBLANE
    cmp_spec = pl.BlockSpec((1, NSA_GROUPS, ncp, HEAD_DIM), lambda b, i: (b, 0, 0, 0))
    return pl.pallas_call(
        functools.partial(_cmp_select_t_body, tq=tq, n_cmp=n_cmp, n_sel=n_sel, pos0=pos0, scale=HEAD_DIM ** -0.5),
        grid=(bx, t // tq),
        in_specs=[pl.BlockSpec((1, tq, wq), lambda b, i: (b, i, 0)), cmp_spec, cmp_spec,
                  pl.BlockSpec((1, tq, LANE), lambda b, i: (b, i, 0))],
        out_specs=[pl.BlockSpec((1, tq, wq), lambda b, i: (b, i, 0)),
                   pl.BlockSpec((1, NSA_GROUPS, nselp, tq), lambda b, i: (b, 0, 0, i))],
        out_shape=[jax.ShapeDtypeStruct((bx, t, wq), F32),
                   jax.ShapeDtypeStruct((bx, NSA_GROUPS, nselp, t), F32)],
        compiler_params=_params("arbitrary", "arbitrary"),
        name="cmp_select_t",
    )(qn, kcmp, vcmp, gates)


def _pages_per_step(n_pages, want=PAGES_PER_STEP):
    pps = min(want, n_pages)
    assert n_pages % pps == 0
    return pps


def _page_map(n_pages, pps, i, rank):
    return lambda b, p, pt: (pt[b * n_pages + p * pps + i],) + (0,) * (rank - 1)


def _softmax_step(s, vis, m_scr, l_scr, acc_scr, pv_fn):
    m_prev = m_scr[...]
    m_new = jnp.maximum(m_prev, jnp.max(s, axis=-1, keepdims=True))
    alpha = jnp.exp(m_prev - m_new)
    p = jnp.exp(s - m_new)
    if vis is not None:
        p = jnp.where(vis, p, 0.0)
    l_scr[...] = alpha * l_scr[...] + jnp.sum(p, axis=-1, keepdims=True)
    acc_scr[...] = alpha * acc_scr[...] + pv_fn(p)
    m_scr[...] = m_new


def _new_rows_step(qbd, k_new, v_new, row_t, bias_cols, m_scr, l_scr, acc_scr, tnew):
    cols = []
    for j in range(tnew):
        sj = jnp.sum(qbd * k_new[j:j + 1, :], axis=-1, keepdims=True)
        if bias_cols is not None:
            sj = sj - bias_cols[j]
        cols.append(jnp.where(row_t >= j, sj, NEG_INF))
    m_prev = m_scr[...]
    m_new = m_prev
    for sj in cols:
        m_new = jnp.maximum(m_new, sj)
    alpha = jnp.exp(m_prev - m_new)
    l = alpha * l_scr[...]
    acc = alpha * acc_scr[...]
    for j, sj in enumerate(cols):
        pj = jnp.where(row_t >= j, jnp.exp(sj - m_new), 0.0)
        l = l + pj
        acc = acc + pj * v_new[j:j + 1, :]
    return acc / l


def _init_softmax(m_scr, l_scr, acc_scr):
    m_scr[...] = jnp.full(m_scr.shape, NEG_INF, F32)
    l_scr[...] = jnp.zeros(l_scr.shape, F32)
    acc_scr[...] = jnp.zeros(acc_scr.shape, F32)


def _fox_sample_body(pt_ref, q_ref, kn_ref, vn_ref, fn_ref, bf_ref, *rest, n_steps, pps, tnew, scale):
    del pt_ref
    kt_refs, vt_refs, lf_refs = rest[:pps], rest[pps:2 * pps], rest[2 * pps:3 * pps]
    o_ref, lfo_ref, qbd_scr, m_scr, l_scr, acc_scr, carry_scr = rest[3 * pps:]
    p = pl.program_id(1)
    nh, w = FOX_HEADS, FOX_WIDTH
    rows = tnew * nh
    head_of_lane = lax.broadcasted_iota(jnp.int32, (nh, w), 1) // HEAD_DIM
    hmask = (head_of_lane == lax.broadcasted_iota(jnp.int32, (nh, w), 0)).astype(F32)

    @pl.when(p == 0)
    def _():
        for t in range(tnew):
            qbd_scr[t * nh:(t + 1) * nh, :] = q_ref[0, t:t + 1, :] * scale * hmask
        _init_softmax(m_scr, l_scr, acc_scr)
        carry_scr[...] = jnp.zeros(carry_scr.shape, F32)

    tri = (lax.broadcasted_iota(jnp.int32, (PAGE, PAGE), 0) <= lax.broadcasted_iota(jnp.int32, (PAGE, PAGE), 1)
           ).astype(BF16)
    lf_all = jnp.concatenate([r[0] for r in lf_refs], axis=0)
    n_lf = pps * nh
    terms = _mm(jnp.concatenate(_bf16_terms(lf_all), axis=0), tri)
    within_all = (terms[0:n_lf] + terms[n_lf:2 * n_lf]) + terms[2 * n_lf:3 * n_lf]
    carry = carry_scr[...]
    cums = []
    for i in range(pps):
        within = within_all[i * nh:(i + 1) * nh]
        cums.append(carry + within)
        carry = carry + within[:, PAGE - 1:PAGE]
    carry_scr[...] = carry
    bias = jnp.concatenate([jnp.concatenate(cums, axis=1)] * tnew, axis=0)
    kt = jnp.concatenate([r[0].reshape(w, PAGE).astype(BF16) for r in kt_refs], axis=1)
    vt = jnp.concatenate([r[0].reshape(w, PAGE).astype(BF16) for r in vt_refs], axis=1)
    s = _mm(qbd_scr[...].astype(BF16), kt) - bias
    _softmax_step(s, None, m_scr, l_scr, acc_scr, lambda pr: _nt(pr.astype(BF16), vt))

    @pl.when(p == n_steps - 1)
    def _():
        lf_new = jax.nn.log_sigmoid(fn_ref[0] + bf_ref[...])
        lfo_ref[0] = lf_new
        run = carry_scr[...]
        bias_cols = []
        for j in range(tnew):
            run = run + lf_new[:, j:j + 1]
            bias_cols.append(jnp.concatenate([run] * tnew, axis=0))
        row_t = lax.broadcasted_iota(jnp.int32, (rows, 1), 0) // nh
        o = _new_rows_step(qbd_scr[...], kn_ref[0], vn_ref[0], row_t, bias_cols, m_scr, l_scr, acc_scr, tnew)
        o = o * jnp.concatenate([hmask] * tnew, axis=0)
        for t in range(tnew):
            o_ref[0, t:t + 1, :] = jnp.sum(o[t * nh:(t + 1) * nh], axis=0, keepdims=True)


def _fox_sample(q, k_new, v_new, f_new, b_fgate, cache_kt, cache_vt, cache_lft, page_table):
    bx, tnew, w = q.shape
    n_pages = page_table.shape[1]
    pps = _pages_per_step(n_pages, ATTN_PAGES_PER_STEP)
    nh = FOX_HEADS
    rows = tnew * nh
    new_spec = pl.BlockSpec((1, tnew, w), lambda b, p, pt: (b, 0, 0))
    kv_specs = [pl.BlockSpec((1, nh, HEAD_DIM, PAGE), _page_map(n_pages, pps, i, 4)) for i in range(pps)]
    lf_specs = [pl.BlockSpec((1, nh, PAGE), _page_map(n_pages, pps, i, 3)) for i in range(pps)]
    return pl.pallas_call(
        functools.partial(_fox_sample_body, n_steps=n_pages // pps, pps=pps, tnew=tnew, scale=HEAD_DIM ** -0.5),
        grid_spec=pltpu.PrefetchScalarGridSpec(
            num_scalar_prefetch=1, grid=(bx, n_pages // pps),
            in_specs=[new_spec, new_spec, new_spec,
                      pl.BlockSpec((1, nh, tnew), lambda b, p, pt: (b, 0, 0)),
                      pl.BlockSpec((nh, 1), lambda b, p, pt: (0, 0))] + kv_specs + kv_specs + lf_specs,
            out_specs=[new_spec, pl.BlockSpec((1, nh, tnew), lambda b, p, pt: (b, 0, 0))],
            scratch_shapes=[pltpu.VMEM((rows, w), F32), pltpu.VMEM((rows, 1), F32), pltpu.VMEM((rows, 1), F32),
                            pltpu.VMEM((rows, w), F32), pltpu.VMEM((nh, 1), F32)]),
        out_shape=[jax.ShapeDtypeStruct((bx, tnew, w), F32), jax.ShapeDtypeStruct((bx, nh, tnew), F32)],
        compiler_params=_params("arbitrary", "arbitrary"),
        name="fox_sample",
    )(page_table.reshape(-1), q, k_new, v_new, f_new, b_fgate.reshape(nh, 1),
      *([cache_kt] * pps), *([cache_vt] * pps), *([cache_lft] * pps))


def _fill_group_queries(qbd_scr, q_ref, tnew, scale):
    qbd_scr[...] = jnp.zeros(qbd_scr.shape, F32)
    for h in range(NSA_HEADS):
        g = h // NSA_REP
        qbd_scr[h * tnew:(h + 1) * tnew, g * HEAD_DIM:(g + 1) * HEAD_DIM] = (
            q_ref[0, :, h * HEAD_DIM:(h + 1) * HEAD_DIM] * scale)


def _write_group_heads(o_ref, o, gate_ref, branch, tnew):
    for h in range(NSA_HEADS):
        g = h // NSA_REP
        c = h * NSA_BRANCHES + branch
        o_ref[0, :, h * HEAD_DIM:(h + 1) * HEAD_DIM] = (
            o[h * tnew:(h + 1) * tnew, g * HEAD_DIM:(g + 1) * HEAD_DIM] * gate_ref[0, :, c:c + 1])


def _sel_sample_body(pt_ref, q_ref, kn_ref, vn_ref, selrows_ref, gate_ref, *rest, n_steps, pps, tnew, scale):
    del pt_ref
    kt_refs, vt_refs = rest[:pps], rest[pps:2 * pps]
    o_ref, qbd_scr, m_scr, l_scr, acc_scr = rest[2 * pps:]
    p = pl.program_id(1)
    rows = NSA_HEADS * tnew
    nselp = selrows_ref.shape[-1]
    keys = pps * PAGE

    @pl.when(p == 0)
    def _():
        _fill_group_queries(qbd_scr, q_ref, tnew, scale)
        _init_softmax(m_scr, l_scr, acc_scr)

    blk_of_lane = (p * keys + lax.broadcasted_iota(jnp.int32, (nselp, keys), 1)) // SEL_BLOCK
    expand = (lax.broadcasted_iota(jnp.int32, (nselp, keys), 0) == blk_of_lane).astype(BF16)
    vis = _mm(selrows_ref[0].astype(BF16), expand) > 0.5
    kt = jnp.concatenate([r[0].reshape(NSA_KV_WIDTH, PAGE).astype(BF16) for r in kt_refs], axis=1)
    vt = jnp.concatenate([r[0].reshape(NSA_KV_WIDTH, PAGE).astype(BF16) for r in vt_refs], axis=1)
    s = jnp.where(vis, _mm(qbd_scr[...].astype(BF16), kt), NEG_INF)
    _softmax_step(s, vis, m_scr, l_scr, acc_scr, lambda pr: _nt(pr.astype(BF16), vt))

    @pl.when(p == n_steps - 1)
    def _():
        row_t = lax.broadcasted_iota(jnp.int32, (rows, 1), 0) % tnew
        o = _new_rows_step(qbd_scr[...], kn_ref[0], vn_ref[0], row_t, None, m_scr, l_scr, acc_scr, tnew)
        _write_group_heads(o_ref, o, gate_ref, 1, tnew)


def _sel_sample(q, k_new, v_new, selrows, gates, cache_kt, cache_vt, page_table):
    bx, tnew, wq = q.shape
    n_pages = page_table.shape[1]
    rows = NSA_HEADS * tnew
    wk = NSA_KV_WIDTH
    pps = _pages_per_step(n_pages)
    fixed = lambda b, p, pt: (b, 0, 0)
    kv_specs = [pl.BlockSpec((1, NSA_GROUPS, HEAD_DIM, PAGE), _page_map(n_pages, pps, i, 4)) for i in range(pps)]
    return pl.pallas_call(
        functools.partial(_sel_sample_body, n_steps=n_pages // pps, pps=pps, tnew=tnew, scale=HEAD_DIM ** -0.5),
        grid_spec=pltpu.PrefetchScalarGridSpec(
            num_scalar_prefetch=1, grid=(bx, n_pages // pps),
            in_specs=[pl.BlockSpec((1, tnew, wq), fixed), pl.BlockSpec((1, tnew, wk), fixed),
                      pl.BlockSpec((1, tnew, wk), fixed), pl.BlockSpec((1, rows, selrows.shape[-1]), fixed),
                      pl.BlockSpec((1, tnew, LANE), fixed)] + kv_specs + kv_specs,
            out_specs=pl.BlockSpec((1, tnew, wq), fixed),
            scratch_shapes=[pltpu.VMEM((rows, wk), F32), pltpu.VMEM((rows, 1), F32), pltpu.VMEM((rows, 1), F32),
                            pltpu.VMEM((rows, wk), F32)]),
        out_shape=jax.ShapeDtypeStruct((bx, tnew, wq), F32),
        compiler_params=_params("arbitrary", "arbitrary"),
        name="sel_sample",
    )(page_table.reshape(-1), q, k_new, v_new, selrows, gates, *([cache_kt] * pps), *([cache_vt] * pps))


def _win_sample_body(q_ref, kn_ref, vn_ref, gate_ref, kt_ref, vt_ref, o_ref, qbd_scr, m_scr, l_scr, acc_scr,
                     *, tnew, wbuf, scale):
    rows = NSA_HEADS * tnew
    _fill_group_queries(qbd_scr, q_ref, tnew, scale)
    _init_softmax(m_scr, l_scr, acc_scr)
    row_t = lax.broadcasted_iota(jnp.int32, (rows, 1), 0) % tnew
    vis = lax.broadcasted_iota(jnp.int32, (rows, wbuf), 1) > row_t + (wbuf - WINDOW)
    kt = kt_ref[0].reshape(NSA_KV_WIDTH, wbuf).astype(BF16)
    vt = vt_ref[0].reshape(NSA_KV_WIDTH, wbuf).astype(BF16)
    s = jnp.where(vis, _mm(qbd_scr[...].astype(BF16), kt), NEG_INF)
    _softmax_step(s, vis, m_scr, l_scr, acc_scr, lambda pr: _nt(pr.astype(BF16), vt))
    o = _new_rows_step(qbd_scr[...], kn_ref[0], vn_ref[0], row_t, None, m_scr, l_scr, acc_scr, tnew)
    _write_group_heads(o_ref, o, gate_ref, 2, tnew)


def _win_sample(q, k_new, v_new, gates, buf_kt, buf_vt):
    bx, tnew, wq = q.shape
    wbuf = buf_kt.shape[-1]
    rows = NSA_HEADS * tnew
    wk = NSA_KV_WIDTH
    fixed = lambda b: (b, 0, 0)
    buf = pl.BlockSpec((1, NSA_GROUPS, HEAD_DIM, wbuf), lambda b: (b, 0, 0, 0))
    return pl.pallas_call(
        functools.partial(_win_sample_body, tnew=tnew, wbuf=wbuf, scale=HEAD_DIM ** -0.5),
        grid=(bx,),
        in_specs=[pl.BlockSpec((1, tnew, wq), fixed), pl.BlockSpec((1, tnew, wk), fixed),
                  pl.BlockSpec((1, tnew, wk), fixed), pl.BlockSpec((1, tnew, LANE), fixed), buf, buf],
        out_specs=pl.BlockSpec((1, tnew, wq), fixed),
        out_shape=jax.ShapeDtypeStruct((bx, tnew, wq), F32),
        scratch_shapes=[pltpu.VMEM((rows, wk), F32), pltpu.VMEM((rows, 1), F32), pltpu.VMEM((rows, 1), F32),
                        pltpu.VMEM((rows, wk), F32)],
        compiler_params=_params("arbitrary"),
        name="win_sample",
    )(q, k_new, v_new, gates, buf_kt, buf_vt)


def _tile_heads(v, n):
    return jnp.tile(v.astype(F32), n)


def _mix_ab(x, mods, pos0, past, p, tiles):
    y, f_t = _project(x, p["norm_mix"], mods["sc1"], mods["sh1"], p["w_in_main"], tiles["tm"], w_t=p["w_in_f_t"])
    gains = jnp.concatenate([_tile_heads(p["ab_q_norm"], FOX_HEADS), _tile_heads(p["ab_k_norm"], FOX_HEADS)])
    n_tiles = 2 * FOX_WIDTH // LANE
    (qk,) = _headnorm(y, 2 * FOX_WIDTH, gains.reshape(1, -1), [(c, 0, c) for c in range(n_tiles)],
                      [2 * FOX_WIDTH], tiles["tm_norm"])
    return y, f_t, qk


def _nsa_project(x, mods, pos_rows, p, tiles):
    n_gate = NSA_BRANCHES * NSA_HEADS
    y, gl = _project(x, p["norm_mix"], mods["sc1"], mods["sh1"], p["w_in_main"], tiles["tm"], w_side=p["w_in_gate"])
    kn3 = p["nsa_k_norm"]
    qw = NSA_HEADS * HEAD_DIM
    gains = jnp.concatenate([
        _tile_heads(p["nsa_q_norm"], NSA_HEADS),
        _tile_heads(kn3[0], NSA_GROUPS), jnp.ones((NSA_KV_WIDTH,), F32),
        _tile_heads(kn3[1], NSA_GROUPS), jnp.ones((NSA_KV_WIDTH,), F32),
        _tile_heads(kn3[2], NSA_GROUPS)])
    width = gains.shape[0]
    qt = qw // LANE
    kt = NSA_KV_WIDTH // LANE
    tile_map = [(c, 0, c) for c in range(qt)]
    for i in range(NSA_BRANCHES):
        tile_map += [(qt + 2 * i * kt + c, 1, i * kt + c) for c in range(kt)]
    b_gate = jnp.pad(p["nsa_b_gate"], (0, LANE - n_gate)).reshape(1, LANE)
    qn, kn, gates = _headnorm(y, width, gains.reshape(1, -1), tile_map, [qw, NSA_BRANCHES * NSA_KV_WIDTH],
                              tiles["tm_norm"], rope_tabs=_rope_tables(pos_rows), gate_logits=gl, gate_bias=b_gate)
    return y, qn, kn, gates


def _layer_params(params, layer):
    e = layer // 2
    p = {"norm_mix": params["norm_mix"][layer], "norm_ffn": params["norm_ffn"][layer],
         "w_up": params["w_up"][layer].astype(BF16), "w_down": params["w_down"][layer].astype(BF16)}
    prefix = "ab_" if layer % 2 == 0 else "nsa_"
    for k, v in params.items():
        if k.startswith(prefix):
            p[k] = v[e]
    if layer % 2 == 0:
        w_in = p["ab_w_in"]
        split_f = 3 * FOX_WIDTH
        p["w_in_main"] = jnp.concatenate([w_in[:, :split_f], w_in[:, split_f + FOX_HEADS:]], axis=1).astype(BF16)
        p["w_in_f_t"] = w_in[:, split_f:split_f + FOX_HEADS].T.astype(BF16)
        p["w_out"] = p["ab_w_out"].astype(BF16)
    else:
        w_in = p["nsa_w_in"]
        n_main = NSA_HEADS * HEAD_DIM + 6 * NSA_KV_WIDTH
        n_gate = NSA_BRANCHES * NSA_HEADS
        p["w_in_main"] = w_in[:, :n_main].astype(BF16)
        p["w_in_gate"] = jnp.pad(w_in[:, n_main:], ((0, 0), (0, LANE - n_gate))).astype(BF16)
        p["w_out"] = p["nsa_w_out"].astype(BF16)
    return p


def _heads(a, n):
    return a.reshape(a.shape[0], a.shape[1], n, HEAD_DIM)


def _prompt_trunk(x, mod, layers):
    bx, t, d = x.shape
    tiles = {"tm": 512, "tm_norm": 256}
    states = {}
    for layer, p in enumerate(layers):
        sh1, sc1, g1, sh2, sc2, g2 = [m[:, None, :] for m in jnp.split(mod[layer], 6, axis=-1)]
        mods = {"sc1": sc1, "sh1": sh1}
        if layer % 2 == 0:
            y, f_t, qk = _mix_ab(x, mods, 0, None, p, tiles)
            lf_t, cum = _logf_cumsum(f_t, p["ab_b_fgate"])
            o_fox = _fox_flash(qk, y, 2, cum, 512)
            o_pool = _pool_mixer(y, 3, jnp.zeros((bx, POOL_HALO, POOL_WIDTH), F32), p["ab_pool_map"],
                                 p["ab_pool_scale"], 0, 512)
            x = _out_project([[o_fox], [o_pool]], p["w_out"], x, g1, tiles["tm"])
            states["fox_k"] = _heads(qk[:, :, FOX_WIDTH:], FOX_HEADS)
            states["fox_v"] = _heads(y[:, :, 2 * FOX_WIDTH:3 * FOX_WIDTH], FOX_HEADS)
            states["fox_logf"] = lf_t.transpose(0, 2, 1)
            states["pool"] = y[:, t - (POOL_HALO - 1):, 3 * FOX_WIDTH:]
        else:
            y, qn, kn, gates = _nsa_project(x, mods, jnp.arange(t), p, tiles)
            qw, kw_ = NSA_HEADS * HEAD_DIM, NSA_KV_WIDTH
            n_chunk = t // CMP_STRIDE
            kcmp = _compress(_chunk_rows(kn, 0, t), p["nsa_cmp_pos_k"], p["nsa_cmp_w1_k"], p["nsa_cmp_w2_k"])
            vcmp = _compress(_chunk_rows(y, (qw + kw_) // kw_, t), p["nsa_cmp_pos_v"], p["nsa_cmp_w1_v"],
                             p["nsa_cmp_w2_v"])
            n_sel = -(-t // SEL_BLOCK)
            o_cmp, sel_t = _cmp_select_t(qn, kcmp, vcmp, gates, n_chunk - 1, n_sel, 0, 256)
            o_sel = _nsa_flash(qn, kn, 1, y, (qw + 3 * kw_) // kw_, gates, 1, 256, "causal", sel_t=sel_t)
            o_win = _nsa_flash(qn, kn, 2, y, (qw + 5 * kw_) // kw_, gates, 2, 256, "window")
            x = _out_project([[o_cmp, o_sel, o_win]], p["w_out"], x, g1, tiles["tm"])
            buf = min(WINDOW, t)
            states["nsa_kc"] = _heads(kn[:, :, :kw_], NSA_GROUPS)
            states["nsa_vc"] = _heads(y[:, :, qw + kw_:qw + 2 * kw_], NSA_GROUPS)
            states["nsa_ks"] = _heads(kn[:, :, kw_:2 * kw_], NSA_GROUPS)
            states["nsa_vs"] = _heads(y[:, :, qw + 3 * kw_:qw + 4 * kw_], NSA_GROUPS)
            states["nsa_kw"] = _heads(kn[:, t - buf:, 2 * kw_:], NSA_GROUPS)
            states["nsa_vw"] = _heads(y[:, t - buf:, qw + 5 * kw_:qw + 6 * kw_], NSA_GROUPS)
        x = _ffn(x, p["norm_ffn"], sc2, sh2, g2, p["w_up"], p["w_down"], tiles["tm"])
    return x, states


def _sample_trunk(x, mod, layers, past, page_table):
    bx, tnew, d = x.shape
    rows = bx * tnew
    n_pages = page_table.shape[1]
    pos0 = n_pages * PAGE
    assert tnew < CMP_STRIDE and pos0 % CMP_STRIDE == 0 and pos0 >= WINDOW
    tiles = {"tm": rows, "tm_norm": rows}
    xf = x.reshape(1, rows, d)
    per_batch = lambda a: a.reshape(bx, tnew, a.shape[-1])
    states = {}
    for layer, p in enumerate(layers):
        e = layer // 2
        sh1, sc1, g1, sh2, sc2, g2 = [jnp.repeat(m, tnew, axis=0)[None] for m in jnp.split(mod[layer], 6, axis=-1)]
        mods = {"sc1": sc1, "sh1": sh1}
        if layer % 2 == 0:
            y, f_t, qk = _mix_ab(xf, mods, pos0, None, p, tiles)
            q_s, k_s = per_batch(qk[0, :, :FOX_WIDTH]), per_batch(qk[0, :, FOX_WIDTH:])
            v_s = per_batch(y[0, :, 2 * FOX_WIDTH:3 * FOX_WIDTH])
            u_s = per_batch(y[0, :, 3 * FOX_WIDTH:])
            f_new = f_t[0].reshape(FOX_HEADS, bx, tnew).transpose(1, 0, 2)
            cache_kt = past["cache_fox_k"][e].transpose(0, 2, 3, 1)
            cache_vt = past["cache_fox_v"][e].transpose(0, 2, 3, 1)
            cache_lft = past["cache_fox_logf"][e].transpose(0, 2, 1)
            o_fox, lf_new = _fox_sample(q_s, k_s, v_s, f_new, p["ab_b_fgate"], cache_kt, cache_vt, cache_lft,
                                        page_table)
            pool_prev = past["state_pool"][e]
            prefix = jnp.pad(pool_prev, ((0, 0), (1, 0), (0, 0)))
            o_pool = _pool_mixer(u_s, 0, prefix, p["ab_pool_map"], p["ab_pool_scale"], pos0, tnew)
            xf = _out_project([[o_fox.reshape(1, rows, -1)], [o_pool.reshape(1, rows, -1)]], p["w_out"], xf, g1, rows)
            states["fox_k"] = _heads(k_s, FOX_HEADS)
            states["fox_v"] = _heads(v_s, FOX_HEADS)
            states["fox_logf"] = lf_new.transpose(0, 2, 1)
            states["pool"] = jnp.concatenate([pool_prev, u_s], axis=1)[:, -(POOL_HALO - 1):]
        else:
            pos_rows = pos0 + jnp.arange(rows) % tnew
            y, qn, kn, gates = _nsa_project(xf, mods, pos_rows, p, tiles)
            qw, kw_ = NSA_HEADS * HEAD_DIM, NSA_KV_WIDTH
            q_s, gates_s = per_batch(qn[0]), per_batch(gates[0])
            kn_s, y_s = per_batch(kn[0]), per_batch(y[0])
            kc_s, ks_s, kwn_s = kn_s[..., :kw_], kn_s[..., kw_:2 * kw_], kn_s[..., 2 * kw_:]
            vc_s, vs_s, vwn_s = (y_s[..., qw + kw_:qw + 2 * kw_], y_s[..., qw + 3 * kw_:qw + 4 * kw_],
                                 y_s[..., qw + 5 * kw_:qw + 6 * kw_])
            d_major = lambda a: a.transpose(0, 2, 3, 1)
            kcmp = _compress_pages(d_major(past["cache_nsa_kc"][e]), page_table, p["nsa_cmp_pos_k"],
                                   p["nsa_cmp_w1_k"], p["nsa_cmp_w2_k"])
            vcmp = _compress_pages(d_major(past["cache_nsa_vc"][e]), page_table, p["nsa_cmp_pos_v"],
                                   p["nsa_cmp_w1_v"], p["nsa_cmp_w2_v"])
            total = pos0 + tnew
            n_cmp = total // CMP_STRIDE - 1
            n_sel = -(-total // SEL_BLOCK)
            nselp = -(-n_sel // (2 * LANE)) * (2 * LANE)
            o_cmp, sel = _cmp_select(q_s, kcmp, vcmp, gates_s, n_cmp, n_sel, nselp, pos0)
            selrows = jnp.repeat(sel.reshape(bx, NSA_GROUPS, tnew, nselp), NSA_REP, axis=1).reshape(
                bx, NSA_HEADS * tnew, nselp)
            o_sel = _sel_sample(q_s, ks_s, vs_s, selrows, gates_s, d_major(past["cache_nsa_ks"][e]),
                                d_major(past["cache_nsa_vs"][e]), page_table)
            kw_prev, vw_prev = past["state_nsa_kw"][e], past["state_nsa_vw"][e]
            o_win = _win_sample(q_s, kwn_s, vwn_s, gates_s, d_major(kw_prev), d_major(vw_prev))
            flat = lambda a: a.reshape(1, rows, -1)
            xf = _out_project([[flat(o_cmp), flat(o_sel), flat(o_win)]], p["w_out"], xf, g1, rows)
            buf = kw_prev.shape[1]
            states["nsa_kc"] = _heads(kc_s, NSA_GROUPS)
            states["nsa_vc"] = _heads(vc_s, NSA_GROUPS)
            states["nsa_ks"] = _heads(ks_s, NSA_GROUPS)
            states["nsa_vs"] = _heads(vs_s, NSA_GROUPS)
            states["nsa_kw"] = jnp.concatenate([kw_prev, _heads(kwn_s, NSA_GROUPS)], axis=1)[:, -buf:]
            states["nsa_vw"] = jnp.concatenate([vw_prev, _heads(vwn_s, NSA_GROUPS)], axis=1)[:, -buf:]
        xf = _ffn(xf, p["norm_ffn"], sc2, sh2, g2, p["w_up"], p["w_down"], rows)
    return xf.reshape(bx, tnew, d), states


_STATE_NAMES = ("fox_k", "fox_v", "fox_logf", "pool", "nsa_kc", "nsa_vc", "nsa_ks", "nsa_vs", "nsa_kw", "nsa_vw")


def kernel(x_prompt, x_sample, cache_fox_k, cache_fox_v, cache_fox_logf, state_pool, cache_nsa_kc, cache_nsa_vc,
           cache_nsa_ks, cache_nsa_vs, state_nsa_kw, state_nsa_vw, page_table, c_prompt, c_sample, w_mod, b_mod,
           norm_mix, norm_ffn, w_up, w_down, ab_w_in, ab_b_fgate, ab_q_norm, ab_k_norm, ab_pool_map, ab_pool_scale,
           ab_w_out, nsa_w_in, nsa_b_gate, nsa_q_norm, nsa_k_norm, nsa_cmp_pos_k, nsa_cmp_w1_k, nsa_cmp_w2_k,
           nsa_cmp_pos_v, nsa_cmp_w1_v, nsa_cmp_w2_v, nsa_w_out):
    assert w_mod.shape[0] == 2, "one forgetting/pooling layer followed by one sparse-attention layer"
    params = {
        "norm_mix": norm_mix, "norm_ffn": norm_ffn, "w_up": w_up, "w_down": w_down,
        "ab_w_in": ab_w_in, "ab_b_fgate": ab_b_fgate, "ab_q_norm": ab_q_norm, "ab_k_norm": ab_k_norm,
        "ab_pool_map": ab_pool_map, "ab_pool_scale": ab_pool_scale, "ab_w_out": ab_w_out,
        "nsa_w_in": nsa_w_in, "nsa_b_gate": nsa_b_gate, "nsa_q_norm": nsa_q_norm, "nsa_k_norm": nsa_k_norm,
        "nsa_cmp_pos_k": nsa_cmp_pos_k, "nsa_cmp_w1_k": nsa_cmp_w1_k, "nsa_cmp_w2_k": nsa_cmp_w2_k,
        "nsa_cmp_pos_v": nsa_cmp_pos_v, "nsa_cmp_w1_v": nsa_cmp_w1_v, "nsa_cmp_w2_v": nsa_cmp_w2_v,
        "nsa_w_out": nsa_w_out,
    }
    past = {
        "cache_fox_k": cache_fox_k, "cache_fox_v": cache_fox_v, "cache_fox_logf": cache_fox_logf,
        "state_pool": state_pool, "cache_nsa_kc": cache_nsa_kc, "cache_nsa_vc": cache_nsa_vc,
        "cache_nsa_ks": cache_nsa_ks, "cache_nsa_vs": cache_nsa_vs,
        "state_nsa_kw": state_nsa_kw, "state_nsa_vw": state_nsa_vw,
    }
    n_prompt = c_prompt.shape[0]
    mod = _modulation(jnp.concatenate([c_prompt, c_sample], axis=0), w_mod, b_mod)
    layers = [_layer_params(params, layer) for layer in range(w_mod.shape[0])]
    y_prompt, sp = _prompt_trunk(x_prompt, mod[:, :n_prompt], layers)
    y_sample, ss = _sample_trunk(x_sample, mod[:, n_prompt:], layers, past, page_table)
    return (y_prompt, y_sample, *[sp[n][None] for n in _STATE_NAMES], *[ss[n][None] for n in _STATE_NAMES])
```
